```python
import math
import jax, jax.numpy as jnp
from jax import lax
import numpy as np

D_MODEL = 2048
BATCH = 1
SEQ = 16384
DEPTH = 4

D_MIX = D_MODEL
MLA_HEADS = 8
QK_NOPE_DIM = 128
QK_ROPE_DIM = 64
V_HEAD_DIM = 128
Q_LORA_RANK = 512
KV_LORA_RANK = 256
D_MLA = MLA_HEADS * V_HEAD_DIM
D_SSM = D_MIX - D_MLA
SSM_GROUP_CH = 16
SSM_GROUPS = D_SSM // SSM_GROUP_CH
SSM_STATE = 64
ROPE_THETA = 10000.0
NORM_EPS = 1e-6
Q_BLOCK = 128
STEP_MIN = 1e-3
STEP_MAX = 1e-1
D_IN = Q_LORA_RANK + KV_LORA_RANK + QK_ROPE_DIM + D_MLA + D_SSM + D_SSM

kernel_name = 'hymba_mla_s5_sandwich'


def _rms_norm(x, g):
    xf = x.astype(jnp.float32)
    inv = lax.rsqrt(jnp.mean(xf * xf, axis=-1, keepdims=True) + NORM_EPS)
    return (xf * inv * g.astype(jnp.float32)).astype(x.dtype)


def _rope_tables(positions):
    inv_freq = ROPE_THETA ** (-jnp.arange(0, QK_ROPE_DIM, 2, dtype=jnp.float32) / QK_ROPE_DIM)
    ang = positions.astype(jnp.float32)[..., None] * inv_freq
    ang = jnp.concatenate([ang, ang], axis=-1)
    return jnp.cos(ang), jnp.sin(ang)


def _apply_rope(x, cos, sin):
    xf = x.astype(jnp.float32)
    x1, x2 = jnp.split(xf, 2, axis=-1)
    rot = jnp.concatenate([-x2, x1], axis=-1)
    return (xf * cos + rot * sin).astype(x.dtype)


def _mla(c_q, c_kv, k_rope, positions, g_q, w_uq, g_kv, w_ukv):
    b, l, _ = c_q.shape
    q = jnp.einsum('blr,rf->blf', _rms_norm(c_q, g_q), w_uq)
    q = q.reshape(b, l, MLA_HEADS, QK_NOPE_DIM + QK_ROPE_DIM)
    q_nope, q_rope = q[..., :QK_NOPE_DIM], q[..., QK_NOPE_DIM:]
    kv = jnp.einsum('blr,rf->blf', _rms_norm(c_kv, g_kv), w_ukv)
    kv = kv.reshape(b, l, MLA_HEADS, QK_NOPE_DIM + V_HEAD_DIM)
    k_nope, v = kv[..., :QK_NOPE_DIM], kv[..., QK_NOPE_DIM:]
    cos, sin = _rope_tables(positions)
    q_rope = _apply_rope(q_rope, cos[:, :, None, :], sin[:, :, None, :])
    k_rope = _apply_rope(k_rope, cos, sin)

    n_blk = l // Q_BLOCK
    qn_blk = q_nope.reshape(b, n_blk, Q_BLOCK, MLA_HEADS, QK_NOPE_DIM).transpose(1, 0, 2, 3, 4)
    qr_blk = q_rope.reshape(b, n_blk, Q_BLOCK, MLA_HEADS, QK_ROPE_DIM).transpose(1, 0, 2, 3, 4)
    key_idx = jnp.arange(l)
    scale = 1.0 / math.sqrt(QK_NOPE_DIM + QK_ROPE_DIM)
    neg = jnp.finfo(jnp.float32).min

    def block(args):
        qn, qr, i = args
        s = (jnp.einsum('bqhd,bkhd->bhqk', qn, k_nope)
             + jnp.einsum('bqhr,bkr->bhqk', qr, k_rope))
        s = s.astype(jnp.float32) * scale
        q_idx = i * Q_BLOCK + jnp.arange(Q_BLOCK)
        causal = key_idx[None, :] <= q_idx[:, None]
        s = jnp.where(causal[None, None], s, neg)
        p = jax.nn.softmax(s, axis=-1).astype(v.dtype)
        return jnp.einsum('bhqk,bkhd->bqhd', p, v)

    o = lax.map(block, (qn_blk, qr_blk, jnp.arange(n_blk)))
    return o.transpose(1, 0, 2, 3, 4).reshape(b, l, D_MLA)


def _ssm_combine(e1, e2):
    a1, b1 = e1
    a2, b2 = e2
    return a1 * a2, a2 * b1 + b2


def _s5(u, a_re, a_im, b_re, b_im, c_re, c_im, d_skip, log_step, w_glu):
    b, l, _ = u.shape
    uf = u.astype(jnp.float32).reshape(b, l, SSM_GROUPS, SSM_GROUP_CH)
    a = lax.complex(a_re.astype(jnp.float32), a_im.astype(jnp.float32))
    step = jnp.exp(log_step.astype(jnp.float32))[:, None]
    a_bar = jnp.exp(step * a)
    b_mat = lax.complex(b_re.astype(jnp.float32), b_im.astype(jnp.float32))
    b_bar = ((a_bar - 1.0) / a)[..., None] * b_mat
    bu = jnp.einsum('gph,blgh->blgp', b_bar, uf.astype(jnp.complex64))
    a_seq = jnp.broadcast_to(a_bar, bu.shape)
    _, states = lax.associative_scan(_ssm_combine, (a_seq, bu), axis=1)
    c_mat = lax.complex(c_re.astype(jnp.float32), c_im.astype(jnp.float32))
    y = (jnp.einsum('ghp,blgp->blgh', c_mat, states).real
         + d_skip.astype(jnp.float32).reshape(SSM_GROUPS, SSM_GROUP_CH) * uf)
    y = jax.nn.gelu(y.reshape(b, l, D_SSM)).astype(u.dtype)
    val, gate = jnp.split(jnp.einsum('blc,cf->blf', y, w_glu), 2, axis=-1)
    return val * jax.nn.sigmoid(gate)


def _layer(x, positions, g_pre, g_post, w_in, g_q, w_uq, g_kv, w_ukv,
           a_re, a_im, b_re, b_im, c_re, c_im, d_skip, log_step, w_glu, w_out):
    h = _rms_norm(x, g_pre)
    proj = jnp.einsum('bld,df->blf', h, w_in)
    s1 = Q_LORA_RANK
    s2 = s1 + KV_LORA_RANK
    s3 = s2 + QK_ROPE_DIM
    s4 = s3 + D_MLA
    s5 = s4 + D_SSM
    c_q, c_kv, k_rope, gate_mla, u_ssm, gate_ssm = jnp.split(proj, [s1, s2, s3, s4, s5], axis=-1)
    mla_out = _mla(c_q, c_kv, k_rope, positions, g_q, w_uq, g_kv, w_ukv)
    ssm_out = _s5(u_ssm, a_re, a_im, b_re, b_im, c_re, c_im, d_skip, log_step, w_glu)
    mixed = jnp.concatenate([mla_out * jax.nn.silu(gate_mla),
                             ssm_out * jax.nn.silu(gate_ssm)], axis=-1)
    out = jnp.einsum('blf,fd->bld', mixed, w_out)
    return x + _rms_norm(out, g_post)


def setup_inputs(seed: int = 0) -> dict:
    key = jax.random.key(seed)
    k = jax.random.split(key, 20)
    f32 = jnp.float32
    x = jax.random.normal(k[0], (BATCH, SEQ, D_MODEL), f32)
    offset = jax.random.randint(k[1], (BATCH, 1), 0, 4096, dtype=jnp.int32)
    positions = (jnp.arange(SEQ, dtype=jnp.int32)[None, :] + offset).astype(jnp.int32)
    norm_pre = 1.0 + 0.1 * jax.random.normal(k[2], (DEPTH, D_MODEL), f32)
    norm_post = 1.0 + 0.1 * jax.random.normal(k[3], (DEPTH, D_MODEL), f32)
    w_in = jax.random.normal(k[4], (DEPTH, D_MODEL, D_IN), f32) * D_MODEL ** -0.5
    q_norm = 1.0 + 0.1 * jax.random.normal(k[5], (DEPTH, Q_LORA_RANK), f32)
    w_uq = jax.random.normal(k[6], (DEPTH, Q_LORA_RANK, MLA_HEADS * (QK_NOPE_DIM + QK_ROPE_DIM)), f32) * Q_LORA_RANK ** -0.5
    kv_norm = 1.0 + 0.1 * jax.random.normal(k[7], (DEPTH, KV_LORA_RANK), f32)
    w_ukv = jax.random.normal(k[8], (DEPTH, KV_LORA_RANK, MLA_HEADS * (QK_NOPE_DIM + V_HEAD_DIM)), f32) * KV_LORA_RANK ** -0.5
    ssm_a_re = -0.5 * jnp.exp(0.05 * jax.random.normal(k[9], (DEPTH, SSM_GROUPS, SSM_STATE), f32))
    ssm_a_im = (math.pi * jnp.arange(SSM_STATE, dtype=f32)[None, None, :]
                + 0.01 * jax.random.normal(k[10], (DEPTH, SSM_GROUPS, SSM_STATE), f32))
    b_scale = (SSM_GROUP_CH ** -0.5) / math.sqrt(2.0)
    c_scale = (SSM_STATE ** -0.5) / math.sqrt(2.0)
    ssm_b_re = jax.random.normal(k[11], (DEPTH, SSM_GROUPS, SSM_STATE, SSM_GROUP_CH), f32) * b_scale
    ssm_b_im = jax.random.normal(k[12], (DEPTH, SSM_GROUPS, SSM_STATE, SSM_GROUP_CH), f32) * b_scale
    ssm_c_re = jax.random.normal(k[13], (DEPTH, SSM_GROUPS, SSM_GROUP_CH, SSM_STATE), f32) * c_scale
    ssm_c_im = jax.random.normal(k[14], (DEPTH, SSM_GROUPS, SSM_GROUP_CH, SSM_STATE), f32) * c_scale
    ssm_d = jax.random.normal(k[15], (DEPTH, D_SSM), f32)
    ssm_log_step = jax.random.uniform(k[16], (DEPTH, SSM_GROUPS), f32,
                                      minval=math.log(STEP_MIN), maxval=math.log(STEP_MAX))
    w_glu = jax.random.normal(k[17], (DEPTH, D_SSM, 2 * D_SSM), f32) * D_SSM ** -0.5
    w_out = jax.random.normal(k[18], (DEPTH, D_MIX, D_MODEL), f32) * D_MIX ** -0.5
    return {'x': x, 'positions': positions, 'norm_pre': norm_pre, 'norm_post': norm_post,
            'w_in': w_in, 'q_norm': q_norm, 'w_uq': w_uq, 'kv_norm': kv_norm, 'w_ukv': w_ukv,
            'ssm_a_re': ssm_a_re, 'ssm_a_im': ssm_a_im, 'ssm_b_re': ssm_b_re, 'ssm_b_im': ssm_b_im,
            'ssm_c_re': ssm_c_re, 'ssm_c_im': ssm_c_im, 'ssm_d': ssm_d, 'ssm_log_step': ssm_log_step,
            'w_glu': w_glu, 'w_out': w_out}


def reference(x, positions, norm_pre, norm_post, w_in, q_norm, w_uq, kv_norm, w_ukv,
              ssm_a_re, ssm_a_im, ssm_b_re, ssm_b_im, ssm_c_re, ssm_c_im, ssm_d,
              ssm_log_step, w_glu, w_out):
    h = x
    for i in range(DEPTH):
        h = _layer(h, positions, norm_pre[i], norm_post[i], w_in[i], q_norm[i], w_uq[i],
                   kv_norm[i], w_ukv[i], ssm_a_re[i], ssm_a_im[i], ssm_b_re[i], ssm_b_im[i],
                   ssm_c_re[i], ssm_c_im[i], ssm_d[i], ssm_log_step[i], w_glu[i], w_out[i])
    return h
```

```python
import functools
import math

import jax
import jax.numpy as jnp
from jax import lax
from jax.experimental import pallas as pl
from jax.experimental.pallas import tpu as pltpu

F32 = jnp.float32
BF16 = jnp.bfloat16

D_MODEL = 2048
HEADS = 8
NOPE = 128
ROPE = 64
QK = NOPE + ROPE
DV = 128
Q_LORA = 512
KV_LORA = 256
D_MLA = HEADS * DV
D_SSM = 1024
GROUP_CH = 16
GROUPS = D_SSM // GROUP_CH
NSTATE = 64
ROPE_THETA = 10000.0
NORM_EPS = 1e-6
LANES = 128
SUBLANES = 8
STATE_TILES = GROUPS * NSTATE // LANES
SLAB_TILES = SUBLANES
N_SLABS = STATE_TILES // SLAB_TILES
W_IN_COLS = Q_LORA + KV_LORA + D_MLA + D_SSM + D_SSM + LANES
VMEM_LIMIT = 56 * 1024 * 1024


def _resident(shape):
    zeros = (0,) * len(shape)
    return pl.BlockSpec(shape, lambda *_: zeros, pipeline_mode=pl.Buffered(1))


def _sigmoid(x):
    return 1.0 / (1.0 + jnp.exp(-x))


def _params(*sem):
    return pltpu.CompilerParams(dimension_semantics=sem, vmem_limit_bytes=VMEM_LIMIT)


def _s5_prep_kernel(are_ref, aim_ref, lstep_ref, bre_ref, bim_ref,
                    abre_ref, abim_ref, bbre_ref, bbim_ref):
    a_re = are_ref[...]
    a_im = aim_ref[...]
    step = jnp.exp(lstep_ref[...])
    mag = jnp.exp(step * a_re)
    ab_re = mag * jnp.cos(step * a_im)
    ab_im = mag * jnp.sin(step * a_im)
    n_re = ab_re - 1.0
    den = a_re * a_re + a_im * a_im
    z_re = (n_re * a_re + ab_im * a_im) / den
    z_im = (ab_im * a_re - n_re * a_im) / den
    b_re = bre_ref[...]
    b_im = bim_ref[...]
    abre_ref[...] = ab_re
    abim_ref[...] = ab_im
    bbre_ref[...] = z_re * b_re - z_im * b_im
    bbim_ref[...] = z_re * b_im + z_im * b_re


def _s5_prep(a_re, a_im, log_step, b_re, b_im):
    depth = a_re.shape[0]
    rows = depth * GROUPS
    cols = NSTATE * GROUP_CH

    def expand(a):
        return jnp.broadcast_to(a[..., None], (depth, GROUPS, NSTATE, GROUP_CH)).reshape(rows, cols)

    step = jnp.broadcast_to(log_step[..., None], (depth, GROUPS, cols)).reshape(rows, cols)
    sds = jax.ShapeDtypeStruct((rows, cols), F32)
    ab_re, ab_im, bb_re, bb_im = pl.pallas_call(
        _s5_prep_kernel, out_shape=(sds, sds, sds, sds), name="s5_prep",
    )(expand(a_re), expand(a_im), step, b_re.reshape(rows, cols), b_im.reshape(rows, cols))
    shape4 = (depth, GROUPS, NSTATE, GROUP_CH)
    ab_re = ab_re.reshape(shape4)[..., 0]
    ab_im = ab_im.reshape(shape4)[..., 0]
    return ab_re, ab_im, bb_re.reshape(shape4), bb_im.reshape(shape4)


def _inproj_kernel(x_ref, gpre_ref, w_ref, gq_ref, gkv_ref,
                   cq_ref, ckv_ref, kr_ref, gm_ref, u_ref, gs_ref):
    x = x_ref[...]
    inv = lax.rsqrt(jnp.mean(x * x, axis=-1, keepdims=True) + NORM_EPS)
    h = (x * inv * gpre_ref[...]).astype(BF16)

    def proj(lo, hi):
        return jnp.dot(h, w_ref[:, lo:hi], preferred_element_type=F32)

    def latent_norm(c, g_ref):
        inv_c = lax.rsqrt(jnp.mean(c * c, axis=-1, keepdims=True) + NORM_EPS)
        return (c * inv_c * g_ref[...]).astype(BF16)

    o = 0
    cq_ref[...] = latent_norm(proj(o, o + Q_LORA), gq_ref)
    o += Q_LORA
    ckv_ref[...] = latent_norm(proj(o, o + KV_LORA), gkv_ref)
    o += KV_LORA
    gm_ref[...] = proj(o, o + D_MLA)
    o += D_MLA
    u_ref[...] = proj(o, o + D_SSM)
    o += D_SSM
    gs_ref[...] = proj(o, o + D_SSM)
    o += D_SSM
    kr_ref[...] = proj(o, o + LANES)


def _inproj(x, g_pre, w_in, g_q, g_kv, tm):
    L = x.shape[0]
    row = lambda n: pl.BlockSpec((tm, n), lambda i: (i, 0))
    out_shape = (
        jax.ShapeDtypeStruct((L, Q_LORA), BF16),
        jax.ShapeDtypeStruct((L, KV_LORA), BF16),
        jax.ShapeDtypeStruct((L, LANES), F32),
        jax.ShapeDtypeStruct((L, D_MLA), F32),
        jax.ShapeDtypeStruct((L, D_SSM), F32),
        jax.ShapeDtypeStruct((L, D_SSM), F32),
    )
    return pl.pallas_call(
        _inproj_kernel,
        grid=(L // tm,),
        in_specs=[row(D_MODEL), _resident((1, D_MODEL)), _resident((D_MODEL, W_IN_COLS)),
                  _resident((1, Q_LORA)), _resident((1, KV_LORA))],
        out_specs=(row(Q_LORA), row(KV_LORA), row(LANES), row(D_MLA), row(D_SSM), row(D_SSM)),
        out_shape=out_shape,
        compiler_params=_params("parallel"),
        name="inproj",
    )(x, g_pre, w_in, g_q, g_kv)


def _qkv_kernel(cq_ref, ckv_ref, kr_ref, posc_ref, posr_ref, fcol_ref, frow_ref,
                wqt_ref, wk_ref, wvt_ref, qt_ref, k_ref, vt_ref, *, q_scale):
    nt = (((1,), (1,)), ((), ()))
    cq = cq_ref[...]
    ckv = ckv_ref[...]
    half = ROPE // 2

    ang_t = fcol_ref[...] * posr_ref[...].astype(F32)
    cos_t = jnp.cos(ang_t)
    sin_t = jnp.sin(ang_t)
    ang = posc_ref[...].astype(F32) * frow_ref[...]
    cos_r = jnp.cos(ang)
    sin_r = jnp.sin(ang)

    kr = kr_ref[:, :ROPE]
    kr_rot = jnp.concatenate([-kr[:, half:], kr[:, :half]], axis=-1)
    k_rope = (kr * cos_r + kr_rot * sin_r).astype(BF16)
    k_nope = jnp.dot(ckv, wk_ref[...], preferred_element_type=F32)

    for h in range(HEADS):
        qt = lax.dot_general(wqt_ref[h], cq, nt, preferred_element_type=F32)
        qr = qt[NOPE:]
        qr_rot = jnp.concatenate([-qr[half:], qr[:half]], axis=0)
        qr = qr * cos_t + qr_rot * sin_t
        qt_ref[h, :NOPE, :] = (qt[:NOPE] * q_scale).astype(BF16)
        qt_ref[h, NOPE:, :] = (qr * q_scale).astype(BF16)
        k_ref[h, :, :NOPE] = k_nope[:, h * NOPE:(h + 1) * NOPE].astype(BF16)
        k_ref[h, :, NOPE:] = k_rope
        vt = lax.dot_general(wvt_ref[h], ckv, nt, preferred_element_type=F32)
        vt_ref[h, 0] = vt.astype(BF16)


def _qkv(cq, ckv, kr, pos_col, pos_row, f_col, f_row, wqt, wk, wvt, tk):
    L = cq.shape[0]
    q_scale = math.log2(math.e) / math.sqrt(QK)
    row = lambda n: pl.BlockSpec((tk, n), lambda i: (i, 0))
    return pl.pallas_call(
        functools.partial(_qkv_kernel, q_scale=q_scale),
        grid=(L // tk,),
        in_specs=[row(Q_LORA), row(KV_LORA), row(LANES), row(1),
                  pl.BlockSpec((1, tk), lambda i: (0, i)),
                  _resident((ROPE, 1)), _resident((1, ROPE)),
                  _resident((HEADS, QK, Q_LORA)), _resident((KV_LORA, HEADS * NOPE)),
                  _resident((HEADS, DV, KV_LORA))],
        out_specs=(pl.BlockSpec((HEADS, QK, tk), lambda i: (0, 0, i)),
                   pl.BlockSpec((HEADS, tk, QK), lambda i: (0, i, 0)),
                   pl.BlockSpec((HEADS, 1, DV, tk), lambda i: (0, i, 0, 0))),
        out_shape=(jax.ShapeDtypeStruct((HEADS, QK, L), BF16),
                   jax.ShapeDtypeStruct((HEADS, L, QK), BF16),
                   jax.ShapeDtypeStruct((HEADS, L // tk, DV, tk), BF16)),
        compiler_params=_params("parallel"),
        name="qkv",
    )(cq, ckv, kr, pos_col, pos_row, f_col, f_row, wqt, wk, wvt)


def _attn_kernel(qt_ref, k_ref, vt_ref, gate_ref, o_ref, acc_ref, m_ref, l_ref, *, tq, tk):
    i = pl.program_id(1)
    qt = qt_ref[0]
    m_ref[...] = jnp.full(m_ref.shape, -1e30, F32)
    l_ref[...] = jnp.zeros(l_ref.shape, F32)
    acc_ref[...] = jnp.zeros(acc_ref.shape, F32)
    per_q = tq // tk

    def step(j, masked):
        start = pl.multiple_of(j * tk, tk)
        s = jnp.dot(k_ref[0, pl.ds(start, tk), :], qt, preferred_element_type=F32)
        if masked:
            key = start + lax.broadcasted_iota(jnp.int32, (tk, tq), 0)
            qry = i * tq + lax.broadcasted_iota(jnp.int32, (tk, tq), 1)
            s = jnp.where(key <= qry, s, -1e30)
        m_prev = m_ref[...]
        m_new = jnp.maximum(m_prev, jnp.max(s, axis=0, keepdims=True))
        alpha = jnp.exp2(m_prev - m_new)
        p = jnp.exp2(s - m_new)
        l_ref[...] = alpha * l_ref[...] + jnp.sum(p, axis=0, keepdims=True)
        pv = jnp.dot(vt_ref[0, j], p.astype(BF16), preferred_element_type=F32)
        acc_ref[...] = alpha * acc_ref[...] + pv
        m_ref[...] = m_new

    def body(j, carry):
        step(j, False)
        return carry

    lax.fori_loop(0, i * per_q, body, 0)
    for jj in range(per_q):
        step(i * per_q + jj, True)

    out = (acc_ref[...] * (1.0 / l_ref[...])).T
    g = gate_ref[...]
    o_ref[...] = (out * (g * _sigmoid(g))).astype(BF16)


def _attention(qt, k, vt, gate, tq, tk):
    L = k.shape[1]
    return pl.pallas_call(
        functools.partial(_attn_kernel, tq=tq, tk=tk),
        grid=(HEADS, L // tq),
        in_specs=[pl.BlockSpec((1, QK, tq), lambda h, i: (h, 0, i)),
                  pl.BlockSpec((1, L, QK), lambda h, i: (h, 0, 0)),
                  pl.BlockSpec((1, L // tk, DV, tk), lambda h, i: (h, 0, 0, 0)),
                  pl.BlockSpec((tq, DV), lambda h, i: (i, h))],
        out_specs=pl.BlockSpec((tq, DV), lambda h, i: (i, h)),
        out_shape=jax.ShapeDtypeStruct((L, D_MLA), BF16),
        scratch_shapes=[pltpu.VMEM((DV, tq), F32), pltpu.VMEM((1, tq), F32), pltpu.VMEM((1, tq), F32)],
        compiler_params=_params("parallel", "arbitrary"),
        name="attention",
    )(qt, k, vt, gate)


def _gelu_tanh(x):
    c = math.sqrt(2.0 / math.pi)
    return 0.5 * x * (1.0 + jnp.tanh(c * (x + 0.044715 * (x * x * x))))


def _tile_rows(tile):
    return slice(SUBLANES * tile, SUBLANES * (tile + 1))


def _s5_kernel(u_ref, g_ref, wb_ref, are_ref, aim_ref, wcre_ref, wcim_ref, d_ref, wglu_ref,
               o_ref, hs_ref, st_ref, *, tt):
    nt8 = tt // SUBLANES

    @pl.when(pl.program_id(0) == 0)
    def _():
        st_ref[...] = jnp.zeros(st_ref.shape, F32)

    u = u_ref[...]
    ub = u.astype(BF16)
    groups_per_tile = LANES // GROUP_CH
    tiles_per_dot = groups_per_tile * NSTATE // LANES
    for s in range(D_SSM // LANES):
        r = jnp.dot(ub[:, LANES * s:LANES * (s + 1)], wb_ref[s], preferred_element_type=F32)
        for k in range(tiles_per_dot):
            re = r[:, LANES * k:LANES * (k + 1)]
            im = r[:, LANES * (tiles_per_dot + k):LANES * (tiles_per_dot + k + 1)]
            hs_ref[:, _tile_rows(tiles_per_dot * s + k), :] = re.reshape(nt8, SUBLANES, LANES)
            hs_ref[:, _tile_rows(STATE_TILES + tiles_per_dot * s + k), :] = im.reshape(nt8, SUBLANES, LANES)

    a_re = [are_ref[i] for i in range(N_SLABS)]
    a_im = [aim_ref[i] for i in range(N_SLABS)]

    def scan_body(ti, st):
        st = list(st)
        for sub in range(SUBLANES):
            for i in range(N_SLABS):
                re_rows = pl.ds(sub + SUBLANES * SLAB_TILES * i, SLAB_TILES, stride=SUBLANES)
                im_rows = pl.ds(sub + SUBLANES * (STATE_TILES + SLAB_TILES * i), SLAB_TILES,
                                stride=SUBLANES)
                h_re, h_im = st[i], st[N_SLABS + i]
                n_re = a_re[i] * h_re - a_im[i] * h_im + hs_ref[ti, re_rows, :]
                n_im = a_re[i] * h_im + a_im[i] * h_re + hs_ref[ti, im_rows, :]
                hs_ref[ti, re_rows, :] = n_re
                hs_ref[ti, im_rows, :] = n_im
                st[i], st[N_SLABS + i] = n_re, n_im
        return tuple(st)

    st = lax.fori_loop(0, nt8, scan_body, tuple(st_ref[i] for i in range(2 * N_SLABS)))
    for i in range(2 * N_SLABS):
        st_ref[i] = st[i]

    def state_cols(first_tile):
        tiles = [hs_ref[:, _tile_rows(first_tile + j), :].reshape(tt, LANES) for j in range(SLAB_TILES)]
        return jnp.concatenate(tiles, axis=-1).astype(BF16)

    ys = []
    for i in range(N_SLABS):
        y = jnp.dot(state_cols(SLAB_TILES * i), wcre_ref[i], preferred_element_type=F32)
        y += jnp.dot(state_cols(STATE_TILES + SLAB_TILES * i), wcim_ref[i], preferred_element_type=F32)
        ys.append(y)
    y = jnp.concatenate(ys, axis=-1) + d_ref[...] * u
    y = _gelu_tanh(y).astype(BF16)
    z = jnp.dot(y, wglu_ref[...], preferred_element_type=F32)
    out = z[:, :D_SSM] * _sigmoid(z[:, D_SSM:])
    g = g_ref[...]
    o_ref[...] = (out * (g * _sigmoid(g))).astype(BF16)


def _s5(u, gate, wb, a_re, a_im, wc_re, wc_im, d_skip, w_glu, tt):
    L = u.shape[0]
    row = pl.BlockSpec((tt, D_SSM), lambda c: (c, 0))
    cols_per_slab = SLAB_TILES * LANES
    return pl.pallas_call(
        functools.partial(_s5_kernel, tt=tt),
        grid=(L // tt,),
        in_specs=[row, row,
                  _resident((D_SSM // LANES, LANES, 2 * (LANES // GROUP_CH) * NSTATE)),
                  _resident((N_SLABS, SUBLANES, LANES)), _resident((N_SLABS, SUBLANES, LANES)),
                  _resident((N_SLABS, cols_per_slab, 2 * LANES)),
                  _resident((N_SLABS, cols_per_slab, 2 * LANES)),
                  _resident((1, D_SSM)), _resident((D_SSM, 2 * D_SSM))],
        out_specs=row,
        out_shape=jax.ShapeDtypeStruct((L, D_SSM), BF16),
        scratch_shapes=[pltpu.VMEM((tt // SUBLANES, 2 * STATE_TILES * SUBLANES, LANES), F32),
                        pltpu.VMEM((2 * N_SLABS, SUBLANES, LANES), F32)],
        compiler_params=_params("arbitrary"),
        name="s5",
    )(u, gate, wb, a_re, a_im, wc_re, wc_im, d_skip, w_glu)


def _outproj_kernel(mla_ref, ssm_ref, w_ref, gpost_ref, x_ref, o_ref):
    out = jnp.dot(mla_ref[...], w_ref[:D_MLA, :], preferred_element_type=F32)
    out += jnp.dot(ssm_ref[...], w_ref[D_MLA:, :], preferred_element_type=F32)
    inv = lax.rsqrt(jnp.mean(out * out, axis=-1, keepdims=True) + NORM_EPS)
    o_ref[...] = x_ref[...] + out * inv * gpost_ref[...]


def _outproj(mla, ssm, w_out, g_post, x, tm):
    L = x.shape[0]
    row = lambda n: pl.BlockSpec((tm, n), lambda i: (i, 0))
    return pl.pallas_call(
        _outproj_kernel,
        grid=(L // tm,),
        in_specs=[row(D_MLA), row(D_SSM), _resident((D_MLA + D_SSM, D_MODEL)),
                  _resident((1, D_MODEL)), row(D_MODEL)],
        out_specs=row(D_MODEL),
        out_shape=jax.ShapeDtypeStruct((L, D_MODEL), F32),
        compiler_params=_params("parallel"),
        name="outproj",
    )(mla, ssm, w_out, g_post, x)


def _pack_in_weights(w_in):
    s1 = Q_LORA
    s2 = s1 + KV_LORA
    s3 = s2 + ROPE
    depth = w_in.shape[0]
    pad = jnp.zeros((depth, D_MODEL, LANES - ROPE), w_in.dtype)
    return jnp.concatenate([w_in[..., :s2], w_in[..., s3:], w_in[..., s2:s3], pad], axis=-1).astype(BF16)


def _pack_b(bb_re, bb_im):
    depth = bb_re.shape[0]
    gpt = LANES // GROUP_CH
    eye = jnp.eye(gpt, dtype=F32)

    def pack(b):
        b = b.reshape(depth, GROUPS // gpt, gpt, NSTATE, GROUP_CH)
        return jnp.einsum('dsgph,gk->dsghkp', b, eye).reshape(depth, GROUPS // gpt, LANES, gpt * NSTATE)

    return jnp.concatenate([pack(bb_re), pack(bb_im)], axis=-1).astype(BF16)


def _pack_c(c):
    depth = c.shape[0]
    gps = SLAB_TILES * LANES // NSTATE
    eye = jnp.eye(gps, dtype=F32)
    c = c.reshape(depth, N_SLABS, gps, GROUP_CH, NSTATE)
    return jnp.einsum('dighp,gk->digpkh', c, eye).reshape(
        depth, N_SLABS, gps * NSTATE, gps * GROUP_CH).astype(BF16)


def kernel(x, positions, norm_pre, norm_post, w_in, q_norm, w_uq, kv_norm, w_ukv, ssm_a_re, ssm_a_im,
           ssm_b_re, ssm_b_im, ssm_c_re, ssm_c_im, ssm_d, ssm_log_step, w_glu, w_out):
    batch, L, _ = x.shape
    assert batch == 1
    depth = w_in.shape[0]
    tm = min(256, L)
    tk = min(512, L)
    tq = min(512, L)
    tt = min(256, L)

    pos_col = positions.reshape(L, 1)
    pos_row = positions.reshape(1, L)
    inv_freq = ROPE_THETA ** (-jnp.arange(0, ROPE, 2, dtype=F32) / ROPE)
    inv_freq = jnp.concatenate([inv_freq, inv_freq])
    f_col = inv_freq.reshape(ROPE, 1)
    f_row = inv_freq.reshape(1, ROPE)

    w_in_p = _pack_in_weights(w_in)
    wqt = w_uq.reshape(depth, Q_LORA, HEADS, QK).transpose(0, 2, 3, 1).astype(BF16)
    w_ukv4 = w_ukv.reshape(depth, KV_LORA, HEADS, NOPE + DV)
    wk = w_ukv4[..., :NOPE].reshape(depth, KV_LORA, HEADS * NOPE).astype(BF16)
    wvt = w_ukv4[..., NOPE:].transpose(0, 2, 3, 1).astype(BF16)
    w_glu_b = w_glu.astype(BF16)
    w_out_b = w_out.astype(BF16)

    ab_re, ab_im, bb_re, bb_im = _s5_prep(ssm_a_re, ssm_a_im, ssm_log_step, ssm_b_re, ssm_b_im)
    wb = _pack_b(bb_re, bb_im)
    slab_shape = (depth, N_SLABS, SUBLANES, LANES)
    a_re_s = ab_re.reshape(slab_shape)
    a_im_s = ab_im.reshape(slab_shape)
    wc_re = _pack_c(ssm_c_re)
    wc_im = _pack_c(-ssm_c_im)

    h = x[0]
    for i in range(depth):
        cq, ckv, kr, gate_mla, u, gate_ssm = _inproj(
            h, norm_pre[i][None], w_in_p[i], q_norm[i][None], kv_norm[i][None], tm)
        qt, k, vt = _qkv(cq, ckv, kr, pos_col, pos_row, f_col, f_row, wqt[i], wk[i], wvt[i], tk)
        mla = _attention(qt, k, vt, gate_mla, tq, tk)
        ssm = _s5(u, gate_ssm, wb[i], a_re_s[i], a_im_s[i], wc_re[i], wc_im[i],
                  ssm_d[i][None], w_glu_b[i], tt)
        h = _outproj(mla, ssm, w_out_b[i], norm_post[i][None], h, tm)
    return h[None]
```

```python
import functools
import math

import jax
import jax.numpy as jnp
from jax import lax
from jax.experimental import pallas as pl
from jax.experimental.pallas import tpu as pltpu

F32 = jnp.float32
BF16 = jnp.bfloat16

D_MODEL = 2048
HEADS = 8
NOPE = 128
ROPE = 64
QK = NOPE + ROPE
DV = 128
Q_LORA = 512
KV_LORA = 256
D_MLA = HEADS * DV
D_SSM = 1024
GROUP_CH = 16
GROUPS = D_SSM // GROUP_CH
NSTATE = 64
ROPE_THETA = 10000.0
NORM_EPS = 1e-6
LANES = 128
SUBLANES = 8
MXU_COLS = 256
STATE_TILES = GROUPS * NSTATE // LANES
SLAB_TILES = SUBLANES
N_SLABS = STATE_TILES // SLAB_TILES
W_IN_COLS = Q_LORA + KV_LORA + D_MLA + D_SSM + D_SSM + LANES
VMEM_LIMIT = 56 * 1024 * 1024


def _resident(shape):
    zeros = (0,) * len(shape)
    return pl.BlockSpec(shape, lambda *_: zeros, pipeline_mode=pl.Buffered(1))


def _sigmoid(x):
    return 1.0 / (1.0 + jnp.exp(-x))


def _params(*sem):
    return pltpu.CompilerParams(dimension_semantics=sem, vmem_limit_bytes=VMEM_LIMIT)


def _s5_prep_kernel(are_ref, aim_ref, lstep_ref, bre_ref, bim_ref,
                    abre_ref, abim_ref, bbre_ref, bbim_ref):
    a_re = are_ref[...]
    a_im = aim_ref[...]
    step = jnp.exp(lstep_ref[...])
    mag = jnp.exp(step * a_re)
    ab_re = mag * jnp.cos(step * a_im)
    ab_im = mag * jnp.sin(step * a_im)
    n_re = ab_re - 1.0
    den = a_re * a_re + a_im * a_im
    z_re = (n_re * a_re + ab_im * a_im) / den
    z_im = (ab_im * a_re - n_re * a_im) / den
    b_re = bre_ref[...]
    b_im = bim_ref[...]
    abre_ref[...] = ab_re
    abim_ref[...] = ab_im
    bbre_ref[...] = z_re * b_re - z_im * b_im
    bbim_ref[...] = z_re * b_im + z_im * b_re


def _s5_prep(a_re, a_im, log_step, b_re, b_im):
    depth = a_re.shape[0]
    rows = depth * GROUPS
    cols = NSTATE * GROUP_CH

    def expand(a):
        return jnp.broadcast_to(a[..., None], (depth, GROUPS, NSTATE, GROUP_CH)).reshape(rows, cols)

    step = jnp.broadcast_to(log_step[..., None], (depth, GROUPS, cols)).reshape(rows, cols)
    sds = jax.ShapeDtypeStruct((rows, cols), F32)
    ab_re, ab_im, bb_re, bb_im = pl.pallas_call(
        _s5_prep_kernel, out_shape=(sds, sds, sds, sds), name="s5_prep",
    )(expand(a_re), expand(a_im), step, b_re.reshape(rows, cols), b_im.reshape(rows, cols))
    shape4 = (depth, GROUPS, NSTATE, GROUP_CH)
    ab_re = ab_re.reshape(shape4)[..., 0]
    ab_im = ab_im.reshape(shape4)[..., 0]
    return ab_re, ab_im, bb_re.reshape(shape4), bb_im.reshape(shape4)


def _inproj_kernel(x_ref, gpre_ref, w_ref, gq_ref, gkv_ref,
                   cq_ref, ckv_ref, kr_ref, gm_ref, u_ref, gs_ref):
    x = x_ref[...]
    inv = lax.rsqrt(jnp.mean(x * x, axis=-1, keepdims=True) + NORM_EPS)
    h = (x * inv * gpre_ref[...]).astype(BF16)

    def proj(lo, hi):
        return jnp.dot(h, w_ref[:, lo:hi], preferred_element_type=F32)

    def latent_norm(c, g_ref):
        inv_c = lax.rsqrt(jnp.mean(c * c, axis=-1, keepdims=True) + NORM_EPS)
        return (c * inv_c * g_ref[...]).astype(BF16)

    o = 0
    cq_ref[...] = latent_norm(proj(o, o + Q_LORA), gq_ref)
    o += Q_LORA
    ckv_ref[...] = latent_norm(proj(o, o + KV_LORA), gkv_ref)
    o += KV_LORA
    gm_ref[...] = proj(o, o + D_MLA)
    o += D_MLA
    u_ref[...] = proj(o, o + D_SSM)
    o += D_SSM
    gs_ref[...] = proj(o, o + D_SSM)
    o += D_SSM
    kr_ref[...] = proj(o, o + LANES)


def _inproj(x, g_pre, w_in, g_q, g_kv, tm):
    L = x.shape[0]
    row = lambda n: pl.BlockSpec((tm, n), lambda i: (i, 0))
    out_shape = (
        jax.ShapeDtypeStruct((L, Q_LORA), BF16),
        jax.ShapeDtypeStruct((L, KV_LORA), BF16),
        jax.ShapeDtypeStruct((L, LANES), F32),
        jax.ShapeDtypeStruct((L, D_MLA), F32),
        jax.ShapeDtypeStruct((L, D_SSM), F32),
        jax.ShapeDtypeStruct((L, D_SSM), F32),
    )
    return pl.pallas_call(
        _inproj_kernel,
        grid=(L // tm,),
        in_specs=[row(D_MODEL), _resident((1, D_MODEL)), _resident((D_MODEL, W_IN_COLS)),
                  _resident((1, Q_LORA)), _resident((1, KV_LORA))],
        out_specs=(row(Q_LORA), row(KV_LORA), row(LANES), row(D_MLA), row(D_SSM), row(D_SSM)),
        out_shape=out_shape,
        compiler_params=_params("parallel"),
        name="inproj",
    )(x, g_pre, w_in, g_q, g_kv)


def _qkv_kernel(cq_ref, ckv_ref, kr_ref, posc_ref, posr_ref, fcol_ref, frow_ref,
                wqt_ref, wk_ref, wvt_ref, qt_ref, k_ref, vt_ref, *, q_scale):
    nt = (((1,), (1,)), ((), ()))
    cq = cq_ref[...]
    ckv = ckv_ref[...]
    half = ROPE // 2

    ang_t = fcol_ref[...] * posr_ref[...].astype(F32)
    cos_t = jnp.cos(ang_t)
    sin_t = jnp.sin(ang_t)
    ang = posc_ref[...].astype(F32) * frow_ref[...]
    cos_r = jnp.cos(ang)
    sin_r = jnp.sin(ang)

    kr = kr_ref[:, :ROPE]
    kr_rot = jnp.concatenate([-kr[:, half:], kr[:, :half]], axis=-1)
    k_rope = (kr * cos_r + kr_rot * sin_r).astype(BF16)
    k_nope = jnp.dot(ckv, wk_ref[...], preferred_element_type=F32)

    for h in range(HEADS):
        qt = lax.dot_general(wqt_ref[h], cq, nt, preferred_element_type=F32)
        qr = qt[NOPE:]
        qr_rot = jnp.concatenate([-qr[half:], qr[:half]], axis=0)
        qr = qr * cos_t + qr_rot * sin_t
        qt_ref[h, :NOPE, :] = (qt[:NOPE] * q_scale).astype(BF16)
        qt_ref[h, NOPE:, :] = (qr * q_scale).astype(BF16)
        k_ref[h, :, :NOPE] = k_nope[:, h * NOPE:(h + 1) * NOPE].astype(BF16)
        k_ref[h, :, NOPE:] = k_rope
        vt = lax.dot_general(wvt_ref[h], ckv, nt, preferred_element_type=F32)
        vt_ref[h, 0] = vt.astype(BF16)


def _qkv(cq, ckv, kr, pos_col, pos_row, f_col, f_row, wqt, wk, wvt, tk):
    L = cq.shape[0]
    q_scale = math.log2(math.e) / math.sqrt(QK)
    row = lambda n: pl.BlockSpec((tk, n), lambda i: (i, 0))
    return pl.pallas_call(
        functools.partial(_qkv_kernel, q_scale=q_scale),
        grid=(L // tk,),
        in_specs=[row(Q_LORA), row(KV_LORA), row(LANES), row(1),
                  pl.BlockSpec((1, tk), lambda i: (0, i)),
                  _resident((ROPE, 1)), _resident((1, ROPE)),
                  _resident((HEADS, QK, Q_LORA)), _resident((KV_LORA, HEADS * NOPE)),
                  _resident((HEADS, DV, KV_LORA))],
        out_specs=(pl.BlockSpec((HEADS, QK, tk), lambda i: (0, 0, i)),
                   pl.BlockSpec((HEADS, tk, QK), lambda i: (0, i, 0)),
                   pl.BlockSpec((HEADS, 1, DV, tk), lambda i: (0, i, 0, 0))),
        out_shape=(jax.ShapeDtypeStruct((HEADS, QK, L), BF16),
                   jax.ShapeDtypeStruct((HEADS, L, QK), BF16),
                   jax.ShapeDtypeStruct((HEADS, L // tk, DV, tk), BF16)),
        compiler_params=_params("parallel"),
        name="qkv",
    )(cq, ckv, kr, pos_col, pos_row, f_col, f_row, wqt, wk, wvt)


def _attn_kernel(qt_ref, k_ref, vt_ref, gate_ref, o_ref, acc_ref, m_ref, l_ref, s_ref, *, tq, tk):
    i = pl.program_id(1)
    m_ref[...] = jnp.full(m_ref.shape, -1e30, F32)
    l_ref[...] = jnp.zeros(l_ref.shape, F32)
    acc_ref[...] = jnp.zeros(acc_ref.shape, F32)
    assert tq == 2 * tk
    n_chains = tq // MXU_COLS

    def cols(c):
        return slice(MXU_COLS * c, MXU_COLS * (c + 1))

    def scores(j, slot, diag=None):
        kc = k_ref[0, pl.ds(pl.multiple_of(j * tk, tk), tk), :]
        for c in range(n_chains):
            if diag is not None and diag * tk >= MXU_COLS * (c + 1):
                continue
            s_ref[slot, c] = jnp.dot(kc, qt_ref[0, :, cols(c)], preferred_element_type=F32)

    def softmax_pv(j, slot, diag=None):
        vc = vt_ref[0, j]
        for c in range(n_chains):
            if diag is not None and diag * tk >= MXU_COLS * (c + 1):
                continue
            s = s_ref[slot, c]
            if diag is not None and diag * tk + tk - 1 > MXU_COLS * c:
                key = diag * tk + lax.broadcasted_iota(jnp.int32, s.shape, 0)
                qry = MXU_COLS * c + lax.broadcasted_iota(jnp.int32, s.shape, 1)
                s = jnp.where(key <= qry, s, -1e30)
            m_prev = m_ref[:, cols(c)]
            m_new = jnp.maximum(m_prev, jnp.max(s, axis=0, keepdims=True))
            alpha = jnp.exp2(m_prev - m_new)
            p = jnp.exp2(s - m_new)
            l_ref[:, cols(c)] = alpha * l_ref[:, cols(c)] + jnp.sum(p, axis=0, keepdims=True)
            pv = jnp.dot(vc, p.astype(BF16), preferred_element_type=F32)
            acc_ref[:, cols(c)] = alpha * acc_ref[:, cols(c)] + pv
            m_ref[:, cols(c)] = m_new

    scores(0, 0)

    def body(t, carry):
        scores(2 * t + 1, 1)
        softmax_pv(2 * t, 0)
        scores(2 * t + 2, 0)
        softmax_pv(2 * t + 1, 1)
        return carry

    lax.fori_loop(0, i, body, 0)
    scores(2 * i + 1, 1, diag=1)
    softmax_pv(2 * i, 0, diag=0)
    softmax_pv(2 * i + 1, 1, diag=1)

    out = (acc_ref[...] * (1.0 / l_ref[...])).T
    g = gate_ref[...]
    o_ref[...] = (out * (g * _sigmoid(g))).astype(BF16)


def _attention(qt, k, vt, gate, tq, tk):
    L = k.shape[1]
    return pl.pallas_call(
        functools.partial(_attn_kernel, tq=tq, tk=tk),
        grid=(HEADS, L // tq),
        in_specs=[pl.BlockSpec((1, QK, tq), lambda h, i: (h, 0, i)),
                  pl.BlockSpec((1, L, QK), lambda h, i: (h, 0, 0)),
                  pl.BlockSpec((1, L // tk, DV, tk), lambda h, i: (h, 0, 0, 0)),
                  pl.BlockSpec((tq, DV), lambda h, i: (i, h))],
        out_specs=pl.BlockSpec((tq, DV), lambda h, i: (i, h)),
        out_shape=jax.ShapeDtypeStruct((L, D_MLA), BF16),
        scratch_shapes=[pltpu.VMEM((DV, tq), F32), pltpu.VMEM((1, tq), F32), pltpu.VMEM((1, tq), F32),
                        pltpu.VMEM((2, tq // MXU_COLS, tk, MXU_COLS), F32)],
        compiler_params=_params("parallel", "arbitrary"),
        name="attention",
    )(qt, k, vt, gate)


def _gelu_tanh(x):
    c = math.sqrt(2.0 / math.pi)
    return 0.5 * x * (1.0 + jnp.tanh(c * (x + 0.044715 * (x * x * x))))


def _tile_rows(tile):
    return slice(SUBLANES * tile, SUBLANES * (tile + 1))


def _s5_kernel(u_ref, g_ref, wb_ref, are_ref, aim_ref, wcre_ref, wcim_ref, d_ref, wglu_ref,
               o_ref, hs_ref, st_ref, *, tt):
    nt8 = tt // SUBLANES

    @pl.when(pl.program_id(0) == 0)
    def _():
        st_ref[...] = jnp.zeros(st_ref.shape, F32)

    u = u_ref[...]
    ub = u.astype(BF16)
    groups_per_tile = LANES // GROUP_CH
    tiles_per_dot = groups_per_tile * NSTATE // LANES
    for s in range(D_SSM // LANES):
        r = jnp.dot(ub[:, LANES * s:LANES * (s + 1)], wb_ref[s], preferred_element_type=F32)
        for k in range(tiles_per_dot):
            re = r[:, LANES * k:LANES * (k + 1)]
            im = r[:, LANES * (tiles_per_dot + k):LANES * (tiles_per_dot + k + 1)]
            hs_ref[:, _tile_rows(tiles_per_dot * s + k), :] = re.reshape(nt8, SUBLANES, LANES)
            hs_ref[:, _tile_rows(STATE_TILES + tiles_per_dot * s + k), :] = im.reshape(nt8, SUBLANES, LANES)

    a_re = [are_ref[i] for i in range(N_SLABS)]
    a_im = [aim_ref[i] for i in range(N_SLABS)]

    def scan_body(ti, st):
        st = list(st)
        for sub in range(SUBLANES):
            for i in range(N_SLABS):
                re_rows = pl.ds(sub + SUBLANES * SLAB_TILES * i, SLAB_TILES, stride=SUBLANES)
                im_rows = pl.ds(sub + SUBLANES * (STATE_TILES + SLAB_TILES * i), SLAB_TILES,
                                stride=SUBLANES)
                h_re, h_im = st[i], st[N_SLABS + i]
                n_re = a_re[i] * h_re - a_im[i] * h_im + hs_ref[ti, re_rows, :]
                n_im = a_re[i] * h_im + a_im[i] * h_re + hs_ref[ti, im_rows, :]
                hs_ref[ti, re_rows, :] = n_re
                hs_ref[ti, im_rows, :] = n_im
                st[i], st[N_SLABS + i] = n_re, n_im
        return tuple(st)

    st = lax.fori_loop(0, nt8, scan_body, tuple(st_ref[i] for i in range(2 * N_SLABS)))
    for i in range(2 * N_SLABS):
        st_ref[i] = st[i]

    def state_cols(first_tile):
        tiles = [hs_ref[:, _tile_rows(first_tile + j), :].reshape(tt, LANES) for j in range(SLAB_TILES)]
        return jnp.concatenate(tiles, axis=-1).astype(BF16)

    ys = []
    for i in range(N_SLABS):
        y = jnp.dot(state_cols(SLAB_TILES * i), wcre_ref[i], preferred_element_type=F32)
        y += jnp.dot(state_cols(STATE_TILES + SLAB_TILES * i), wcim_ref[i], preferred_element_type=F32)
        ys.append(y)
    y = jnp.concatenate(ys, axis=-1) + d_ref[...] * u
    y = _gelu_tanh(y).astype(BF16)
    z = jnp.dot(y, wglu_ref[...], preferred_element_type=F32)
    out = z[:, :D_SSM] * _sigmoid(z[:, D_SSM:])
    g = g_ref[...]
    o_ref[...] = (out * (g * _sigmoid(g))).astype(BF16)


def _s5(u, gate, wb, a_re, a_im, wc_re, wc_im, d_skip, w_glu, tt):
    L = u.shape[0]
    row = pl.BlockSpec((tt, D_SSM), lambda c: (c, 0))
    cols_per_slab = SLAB_TILES * LANES
    return pl.pallas_call(
        functools.partial(_s5_kernel, tt=tt),
        grid=(L // tt,),
        in_specs=[row, row,
                  _resident((D_SSM // LANES, LANES, 2 * (LANES // GROUP_CH) * NSTATE)),
                  _resident((N_SLABS, SUBLANES, LANES)), _resident((N_SLABS, SUBLANES, LANES)),
                  _resident((N_SLABS, cols_per_slab, 2 * LANES)),
                  _resident((N_SLABS, cols_per_slab, 2 * LANES)),
                  _resident((1, D_SSM)), _resident((D_SSM, 2 * D_SSM))],
        out_specs=row,
        out_shape=jax.ShapeDtypeStruct((L, D_SSM), BF16),
        scratch_shapes=[pltpu.VMEM((tt // SUBLANES, 2 * STATE_TILES * SUBLANES, LANES), F32),
                        pltpu.VMEM((2 * N_SLABS, SUBLANES, LANES), F32)],
        compiler_params=_params("arbitrary"),
        name="s5",
    )(u, gate, wb, a_re, a_im, wc_re, wc_im, d_skip, w_glu)


def _outproj_kernel(mla_ref, ssm_ref, w_ref, gpost_ref, x_ref, o_ref):
    out = jnp.dot(mla_ref[...], w_ref[:D_MLA, :], preferred_element_type=F32)
    out += jnp.dot(ssm_ref[...], w_ref[D_MLA:, :], preferred_element_type=F32)
    inv = lax.rsqrt(jnp.mean(out * out, axis=-1, keepdims=True) + NORM_EPS)
    o_ref[...] = x_ref[...] + out * inv * gpost_ref[...]


def _outproj(mla, ssm, w_out, g_post, x, tm):
    L = x.shape[0]
    row = lambda n: pl.BlockSpec((tm, n), lambda i: (i, 0))
    return pl.pallas_call(
        _outproj_kernel,
        grid=(L // tm,),
        in_specs=[row(D_MLA), row(D_SSM), _resident((D_MLA + D_SSM, D_MODEL)),
                  _resident((1, D_MODEL)), row(D_MODEL)],
        out_specs=row(D_MODEL),
        out_shape=jax.ShapeDtypeStruct((L, D_MODEL), F32),
        compiler_params=_params("parallel"),
        name="outproj",
    )(mla, ssm, w_out, g_post, x)


def _pack_in_weights(w_in):
    s1 = Q_LORA
    s2 = s1 + KV_LORA
    s3 = s2 + ROPE
    depth = w_in.shape[0]
    pad = jnp.zeros((depth, D_MODEL, LANES - ROPE), w_in.dtype)
    return jnp.concatenate([w_in[..., :s2], w_in[..., s3:], w_in[..., s2:s3], pad], axis=-1).astype(BF16)


def _pack_b(bb_re, bb_im):
    depth = bb_re.shape[0]
    gpt = LANES // GROUP_CH
    eye = jnp.eye(gpt, dtype=F32)

    def pack(b):
        b = b.reshape(depth, GROUPS // gpt, gpt, NSTATE, GROUP_CH)
        return jnp.einsum('dsgph,gk->dsghkp', b, eye).reshape(depth, GROUPS // gpt, LANES, gpt * NSTATE)

    return jnp.concatenate([pack(bb_re), pack(bb_im)], axis=-1).astype(BF16)


def _pack_c(c):
    depth = c.shape[0]
    gps = SLAB_TILES * LANES // NSTATE
    eye = jnp.eye(gps, dtype=F32)
    c = c.reshape(depth, N_SLABS, gps, GROUP_CH, NSTATE)
    return jnp.einsum('dighp,gk->digpkh', c, eye).reshape(
        depth, N_SLABS, gps * NSTATE, gps * GROUP_CH).astype(BF16)


def kernel(x, positions, norm_pre, norm_post, w_in, q_norm, w_uq, kv_norm, w_ukv, ssm_a_re, ssm_a_im,
           ssm_b_re, ssm_b_im, ssm_c_re, ssm_c_im, ssm_d, ssm_log_step, w_glu, w_out):
    batch, L, _ = x.shape
    assert batch == 1
    depth = w_in.shape[0]
    tm = min(256, L)
    tq = min(1024, L)
    tk = tq // 2
    tt = min(256, L)

    pos_col = positions.reshape(L, 1)
    pos_row = positions.reshape(1, L)
    inv_freq = ROPE_THETA ** (-jnp.arange(0, ROPE, 2, dtype=F32) / ROPE)
    inv_freq = jnp.concatenate([inv_freq, inv_freq])
    f_col = inv_freq.reshape(ROPE, 1)
    f_row = inv_freq.reshape(1, ROPE)

    w_in_p = _pack_in_weights(w_in)
    wqt = w_uq.reshape(depth, Q_LORA, HEADS, QK).transpose(0, 2, 3, 1).astype(BF16)
    w_ukv4 = w_ukv.reshape(depth, KV_LORA, HEADS, NOPE + DV)
    wk = w_ukv4[..., :NOPE].reshape(depth, KV_LORA, HEADS * NOPE).astype(BF16)
    wvt = w_ukv4[..., NOPE:].transpose(0, 2, 3, 1).astype(BF16)
    w_glu_b = w_glu.astype(BF16)
    w_out_b = w_out.astype(BF16)

    ab_re, ab_im, bb_re, bb_im = _s5_prep(ssm_a_re, ssm_a_im, ssm_log_step, ssm_b_re, ssm_b_im)
    wb = _pack_b(bb_re, bb_im)
    slab_shape = (depth, N_SLABS, SUBLANES, LANES)
    a_re_s = ab_re.reshape(slab_shape)
    a_im_s = ab_im.reshape(slab_shape)
    wc_re = _pack_c(ssm_c_re)
    wc_im = _pack_c(-ssm_c_im)

    h = x[0]
    for i in range(depth):
        cq, ckv, kr, gate_mla, u, gate_ssm = _inproj(
            h, norm_pre[i][None], w_in_p[i], q_norm[i][None], kv_norm[i][None], tm)
        qt, k, vt = _qkv(cq, ckv, kr, pos_col, pos_row, f_col, f_row, wqt[i], wk[i], wvt[i], tk)
        mla = _attention(qt, k, vt, gate_mla, tq, tk)
        ssm = _s5(u, gate_ssm, wb[i], a_re_s[i], a_im_s[i], wc_re[i], wc_im[i],
                  ssm_d[i][None], w_glu_b[i], tt)
        h = _outproj(mla, ssm, w_out_b[i], norm_post[i][None], h, tm)
    return h[None]
```

```python
import functools
import math

import jax
import jax.numpy as jnp
from jax import lax
from jax.experimental import pallas as pl
from jax.experimental.pallas import tpu as pltpu

F32 = jnp.float32
BF16 = jnp.bfloat16

D_MODEL = 2048
HEADS = 8
NOPE = 128
ROPE = 64
QK = NOPE + ROPE
DV = 128
V_ROWS = DV + 16
Q_LORA = 512
KV_LORA = 256
D_MLA = HEADS * DV
D_SSM = 1024
GROUP_CH = 16
GROUPS = D_SSM // GROUP_CH
NSTATE = 64
ROPE_THETA = 10000.0
NORM_EPS = 1e-6
LANES = 128
SUBLANES = 8
MXU_COLS = 256
BF16_ROWS = 16
STATE_TILES = GROUPS * NSTATE // LANES
SLAB_TILES = SUBLANES
N_SLABS = STATE_TILES // SLAB_TILES
W_IN_COLS = Q_LORA + KV_LORA + D_MLA + D_SSM + D_SSM + LANES
VMEM_LIMIT = 56 * 1024 * 1024


def _resident(shape):
    zeros = (0,) * len(shape)
    return pl.BlockSpec(shape, lambda *_: zeros, pipeline_mode=pl.Buffered(1))


def _sigmoid(x):
    return 1.0 / (1.0 + jnp.exp(-x))


def _params(*sem):
    return pltpu.CompilerParams(dimension_semantics=sem, vmem_limit_bytes=VMEM_LIMIT)


def _s5_prep_kernel(are_ref, aim_ref, lstep_ref, bre_ref, bim_ref,
                    abre_ref, abim_ref, bbre_ref, bbim_ref):
    a_re = are_ref[...]
    a_im = aim_ref[...]
    step = jnp.exp(lstep_ref[...])
    mag = jnp.exp(step * a_re)
    ab_re = mag * jnp.cos(step * a_im)
    ab_im = mag * jnp.sin(step * a_im)
    n_re = ab_re - 1.0
    den = a_re * a_re + a_im * a_im
    z_re = (n_re * a_re + ab_im * a_im) / den
    z_im = (ab_im * a_re - n_re * a_im) / den
    b_re = bre_ref[...]
    b_im = bim_ref[...]
    abre_ref[...] = ab_re
    abim_ref[...] = ab_im
    bbre_ref[...] = z_re * b_re - z_im * b_im
    bbim_ref[...] = z_re * b_im + z_im * b_re


def _s5_prep(a_re, a_im, log_step, b_re, b_im):
    depth = a_re.shape[0]
    rows = depth * GROUPS
    cols = NSTATE * GROUP_CH

    def expand(a):
        return jnp.broadcast_to(a[..., None], (depth, GROUPS, NSTATE, GROUP_CH)).reshape(rows, cols)

    step = jnp.broadcast_to(log_step[..., None], (depth, GROUPS, cols)).reshape(rows, cols)
    sds = jax.ShapeDtypeStruct((rows, cols), F32)
    ab_re, ab_im, bb_re, bb_im = pl.pallas_call(
        _s5_prep_kernel, out_shape=(sds, sds, sds, sds), name="s5_prep",
    )(expand(a_re), expand(a_im), step, b_re.reshape(rows, cols), b_im.reshape(rows, cols))
    shape4 = (depth, GROUPS, NSTATE, GROUP_CH)
    ab_re = ab_re.reshape(shape4)[..., 0]
    ab_im = ab_im.reshape(shape4)[..., 0]
    return ab_re, ab_im, bb_re.reshape(shape4), bb_im.reshape(shape4)


def _inproj_kernel(x_ref, gpre_ref, w_ref, gq_ref, gkv_ref,
                   cq_ref, ckv_ref, kr_ref, gm_ref, u_ref, gs_ref):
    x = x_ref[...]
    inv = lax.rsqrt(jnp.mean(x * x, axis=-1, keepdims=True) + NORM_EPS)
    h = (x * inv * gpre_ref[...]).astype(BF16)

    def proj(lo, hi):
        return jnp.dot(h, w_ref[:, lo:hi], preferred_element_type=F32)

    def latent_norm(c, g_ref):
        inv_c = lax.rsqrt(jnp.mean(c * c, axis=-1, keepdims=True) + NORM_EPS)
        return (c * inv_c * g_ref[...]).astype(BF16)

    o = 0
    cq_ref[...] = latent_norm(proj(o, o + Q_LORA), gq_ref)
    o += Q_LORA
    ckv_ref[...] = latent_norm(proj(o, o + KV_LORA), gkv_ref)
    o += KV_LORA
    gm_ref[...] = proj(o, o + D_MLA)
    o += D_MLA
    u_ref[...] = proj(o, o + D_SSM)
    o += D_SSM
    gs_ref[...] = proj(o, o + D_SSM)
    o += D_SSM
    kr_ref[...] = proj(o, o + LANES)


def _inproj(x, g_pre, w_in, g_q, g_kv, tm):
    L = x.shape[0]
    row = lambda n: pl.BlockSpec((tm, n), lambda i: (i, 0))
    out_shape = (
        jax.ShapeDtypeStruct((L, Q_LORA), BF16),
        jax.ShapeDtypeStruct((L, KV_LORA), BF16),
        jax.ShapeDtypeStruct((L, LANES), F32),
        jax.ShapeDtypeStruct((L, D_MLA), F32),
        jax.ShapeDtypeStruct((L, D_SSM), F32),
        jax.ShapeDtypeStruct((L, D_SSM), F32),
    )
    return pl.pallas_call(
        _inproj_kernel,
        grid=(L // tm,),
        in_specs=[row(D_MODEL), _resident((1, D_MODEL)), _resident((D_MODEL, W_IN_COLS)),
                  _resident((1, Q_LORA)), _resident((1, KV_LORA))],
        out_specs=(row(Q_LORA), row(KV_LORA), row(LANES), row(D_MLA), row(D_SSM), row(D_SSM)),
        out_shape=out_shape,
        compiler_params=_params("parallel"),
        name="inproj",
    )(x, g_pre, w_in, g_q, g_kv)


def _qkv_kernel(cq_ref, ckv_ref, kr_ref, posc_ref, posr_ref, fcol_ref, frow_ref,
                wqt_ref, wk_ref, wvt_ref, qt_ref, k_ref, vt_ref, *, q_scale):
    nt = (((1,), (1,)), ((), ()))
    cq = cq_ref[...]
    ckv = ckv_ref[...]
    half = ROPE // 2

    ang_t = fcol_ref[...] * posr_ref[...].astype(F32)
    cos_t = jnp.cos(ang_t)
    sin_t = jnp.sin(ang_t)
    ang = posc_ref[...].astype(F32) * frow_ref[...]
    cos_r = jnp.cos(ang)
    sin_r = jnp.sin(ang)

    kr = kr_ref[:, :ROPE]
    kr_rot = jnp.concatenate([-kr[:, half:], kr[:, :half]], axis=-1)
    k_rope = (kr * cos_r + kr_rot * sin_r).astype(BF16)
    k_nope = jnp.dot(ckv, wk_ref[...], preferred_element_type=F32)

    pad_row = lax.broadcasted_iota(jnp.int32, (V_ROWS - DV, vt_ref.shape[-1]), 0)
    ones_rows = jnp.where(pad_row == 0, 1.0, 0.0).astype(BF16)

    for h in range(HEADS):
        qt = lax.dot_general(wqt_ref[h], cq, nt, preferred_element_type=F32)
        qr = qt[NOPE:]
        qr_rot = jnp.concatenate([-qr[half:], qr[:half]], axis=0)
        qr = qr * cos_t + qr_rot * sin_t
        qt_ref[h, :NOPE, :] = (qt[:NOPE] * q_scale).astype(BF16)
        qt_ref[h, NOPE:, :] = (qr * q_scale).astype(BF16)
        k_ref[h, :, :NOPE] = k_nope[:, h * NOPE:(h + 1) * NOPE].astype(BF16)
        k_ref[h, :, NOPE:] = k_rope
        vt = lax.dot_general(wvt_ref[h], ckv, nt, preferred_element_type=F32)
        vt_ref[h, 0, :DV, :] = vt.astype(BF16)
        vt_ref[h, 0, DV:, :] = ones_rows


def _qkv(cq, ckv, kr, pos_col, pos_row, f_col, f_row, wqt, wk, wvt, tk):
    L = cq.shape[0]
    q_scale = math.log2(math.e) / math.sqrt(QK)
    row = lambda n: pl.BlockSpec((tk, n), lambda i: (i, 0))
    return pl.pallas_call(
        functools.partial(_qkv_kernel, q_scale=q_scale),
        grid=(L // tk,),
        in_specs=[row(Q_LORA), row(KV_LORA), row(LANES), row(1),
                  pl.BlockSpec((1, tk), lambda i: (0, i)),
                  _resident((ROPE, 1)), _resident((1, ROPE)),
                  _resident((HEADS, QK, Q_LORA)), _resident((KV_LORA, HEADS * NOPE)),
                  _resident((HEADS, DV, KV_LORA))],
        out_specs=(pl.BlockSpec((HEADS, QK, tk), lambda i: (0, 0, i)),
                   pl.BlockSpec((HEADS, tk, QK), lambda i: (0, i, 0)),
                   pl.BlockSpec((HEADS, 1, V_ROWS, tk), lambda i: (0, i, 0, 0))),
        out_shape=(jax.ShapeDtypeStruct((HEADS, QK, L), BF16),
                   jax.ShapeDtypeStruct((HEADS, L, QK), BF16),
                   jax.ShapeDtypeStruct((HEADS, L // tk, V_ROWS, tk), BF16)),
        compiler_params=_params("parallel"),
        name="qkv",
    )(cq, ckv, kr, pos_col, pos_row, f_col, f_row, wqt, wk, wvt)


def _attn_kernel(qt_ref, k_ref, vt_ref, gate_ref, o_ref, acc_ref, m_ref, alpha_ref, s_ref, p_ref, *, tq, tk):
    i = pl.program_id(1)
    m_ref[...] = jnp.full(m_ref.shape, -1e30, F32)
    acc_ref[...] = jnp.zeros(acc_ref.shape, F32)
    assert tq == 2 * tk
    n_chains = tq // MXU_COLS

    def cols(c):
        return slice(MXU_COLS * c, MXU_COLS * (c + 1))

    def visible(c, diag):
        return diag is None or diag * tk < MXU_COLS * (c + 1)

    def scores(j, slot, c, diag):
        if not visible(c, diag):
            return
        kc = k_ref[0, pl.ds(pl.multiple_of(j * tk, tk), tk), :]
        s = jnp.dot(kc, qt_ref[0, :, cols(c)], preferred_element_type=F32)
        if diag is not None and diag * tk + tk - 1 > MXU_COLS * c:
            key = diag * tk + lax.broadcasted_iota(jnp.int32, s.shape, 0)
            qry = MXU_COLS * c + lax.broadcasted_iota(jnp.int32, s.shape, 1)
            s = jnp.where(key <= qry, s, -1e30)
        s_ref[slot, c] = s.astype(BF16)

    def softmax(slot, c, diag):
        if not visible(c, diag):
            return
        s = s_ref[slot, c]
        m_blk = jnp.max(s.reshape(tk // BF16_ROWS, BF16_ROWS, MXU_COLS), axis=0)
        m_blk = jnp.max(m_blk.astype(F32), axis=0, keepdims=True)
        m_prev = m_ref[:, cols(c)]
        m_new = jnp.maximum(m_prev, m_blk)
        alpha_ref[slot, :, cols(c)] = jnp.exp2(m_prev - m_new)
        p_ref[slot, c] = jnp.exp2(s - m_new.astype(BF16))
        m_ref[:, cols(c)] = m_new

    def pv(j, slot, c, diag):
        if not visible(c, diag):
            return
        vc = vt_ref[0, j]
        upd = jnp.dot(vc, p_ref[slot, c], preferred_element_type=F32)
        acc_ref[:, cols(c)] = alpha_ref[slot, :, cols(c)] * acc_ref[:, cols(c)] + upd

    def step(score=None, soft=None, pvs=None):
        for c in range(n_chains):
            if soft is not None:
                softmax(soft[1], c, soft[2])
            if score is not None:
                scores(score[0], score[1], c, score[2])
            if pvs is not None:
                pv(pvs[0], pvs[1], c, pvs[2])

    @pl.when(i > 0)
    def _():
        step(score=(0, 0, None))
        step(score=(1, 1, None), soft=(0, 0, None))

        def body(t, carry):
            j = 2 * t
            step(score=(j + 2, 0, None), soft=(j + 1, 1, None), pvs=(j, 0, None))
            step(score=(j + 3, 1, None), soft=(j + 2, 0, None), pvs=(j + 1, 1, None))
            return carry

        lax.fori_loop(0, i - 1, body, 0)
        j = 2 * i - 2
        step(score=(j + 2, 0, 0), soft=(j + 1, 1, None), pvs=(j, 0, None))
        step(score=(j + 3, 1, 1), soft=(j + 2, 0, 0), pvs=(j + 1, 1, None))

    @pl.when(i == 0)
    def _():
        step(score=(0, 0, 0))
        step(score=(1, 1, 1), soft=(0, 0, 0))

    step(soft=(2 * i + 1, 1, 1), pvs=(2 * i, 0, 0))
    step(pvs=(2 * i + 1, 1, 1))

    out = (acc_ref[:DV, :] * (1.0 / acc_ref[DV:DV + 1, :])).T
    g = gate_ref[...]
    o_ref[...] = (out * (g * _sigmoid(g))).astype(BF16)


def _attention(qt, k, vt, gate, tq, tk):
    L = k.shape[1]
    return pl.pallas_call(
        functools.partial(_attn_kernel, tq=tq, tk=tk),
        grid=(HEADS, L // tq),
        in_specs=[pl.BlockSpec((1, QK, tq), lambda h, i: (h, 0, i)),
                  pl.BlockSpec((1, L, QK), lambda h, i: (h, 0, 0)),
                  pl.BlockSpec((1, L // tk, V_ROWS, tk), lambda h, i: (h, 0, 0, 0)),
                  pl.BlockSpec((tq, DV), lambda h, i: (i, h))],
        out_specs=pl.BlockSpec((tq, DV), lambda h, i: (i, h)),
        out_shape=jax.ShapeDtypeStruct((L, D_MLA), BF16),
        scratch_shapes=[pltpu.VMEM((V_ROWS, tq), F32), pltpu.VMEM((1, tq), F32),
                        pltpu.VMEM((2, 1, tq), F32),
                        pltpu.VMEM((2, tq // MXU_COLS, tk, MXU_COLS), BF16),
                        pltpu.VMEM((2, tq // MXU_COLS, tk, MXU_COLS), BF16)],
        compiler_params=_params("parallel", "arbitrary"),
        name="attention",
    )(qt, k, vt, gate)


def _gelu_tanh(x):
    c = math.sqrt(2.0 / math.pi)
    return 0.5 * x * (1.0 + jnp.tanh(c * (x + 0.044715 * (x * x * x))))


def _tile_rows(tile):
    return slice(SUBLANES * tile, SUBLANES * (tile + 1))


def _s5_kernel(u_ref, g_ref, wb_ref, are_ref, aim_ref, wcre_ref, wcim_ref, d_ref, wglu_ref,
               o_ref, hs_ref, st_ref, *, tt):
    nt8 = tt // SUBLANES

    @pl.when(pl.program_id(0) == 0)
    def _():
        st_ref[...] = jnp.zeros(st_ref.shape, F32)

    u = u_ref[...]
    ub = u.astype(BF16)
    groups_per_tile = LANES // GROUP_CH
    tiles_per_dot = groups_per_tile * NSTATE // LANES
    for s in range(D_SSM // LANES):
        r = jnp.dot(ub[:, LANES * s:LANES * (s + 1)], wb_ref[s], preferred_element_type=F32)
        for k in range(tiles_per_dot):
            re = r[:, LANES * k:LANES * (k + 1)]
            im = r[:, LANES * (tiles_per_dot + k):LANES * (tiles_per_dot + k + 1)]
            hs_ref[:, _tile_rows(tiles_per_dot * s + k), :] = re.reshape(nt8, SUBLANES, LANES)
            hs_ref[:, _tile_rows(STATE_TILES + tiles_per_dot * s + k), :] = im.reshape(nt8, SUBLANES, LANES)

    a_re = [are_ref[i] for i in range(N_SLABS)]
    a_im = [aim_ref[i] for i in range(N_SLABS)]

    def scan_body(ti, st):
        st = list(st)
        for sub in range(SUBLANES):
            for i in range(N_SLABS):
                re_rows = pl.ds(sub + SUBLANES * SLAB_TILES * i, SLAB_TILES, stride=SUBLANES)
                im_rows = pl.ds(sub + SUBLANES * (STATE_TILES + SLAB_TILES * i), SLAB_TILES,
                                stride=SUBLANES)
                h_re, h_im = st[i], st[N_SLABS + i]
                n_re = a_re[i] * h_re - a_im[i] * h_im + hs_ref[ti, re_rows, :]
                n_im = a_re[i] * h_im + a_im[i] * h_re + hs_ref[ti, im_rows, :]
                hs_ref[ti, re_rows, :] = n_re
                hs_ref[ti, im_rows, :] = n_im
                st[i], st[N_SLABS + i] = n_re, n_im
        return tuple(st)

    st = lax.fori_loop(0, nt8, scan_body, tuple(st_ref[i] for i in range(2 * N_SLABS)))
    for i in range(2 * N_SLABS):
        st_ref[i] = st[i]

    def state_cols(first_tile):
        tiles = [hs_ref[:, _tile_rows(first_tile + j), :].reshape(tt, LANES) for j in range(SLAB_TILES)]
        return jnp.concatenate(tiles, axis=-1).astype(BF16)

    ys = []
    for i in range(N_SLABS):
        y = jnp.dot(state_cols(SLAB_TILES * i), wcre_ref[i], preferred_element_type=F32)
        y += jnp.dot(state_cols(STATE_TILES + SLAB_TILES * i), wcim_ref[i], preferred_element_type=F32)
        ys.append(y)
    y = jnp.concatenate(ys, axis=-1) + d_ref[...] * u
    y = _gelu_tanh(y).astype(BF16)
    z = jnp.dot(y, wglu_ref[...], preferred_element_type=F32)
    out = z[:, :D_SSM] * _sigmoid(z[:, D_SSM:])
    g = g_ref[...]
    o_ref[...] = (out * (g * _sigmoid(g))).astype(BF16)


def _s5(u, gate, wb, a_re, a_im, wc_re, wc_im, d_skip, w_glu, tt):
    L = u.shape[0]
    row = pl.BlockSpec((tt, D_SSM), lambda c: (c, 0))
    cols_per_slab = SLAB_TILES * LANES
    return pl.pallas_call(
        functools.partial(_s5_kernel, tt=tt),
        grid=(L // tt,),
        in_specs=[row, row,
                  _resident((D_SSM // LANES, LANES, 2 * (LANES // GROUP_CH) * NSTATE)),
                  _resident((N_SLABS, SUBLANES, LANES)), _resident((N_SLABS, SUBLANES, LANES)),
                  _resident((N_SLABS, cols_per_slab, 2 * LANES)),
                  _resident((N_SLABS, cols_per_slab, 2 * LANES)),
                  _resident((1, D_SSM)), _resident((D_SSM, 2 * D_SSM))],
        out_specs=row,
        out_shape=jax.ShapeDtypeStruct((L, D_SSM), BF16),
        scratch_shapes=[pltpu.VMEM((tt // SUBLANES, 2 * STATE_TILES * SUBLANES, LANES), F32),
                        pltpu.VMEM((2 * N_SLABS, SUBLANES, LANES), F32)],
        compiler_params=_params("arbitrary"),
        name="s5",
    )(u, gate, wb, a_re, a_im, wc_re, wc_im, d_skip, w_glu)


def _outproj_kernel(mla_ref, ssm_ref, w_ref, gpost_ref, x_ref, o_ref):
    out = jnp.dot(mla_ref[...], w_ref[:D_MLA, :], preferred_element_type=F32)
    out += jnp.dot(ssm_ref[...], w_ref[D_MLA:, :], preferred_element_type=F32)
    inv = lax.rsqrt(jnp.mean(out * out, axis=-1, keepdims=True) + NORM_EPS)
    o_ref[...] = x_ref[...] + out * inv * gpost_ref[...]


def _outproj(mla, ssm, w_out, g_post, x, tm):
    L = x.shape[0]
    row = lambda n: pl.BlockSpec((tm, n), lambda i: (i, 0))
    return pl.pallas_call(
        _outproj_kernel,
        grid=(L // tm,),
        in_specs=[row(D_MLA), row(D_SSM), _resident((D_MLA + D_SSM, D_MODEL)),
                  _resident((1, D_MODEL)), row(D_MODEL)],
        out_specs=row(D_MODEL),
        out_shape=jax.ShapeDtypeStruct((L, D_MODEL), F32),
        compiler_params=_params("parallel"),
        name="outproj",
    )(mla, ssm, w_out, g_post, x)


def _pack_in_weights(w_in):
    s1 = Q_LORA
    s2 = s1 + KV_LORA
    s3 = s2 + ROPE
    depth = w_in.shape[0]
    pad = jnp.zeros((depth, D_MODEL, LANES - ROPE), w_in.dtype)
    return jnp.concatenate([w_in[..., :s2], w_in[..., s3:], w_in[..., s2:s3], pad], axis=-1).astype(BF16)


def _pack_b(bb_re, bb_im):
    depth = bb_re.shape[0]
    gpt = LANES // GROUP_CH
    eye = jnp.eye(gpt, dtype=F32)

    def pack(b):
        b = b.reshape(depth, GROUPS // gpt, gpt, NSTATE, GROUP_CH)
        return jnp.einsum('dsgph,gk->dsghkp', b, eye).reshape(depth, GROUPS // gpt, LANES, gpt * NSTATE)

    return jnp.concatenate([pack(bb_re), pack(bb_im)], axis=-1).astype(BF16)


def _pack_c(c):
    depth = c.shape[0]
    gps = SLAB_TILES * LANES // NSTATE
    eye = jnp.eye(gps, dtype=F32)
    c = c.reshape(depth, N_SLABS, gps, GROUP_CH, NSTATE)
    return jnp.einsum('dighp,gk->digpkh', c, eye).reshape(
        depth, N_SLABS, gps * NSTATE, gps * GROUP_CH).astype(BF16)


def kernel(x, positions, norm_pre, norm_post, w_in, q_norm, w_uq, kv_norm, w_ukv, ssm_a_re, ssm_a_im,
           ssm_b_re, ssm_b_im, ssm_c_re, ssm_c_im, ssm_d, ssm_log_step, w_glu, w_out):
    batch, L, _ = x.shape
    assert batch == 1
    depth = w_in.shape[0]
    tm = min(256, L)
    tq = min(1024, L)
    tk = tq // 2
    tt = min(256, L)

    pos_col = positions.reshape(L, 1)
    pos_row = positions.reshape(1, L)
    inv_freq = ROPE_THETA ** (-jnp.arange(0, ROPE, 2, dtype=F32) / ROPE)
    inv_freq = jnp.concatenate([inv_freq, inv_freq])
    f_col = inv_freq.reshape(ROPE, 1)
    f_row = inv_freq.reshape(1, ROPE)

    w_in_p = _pack_in_weights(w_in)
    wqt = w_uq.reshape(depth, Q_LORA, HEADS, QK).transpose(0, 2, 3, 1).astype(BF16)
    w_ukv4 = w_ukv.reshape(depth, KV_LORA, HEADS, NOPE + DV)
    wk = w_ukv4[..., :NOPE].reshape(depth, KV_LORA, HEADS * NOPE).astype(BF16)
    wvt = w_ukv4[..., NOPE:].transpose(0, 2, 3, 1).astype(BF16)
    w_glu_b = w_glu.astype(BF16)
    w_out_b = w_out.astype(BF16)

    ab_re, ab_im, bb_re, bb_im = _s5_prep(ssm_a_re, ssm_a_im, ssm_log_step, ssm_b_re, ssm_b_im)
    wb = _pack_b(bb_re, bb_im)
    slab_shape = (depth, N_SLABS, SUBLANES, LANES)
    a_re_s = ab_re.reshape(slab_shape)
    a_im_s = ab_im.reshape(slab_shape)
    wc_re = _pack_c(ssm_c_re)
    wc_im = _pack_c(-ssm_c_im)

    h = x[0]
    for i in range(depth):
        cq, ckv, kr, gate_mla, u, gate_ssm = _inproj(
            h, norm_pre[i][None], w_in_p[i], q_norm[i][None], kv_norm[i][None], tm)
        qt, k, vt = _qkv(cq, ckv, kr, pos_col, pos_row, f_col, f_row, wqt[i], wk[i], wvt[i], tk)
        mla = _attention(qt, k, vt, gate_mla, tq, tk)
        ssm = _s5(u, gate_ssm, wb[i], a_re_s[i], a_im_s[i], wc_re[i], wc_im[i],
                  ssm_d[i][None], w_glu_b[i], tt)
        h = _outproj(mla, ssm, w_out_b[i], norm_post[i][None], h, tm)
    return h[None]
```

```python
import functools
import math

import jax
import jax.numpy as jnp
from jax import lax
from jax.experimental import pallas as pl
from jax.experimental.pallas import tpu as pltpu

F32 = jnp.float32
BF16 = jnp.bfloat16

D_MODEL = 2048
HEADS = 8
NOPE = 128
ROPE = 64
QK = NOPE + ROPE
DV = 128
V_ROWS = DV + 16
Q_LORA = 512
KV_LORA = 256
D_MLA = HEADS * DV
D_SSM = 1024
GROUP_CH = 16
GROUPS = D_SSM // GROUP_CH
NSTATE = 64
ROPE_THETA = 10000.0
NORM_EPS = 1e-6
LANES = 128
SUBLANES = 8
MXU_COLS = 256
BF16_ROWS = 16
STATE_TILES = GROUPS * NSTATE // LANES
SLAB_TILES = SUBLANES
N_SLABS = STATE_TILES // SLAB_TILES
VMEM_LIMIT = 56 * 1024 * 1024


def _resident(shape):
    zeros = (0,) * len(shape)
    return pl.BlockSpec(shape, lambda *_: zeros, pipeline_mode=pl.Buffered(1))


def _sigmoid(x):
    return 1.0 / (1.0 + jnp.exp(-x))


def _params(*sem):
    return pltpu.CompilerParams(dimension_semantics=sem, vmem_limit_bytes=VMEM_LIMIT)


def _s5_prep_kernel(are_ref, aim_ref, lstep_ref, bre_ref, bim_ref,
                    abre_ref, abim_ref, bbre_ref, bbim_ref):
    a_re = are_ref[...]
    a_im = aim_ref[...]
    step = jnp.exp(lstep_ref[...])
    mag = jnp.exp(step * a_re)
    ab_re = mag * jnp.cos(step * a_im)
    ab_im = mag * jnp.sin(step * a_im)
    n_re = ab_re - 1.0
    den = a_re * a_re + a_im * a_im
    z_re = (n_re * a_re + ab_im * a_im) / den
    z_im = (ab_im * a_re - n_re * a_im) / den
    b_re = bre_ref[...]
    b_im = bim_ref[...]
    abre_ref[...] = ab_re
    abim_ref[...] = ab_im
    bbre_ref[...] = z_re * b_re - z_im * b_im
    bbim_ref[...] = z_re * b_im + z_im * b_re


def _s5_prep(a_re, a_im, log_step, b_re, b_im):
    depth = a_re.shape[0]
    rows = depth * GROUPS
    cols = NSTATE * GROUP_CH

    def expand(a):
        return jnp.broadcast_to(a[..., None], (depth, GROUPS, NSTATE, GROUP_CH)).reshape(rows, cols)

    step = jnp.broadcast_to(log_step[..., None], (depth, GROUPS, cols)).reshape(rows, cols)
    sds = jax.ShapeDtypeStruct((rows, cols), F32)
    ab_re, ab_im, bb_re, bb_im = pl.pallas_call(
        _s5_prep_kernel, out_shape=(sds, sds, sds, sds), name="s5_prep",
    )(expand(a_re), expand(a_im), step, b_re.reshape(rows, cols), b_im.reshape(rows, cols))
    shape4 = (depth, GROUPS, NSTATE, GROUP_CH)
    ab_re = ab_re.reshape(shape4)[..., 0]
    ab_im = ab_im.reshape(shape4)[..., 0]
    return ab_re, ab_im, bb_re.reshape(shape4), bb_im.reshape(shape4)


def _inproj_kernel(x_ref, gpre_ref, wlat_ref, wkr_ref, wwide_ref, gq_ref, gkv_ref,
                   cq_ref, ckv_ref, kr_ref, gm_ref, u_ref, gs_ref):
    x = x_ref[...]
    inv = lax.rsqrt(jnp.mean(x * x, axis=-1, keepdims=True) + NORM_EPS)
    h = (x * inv * gpre_ref[...]).astype(BF16)

    def proj(w_ref, lo, hi):
        return jnp.dot(h, w_ref[:, lo:hi], preferred_element_type=F32)

    def latent_norm(c, g_ref):
        inv_c = lax.rsqrt(jnp.mean(c * c, axis=-1, keepdims=True) + NORM_EPS)
        return (c * inv_c * g_ref[...]).astype(BF16)

    cq_ref[...] = latent_norm(proj(wlat_ref, 0, Q_LORA), gq_ref)
    ckv_ref[...] = latent_norm(proj(wlat_ref, Q_LORA, Q_LORA + KV_LORA), gkv_ref)
    kr_ref[...] = proj(wkr_ref, 0, LANES)
    gm_ref[...] = proj(wwide_ref, 0, D_MLA)
    u_ref[...] = proj(wwide_ref, D_MLA, D_MLA + D_SSM)
    gs_ref[...] = proj(wwide_ref, D_MLA + D_SSM, D_MLA + 2 * D_SSM)


def _inproj(x, g_pre, w_lat, w_kr, w_wide, g_q, g_kv, tm):
    L = x.shape[0]
    row = lambda n: pl.BlockSpec((tm, n), lambda i: (i, 0))
    out_shape = (
        jax.ShapeDtypeStruct((L, Q_LORA), BF16),
        jax.ShapeDtypeStruct((L, KV_LORA), BF16),
        jax.ShapeDtypeStruct((L, LANES), F32),
        jax.ShapeDtypeStruct((L, D_MLA), F32),
        jax.ShapeDtypeStruct((L, D_SSM), F32),
        jax.ShapeDtypeStruct((L, D_SSM), F32),
    )
    return pl.pallas_call(
        _inproj_kernel,
        grid=(L // tm,),
        in_specs=[row(D_MODEL), _resident((1, D_MODEL)), _resident((D_MODEL, Q_LORA + KV_LORA)),
                  _resident((D_MODEL, LANES)), _resident((D_MODEL, D_MLA + 2 * D_SSM)),
                  _resident((1, Q_LORA)), _resident((1, KV_LORA))],
        out_specs=(row(Q_LORA), row(KV_LORA), row(LANES), row(D_MLA), row(D_SSM), row(D_SSM)),
        out_shape=out_shape,
        compiler_params=_params("parallel"),
        name="inproj",
    )(x, g_pre, w_lat, w_kr, w_wide, g_q, g_kv)


def _rope_kernel(posc_ref, posr_ref, fcol_ref, frow_ref, cosr_ref, sinr_ref, cost_ref, sint_ref):
    ang = posc_ref[...].astype(F32) * frow_ref[...]
    cosr_ref[...] = jnp.cos(ang)
    sinr_ref[...] = jnp.sin(ang)
    ang_t = fcol_ref[...] * posr_ref[...].astype(F32)
    cost_ref[...] = jnp.cos(ang_t)
    sint_ref[...] = jnp.sin(ang_t)


def _rope_tables(positions, tm):
    L = positions.shape[-1]
    inv_freq = ROPE_THETA ** (-jnp.arange(0, ROPE, 2, dtype=F32) / ROPE)
    inv_freq = jnp.concatenate([inv_freq, inv_freq])
    row = pl.BlockSpec((tm, ROPE), lambda i: (i, 0))
    col = pl.BlockSpec((ROPE, tm), lambda i: (0, i))
    return pl.pallas_call(
        _rope_kernel,
        grid=(L // tm,),
        in_specs=[pl.BlockSpec((tm, 1), lambda i: (i, 0)), pl.BlockSpec((1, tm), lambda i: (0, i)),
                  _resident((ROPE, 1)), _resident((1, ROPE))],
        out_specs=(row, row, col, col),
        out_shape=(jax.ShapeDtypeStruct((L, ROPE), F32), jax.ShapeDtypeStruct((L, ROPE), F32),
                   jax.ShapeDtypeStruct((ROPE, L), F32), jax.ShapeDtypeStruct((ROPE, L), F32)),
        compiler_params=_params("parallel"),
        name="rope_tables",
    )(positions.reshape(L, 1), positions.reshape(1, L), inv_freq.reshape(ROPE, 1), inv_freq.reshape(1, ROPE))


def _qkv_kernel(cq_ref, ckv_ref, kr_ref, cosr_ref, sinr_ref, cost_ref, sint_ref,
                wqt_ref, wk_ref, wvt_ref, qt_ref, k_ref, vt_ref, *, q_scale):
    nt = (((1,), (1,)), ((), ()))
    cq = cq_ref[...]
    ckv = ckv_ref[...]
    half = ROPE // 2
    cos_r = cosr_ref[...]
    sin_r = sinr_ref[...]
    cos_t = cost_ref[...]
    sin_t = sint_ref[...]

    kr = kr_ref[:, :ROPE]
    kr_rot = jnp.concatenate([-kr[:, half:], kr[:, :half]], axis=-1)
    k_rope = (kr * cos_r + kr_rot * sin_r).astype(BF16)
    k_nope = jnp.dot(ckv, wk_ref[...], preferred_element_type=F32)

    pad_row = lax.broadcasted_iota(jnp.int32, (V_ROWS - DV, vt_ref.shape[-1]), 0)
    ones_rows = jnp.where(pad_row == 0, 1.0, 0.0).astype(BF16)

    for h in range(HEADS):
        qt = lax.dot_general(wqt_ref[h], cq, nt, preferred_element_type=F32)
        qr = qt[NOPE:]
        qr_rot = jnp.concatenate([-qr[half:], qr[:half]], axis=0)
        qr = qr * cos_t + qr_rot * sin_t
        qt_ref[h, :NOPE, :] = (qt[:NOPE] * q_scale).astype(BF16)
        qt_ref[h, NOPE:, :] = (qr * q_scale).astype(BF16)
        k_ref[h, :, :NOPE] = k_nope[:, h * NOPE:(h + 1) * NOPE].astype(BF16)
        k_ref[h, :, NOPE:] = k_rope
        vt = lax.dot_general(wvt_ref[h], ckv, nt, preferred_element_type=F32)
        vt_ref[h, 0, :DV, :] = vt.astype(BF16)
        vt_ref[h, 0, DV:, :] = ones_rows


def _qkv(cq, ckv, kr, rope, wqt, wk, wvt, tk):
    L = cq.shape[0]
    q_scale = math.log2(math.e) / math.sqrt(QK)
    row = lambda n: pl.BlockSpec((tk, n), lambda i: (i, 0))
    return pl.pallas_call(
        functools.partial(_qkv_kernel, q_scale=q_scale),
        grid=(L // tk,),
        in_specs=[row(Q_LORA), row(KV_LORA), row(LANES), row(ROPE), row(ROPE),
                  pl.BlockSpec((ROPE, tk), lambda i: (0, i)), pl.BlockSpec((ROPE, tk), lambda i: (0, i)),
                  _resident((HEADS, QK, Q_LORA)), _resident((KV_LORA, HEADS * NOPE)),
                  _resident((HEADS, DV, KV_LORA))],
        out_specs=(pl.BlockSpec((HEADS, QK, tk), lambda i: (0, 0, i)),
                   pl.BlockSpec((HEADS, tk, QK), lambda i: (0, i, 0)),
                   pl.BlockSpec((HEADS, 1, V_ROWS, tk), lambda i: (0, i, 0, 0))),
        out_shape=(jax.ShapeDtypeStruct((HEADS, QK, L), BF16),
                   jax.ShapeDtypeStruct((HEADS, L, QK), BF16),
                   jax.ShapeDtypeStruct((HEADS, L // tk, V_ROWS, tk), BF16)),
        compiler_params=_params("parallel"),
        name="qkv",
    )(cq, ckv, kr, *rope, wqt, wk, wvt)


def _attn_kernel(qt_ref, k_ref, vt_ref, gate_ref, o_ref, acc_ref, m_ref, alpha_ref, s_ref, p_ref, *, tq, tk):
    i = pl.program_id(1)
    m_ref[...] = jnp.full(m_ref.shape, -1e30, F32)
    acc_ref[...] = jnp.zeros(acc_ref.shape, F32)
    assert tq == 2 * tk
    n_chains = tq // MXU_COLS

    def cols(c):
        return slice(MXU_COLS * c, MXU_COLS * (c + 1))

    def visible(c, diag):
        return diag is None or diag * tk < MXU_COLS * (c + 1)

    def scores(j, slot, c, diag):
        if not visible(c, diag):
            return
        kc = k_ref[0, pl.ds(pl.multiple_of(j * tk, tk), tk), :]
        s = jnp.dot(kc, qt_ref[0, :, cols(c)], preferred_element_type=F32)
        if diag is not None and diag * tk + tk - 1 > MXU_COLS * c:
            key = diag * tk + lax.broadcasted_iota(jnp.int32, s.shape, 0)
            qry = MXU_COLS * c + lax.broadcasted_iota(jnp.int32, s.shape, 1)
            s = jnp.where(key <= qry, s, -1e30)
        s_ref[slot, c] = s.astype(BF16)

    def softmax(slot, c, diag):
        if not visible(c, diag):
            return
        s = s_ref[slot, c]
        m_blk = jnp.max(s.reshape(tk // BF16_ROWS, BF16_ROWS, MXU_COLS), axis=0)
        m_blk = jnp.max(m_blk.astype(F32), axis=0, keepdims=True)
        m_prev = m_ref[:, cols(c)]
        m_new = jnp.maximum(m_prev, m_blk)
        alpha_ref[slot, :, cols(c)] = jnp.exp2(m_prev - m_new)
        p_ref[slot, c] = jnp.exp2(s - m_new.astype(BF16))
        m_ref[:, cols(c)] = m_new

    def pv(j, slot, c, diag):
        if not visible(c, diag):
            return
        vc = vt_ref[0, j]
        upd = jnp.dot(vc, p_ref[slot, c], preferred_element_type=F32)
        acc_ref[:, cols(c)] = alpha_ref[slot, :, cols(c)] * acc_ref[:, cols(c)] + upd

    def step(score=None, soft=None, pvs=None):
        for c in range(n_chains):
            if soft is not None:
                softmax(soft[1], c, soft[2])
            if score is not None:
                scores(score[0], score[1], c, score[2])
            if pvs is not None:
                pv(pvs[0], pvs[1], c, pvs[2])

    @pl.when(i > 0)
    def _():
        step(score=(0, 0, None))
        step(score=(1, 1, None), soft=(0, 0, None))

        def pair(t):
            j = 2 * t
            step(score=(j + 2, 0, None), soft=(j + 1, 1, None), pvs=(j, 0, None))
            step(score=(j + 3, 1, None), soft=(j + 2, 0, None), pvs=(j + 1, 1, None))

        def body(t, carry):
            pair(2 * t)
            pair(2 * t + 1)
            return carry

        n_pairs = i - 1
        lax.fori_loop(0, n_pairs // 2, body, 0)

        @pl.when(n_pairs % 2 == 1)
        def _():
            pair(n_pairs - 1)

        j = 2 * i - 2
        step(score=(j + 2, 0, 0), soft=(j + 1, 1, None), pvs=(j, 0, None))
        step(score=(j + 3, 1, 1), soft=(j + 2, 0, 0), pvs=(j + 1, 1, None))

    @pl.when(i == 0)
    def _():
        step(score=(0, 0, 0))
        step(score=(1, 1, 1), soft=(0, 0, 0))

    step(soft=(2 * i + 1, 1, 1), pvs=(2 * i, 0, 0))
    step(pvs=(2 * i + 1, 1, 1))

    out = (acc_ref[:DV, :] * (1.0 / acc_ref[DV:DV + 1, :])).T
    g = gate_ref[...]
    o_ref[...] = (out * (g * _sigmoid(g))).astype(BF16)


def _attention(qt, k, vt, gate, tq, tk):
    L = k.shape[1]
    return pl.pallas_call(
        functools.partial(_attn_kernel, tq=tq, tk=tk),
        grid=(HEADS, L // tq),
        in_specs=[pl.BlockSpec((1, QK, tq), lambda h, i: (h, 0, i)),
                  pl.BlockSpec((1, L, QK), lambda h, i: (h, 0, 0)),
                  pl.BlockSpec((1, L // tk, V_ROWS, tk), lambda h, i: (h, 0, 0, 0)),
                  pl.BlockSpec((tq, DV), lambda h, i: (i, h))],
        out_specs=pl.BlockSpec((tq, DV), lambda h, i: (i, h)),
        out_shape=jax.ShapeDtypeStruct((L, D_MLA), BF16),
        scratch_shapes=[pltpu.VMEM((V_ROWS, tq), F32), pltpu.VMEM((1, tq), F32),
                        pltpu.VMEM((2, 1, tq), F32),
                        pltpu.VMEM((2, tq // MXU_COLS, tk, MXU_COLS), BF16),
                        pltpu.VMEM((2, tq // MXU_COLS, tk, MXU_COLS), BF16)],
        compiler_params=_params("parallel", "arbitrary"),
        name="attention",
    )(qt, k, vt, gate)


def _gelu_tanh(x):
    c = math.sqrt(2.0 / math.pi)
    return 0.5 * x * (1.0 + jnp.tanh(c * (x + 0.044715 * (x * x * x))))


def _tile_rows(tile):
    return slice(SUBLANES * tile, SUBLANES * (tile + 1))


def _s5_kernel(u_ref, g_ref, wb_ref, are_ref, aim_ref, wcre_ref, wcim_ref, d_ref, wglu_ref,
               o_ref, hs_ref, st_ref, *, tt):
    nt8 = tt // SUBLANES

    @pl.when(pl.program_id(0) == 0)
    def _():
        st_ref[...] = jnp.zeros(st_ref.shape, F32)

    u = u_ref[...]
    ub = u.astype(BF16)
    groups_per_tile = LANES // GROUP_CH
    tiles_per_dot = groups_per_tile * NSTATE // LANES
    for s in range(D_SSM // LANES):
        r = jnp.dot(ub[:, LANES * s:LANES * (s + 1)], wb_ref[s], preferred_element_type=F32)
        for k in range(tiles_per_dot):
            re = r[:, LANES * k:LANES * (k + 1)]
            im = r[:, LANES * (tiles_per_dot + k):LANES * (tiles_per_dot + k + 1)]
            hs_ref[:, _tile_rows(tiles_per_dot * s + k), :] = re.reshape(nt8, SUBLANES, LANES)
            hs_ref[:, _tile_rows(STATE_TILES + tiles_per_dot * s + k), :] = im.reshape(nt8, SUBLANES, LANES)

    a_re = [are_ref[i] for i in range(N_SLABS)]
    a_im = [aim_ref[i] for i in range(N_SLABS)]

    def scan_body(ti, st):
        st = list(st)
        for sub in range(SUBLANES):
            for i in range(N_SLABS):
                re_rows = pl.ds(sub + SUBLANES * SLAB_TILES * i, SLAB_TILES, stride=SUBLANES)
                im_rows = pl.ds(sub + SUBLANES * (STATE_TILES + SLAB_TILES * i), SLAB_TILES,
                                stride=SUBLANES)
                h_re, h_im = st[i], st[N_SLABS + i]
                n_re = a_re[i] * h_re - a_im[i] * h_im + hs_ref[ti, re_rows, :]
                n_im = a_re[i] * h_im + a_im[i] * h_re + hs_ref[ti, im_rows, :]
                hs_ref[ti, re_rows, :] = n_re
                hs_ref[ti, im_rows, :] = n_im
                st[i], st[N_SLABS + i] = n_re, n_im
        return tuple(st)

    st = lax.fori_loop(0, nt8, scan_body, tuple(st_ref[i] for i in range(2 * N_SLABS)))
    for i in range(2 * N_SLABS):
        st_ref[i] = st[i]

    def state_cols(first_tile):
        tiles = [hs_ref[:, _tile_rows(first_tile + j), :].reshape(tt, LANES) for j in range(SLAB_TILES)]
        return jnp.concatenate(tiles, axis=-1).astype(BF16)

    ys = []
    for i in range(N_SLABS):
        y = jnp.dot(state_cols(SLAB_TILES * i), wcre_ref[i], preferred_element_type=F32)
        y += jnp.dot(state_cols(STATE_TILES + SLAB_TILES * i), wcim_ref[i], preferred_element_type=F32)
        ys.append(y)
    y = jnp.concatenate(ys, axis=-1) + d_ref[...] * u
    y = _gelu_tanh(y).astype(BF16)
    z = jnp.dot(y, wglu_ref[...], preferred_element_type=F32)
    out = z[:, :D_SSM] * _sigmoid(z[:, D_SSM:])
    g = g_ref[...]
    o_ref[...] = (out * (g * _sigmoid(g))).astype(BF16)


def _s5(u, gate, wb, a_re, a_im, wc_re, wc_im, d_skip, w_glu, tt):
    L = u.shape[0]
    row = pl.BlockSpec((tt, D_SSM), lambda c: (c, 0))
    cols_per_slab = SLAB_TILES * LANES
    return pl.pallas_call(
        functools.partial(_s5_kernel, tt=tt),
        grid=(L // tt,),
        in_specs=[row, row,
                  _resident((D_SSM // LANES, LANES, 2 * (LANES // GROUP_CH) * NSTATE)),
                  _resident((N_SLABS, SUBLANES, LANES)), _resident((N_SLABS, SUBLANES, LANES)),
                  _resident((N_SLABS, cols_per_slab, 2 * LANES)),
                  _resident((N_SLABS, cols_per_slab, 2 * LANES)),
                  _resident((1, D_SSM)), _resident((D_SSM, 2 * D_SSM))],
        out_specs=row,
        out_shape=jax.ShapeDtypeStruct((L, D_SSM), BF16),
        scratch_shapes=[pltpu.VMEM((tt // SUBLANES, 2 * STATE_TILES * SUBLANES, LANES), F32),
                        pltpu.VMEM((2 * N_SLABS, SUBLANES, LANES), F32)],
        compiler_params=_params("arbitrary"),
        name="s5",
    )(u, gate, wb, a_re, a_im, wc_re, wc_im, d_skip, w_glu)


def _outproj_kernel(mla_ref, ssm_ref, w_ref, gpost_ref, x_ref, o_ref):
    out = jnp.dot(mla_ref[...], w_ref[:D_MLA, :], preferred_element_type=F32)
    out += jnp.dot(ssm_ref[...], w_ref[D_MLA:, :], preferred_element_type=F32)
    inv = lax.rsqrt(jnp.mean(out * out, axis=-1, keepdims=True) + NORM_EPS)
    o_ref[...] = x_ref[...] + out * inv * gpost_ref[...]


def _outproj(mla, ssm, w_out, g_post, x, tm):
    L = x.shape[0]
    row = lambda n: pl.BlockSpec((tm, n), lambda i: (i, 0))
    return pl.pallas_call(
        _outproj_kernel,
        grid=(L // tm,),
        in_specs=[row(D_MLA), row(D_SSM), _resident((D_MLA + D_SSM, D_MODEL)),
                  _resident((1, D_MODEL)), row(D_MODEL)],
        out_specs=row(D_MODEL),
        out_shape=jax.ShapeDtypeStruct((L, D_MODEL), F32),
        compiler_params=_params("parallel"),
        name="outproj",
    )(mla, ssm, w_out, g_post, x)


def _split_in_weights(w_in):
    s2 = Q_LORA + KV_LORA
    s3 = s2 + ROPE
    w_kr = jnp.pad(w_in[..., s2:s3].astype(BF16), ((0, 0), (0, 0), (0, LANES - ROPE)))
    return w_in[..., :s2].astype(BF16), w_kr, w_in[..., s3:].astype(BF16)


def _pack_b(bb_re, bb_im):
    depth = bb_re.shape[0]
    gpt = LANES // GROUP_CH
    eye = jnp.eye(gpt, dtype=F32)

    def pack(b):
        b = b.reshape(depth, GROUPS // gpt, gpt, NSTATE, GROUP_CH)
        return jnp.einsum('dsgph,gk->dsghkp', b, eye).reshape(depth, GROUPS // gpt, LANES, gpt * NSTATE)

    return jnp.concatenate([pack(bb_re), pack(bb_im)], axis=-1).astype(BF16)


def _pack_c(c):
    depth = c.shape[0]
    gps = SLAB_TILES * LANES // NSTATE
    eye = jnp.eye(gps, dtype=F32)
    c = c.reshape(depth, N_SLABS, gps, GROUP_CH, NSTATE)
    return jnp.einsum('dighp,gk->digpkh', c, eye).reshape(
        depth, N_SLABS, gps * NSTATE, gps * GROUP_CH).astype(BF16)


def kernel(x, positions, norm_pre, norm_post, w_in, q_norm, w_uq, kv_norm, w_ukv, ssm_a_re, ssm_a_im,
           ssm_b_re, ssm_b_im, ssm_c_re, ssm_c_im, ssm_d, ssm_log_step, w_glu, w_out):
    batch, L, _ = x.shape
    assert batch == 1
    depth = w_in.shape[0]
    tm = min(256, L)
    tm_out = min(512, L)
    tq = min(1024, L)
    tk = tq // 2
    tt = min(256, L)

    rope = _rope_tables(positions, tk)

    w_lat, w_kr, w_wide = _split_in_weights(w_in)
    wqt = w_uq.reshape(depth, Q_LORA, HEADS, QK).transpose(0, 2, 3, 1).astype(BF16)
    w_ukv4 = w_ukv.reshape(depth, KV_LORA, HEADS, NOPE + DV)
    wk = w_ukv4[..., :NOPE].reshape(depth, KV_LORA, HEADS * NOPE).astype(BF16)
    wvt = w_ukv4[..., NOPE:].transpose(0, 2, 3, 1).astype(BF16)
    w_glu_b = w_glu.astype(BF16)
    w_out_b = w_out.astype(BF16)

    ab_re, ab_im, bb_re, bb_im = _s5_prep(ssm_a_re, ssm_a_im, ssm_log_step, ssm_b_re, ssm_b_im)
    wb = _pack_b(bb_re, bb_im)
    slab_shape = (depth, N_SLABS, SUBLANES, LANES)
    a_re_s = ab_re.reshape(slab_shape)
    a_im_s = ab_im.reshape(slab_shape)
    wc_re = _pack_c(ssm_c_re)
    wc_im = _pack_c(-ssm_c_im)

    h = x[0]
    for i in range(depth):
        cq, ckv, kr, gate_mla, u, gate_ssm = _inproj(
            h, norm_pre[i][None], w_lat[i], w_kr[i], w_wide[i], q_norm[i][None], kv_norm[i][None], tm)
        qt, k, vt = _qkv(cq, ckv, kr, rope, wqt[i], wk[i], wvt[i], tk)
        mla = _attention(qt, k, vt, gate_mla, tq, tk)
        ssm = _s5(u, gate_ssm, wb[i], a_re_s[i], a_im_s[i], wc_re[i], wc_im[i],
                  ssm_d[i][None], w_glu_b[i], tt)
        h = _outproj(mla, ssm, w_out_b[i], norm_post[i][None], h, tm_out)
    return h[None]
```

```python
import functools
import math

import jax
import jax.numpy as jnp
from jax import lax
from jax.experimental import pallas as pl
from jax.experimental.pallas import tpu as pltpu

F32 = jnp.float32
BF16 = jnp.bfloat16

D_MODEL = 2048
HEADS = 8
NOPE = 128
ROPE = 64
QK = NOPE + ROPE
DV = 128
V_ROWS = DV + 16
Q_LORA = 512
KV_LORA = 256
D_MLA = HEADS * DV
D_SSM = 1024
GROUP_CH = 16
GROUPS = D_SSM // GROUP_CH
NSTATE = 64
ROPE_THETA = 10000.0
NORM_EPS = 1e-6
LANES = 128
SUBLANES = 8
MXU_COLS = 256
BF16_ROWS = 16
STATE_TILES = GROUPS * NSTATE // LANES
SLAB_TILES = SUBLANES
N_SLABS = STATE_TILES // SLAB_TILES
VMEM_LIMIT = 56 * 1024 * 1024


def _resident(shape):
    zeros = (0,) * len(shape)
    return pl.BlockSpec(shape, lambda *_: zeros, pipeline_mode=pl.Buffered(1))


def _sigmoid(x):
    return 1.0 / (1.0 + jnp.exp(-x))


def _params(*sem):
    return pltpu.CompilerParams(dimension_semantics=sem, vmem_limit_bytes=VMEM_LIMIT)


def _s5_prep_kernel(are_ref, aim_ref, lstep_ref, bre_ref, bim_ref,
                    abre_ref, abim_ref, bbre_ref, bbim_ref):
    a_re = are_ref[...]
    a_im = aim_ref[...]
    step = jnp.exp(lstep_ref[...])
    mag = jnp.exp(step * a_re)
    ab_re = mag * jnp.cos(step * a_im)
    ab_im = mag * jnp.sin(step * a_im)
    n_re = ab_re - 1.0
    den = a_re * a_re + a_im * a_im
    z_re = (n_re * a_re + ab_im * a_im) / den
    z_im = (ab_im * a_re - n_re * a_im) / den
    b_re = bre_ref[...]
    b_im = bim_ref[...]
    abre_ref[...] = ab_re
    abim_ref[...] = ab_im
    bbre_ref[...] = z_re * b_re - z_im * b_im
    bbim_ref[...] = z_re * b_im + z_im * b_re


def _s5_prep(a_re, a_im, log_step, b_re, b_im):
    depth = a_re.shape[0]
    rows = depth * GROUPS
    cols = NSTATE * GROUP_CH

    def expand(a):
        return jnp.broadcast_to(a[..., None], (depth, GROUPS, NSTATE, GROUP_CH)).reshape(rows, cols)

    step = jnp.broadcast_to(log_step[..., None], (depth, GROUPS, cols)).reshape(rows, cols)
    sds = jax.ShapeDtypeStruct((rows, cols), F32)
    ab_re, ab_im, bb_re, bb_im = pl.pallas_call(
        _s5_prep_kernel, out_shape=(sds, sds, sds, sds), name="s5_prep",
    )(expand(a_re), expand(a_im), step, b_re.reshape(rows, cols), b_im.reshape(rows, cols))
    shape4 = (depth, GROUPS, NSTATE, GROUP_CH)
    ab_re = ab_re.reshape(shape4)[..., 0]
    ab_im = ab_im.reshape(shape4)[..., 0]
    return ab_re, ab_im, bb_re.reshape(shape4), bb_im.reshape(shape4)


def _inproj_kernel(x_ref, gpre_ref, wlat_ref, wkr_ref, wwide_ref, gq_ref, gkv_ref,
                   cq_ref, ckv_ref, kr_ref, gm_ref, u_ref, gs_ref):
    x = x_ref[...]
    inv = lax.rsqrt(jnp.mean(x * x, axis=-1, keepdims=True) + NORM_EPS)
    h = (x * inv * gpre_ref[...]).astype(BF16)

    def proj(w_ref, lo, hi):
        return jnp.dot(h, w_ref[:, lo:hi], preferred_element_type=F32)

    def latent_norm(c, g_ref):
        inv_c = lax.rsqrt(jnp.mean(c * c, axis=-1, keepdims=True) + NORM_EPS)
        return (c * inv_c * g_ref[...]).astype(BF16)

    cq_ref[...] = latent_norm(proj(wlat_ref, 0, Q_LORA), gq_ref)
    ckv_ref[...] = latent_norm(proj(wlat_ref, Q_LORA, Q_LORA + KV_LORA), gkv_ref)
    kr_ref[...] = proj(wkr_ref, 0, LANES)
    gm_ref[...] = proj(wwide_ref, 0, D_MLA)
    u_ref[...] = proj(wwide_ref, D_MLA, D_MLA + D_SSM)
    gs_ref[...] = proj(wwide_ref, D_MLA + D_SSM, D_MLA + 2 * D_SSM)


def _inproj(x, g_pre, w_lat, w_kr, w_wide, g_q, g_kv, tm):
    L = x.shape[0]
    row = lambda n: pl.BlockSpec((tm, n), lambda i: (i, 0))
    out_shape = (
        jax.ShapeDtypeStruct((L, Q_LORA), BF16),
        jax.ShapeDtypeStruct((L, KV_LORA), BF16),
        jax.ShapeDtypeStruct((L, LANES), F32),
        jax.ShapeDtypeStruct((L, D_MLA), F32),
        jax.ShapeDtypeStruct((L, D_SSM), F32),
        jax.ShapeDtypeStruct((L, D_SSM), F32),
    )
    return pl.pallas_call(
        _inproj_kernel,
        grid=(L // tm,),
        in_specs=[row(D_MODEL), _resident((1, D_MODEL)), _resident((D_MODEL, Q_LORA + KV_LORA)),
                  _resident((D_MODEL, LANES)), _resident((D_MODEL, D_MLA + 2 * D_SSM)),
                  _resident((1, Q_LORA)), _resident((1, KV_LORA))],
        out_specs=(row(Q_LORA), row(KV_LORA), row(LANES), row(D_MLA), row(D_SSM), row(D_SSM)),
        out_shape=out_shape,
        compiler_params=_params("parallel"),
        name="inproj",
    )(x, g_pre, w_lat, w_kr, w_wide, g_q, g_kv)


def _rope_kernel(posc_ref, posr_ref, fcol_ref, frow_ref, cosr_ref, sinr_ref, cost_ref, sint_ref):
    ang = posc_ref[...].astype(F32) * frow_ref[...]
    cosr_ref[...] = jnp.cos(ang)
    sinr_ref[...] = jnp.sin(ang)
    ang_t = fcol_ref[...] * posr_ref[...].astype(F32)
    cost_ref[...] = jnp.cos(ang_t)
    sint_ref[...] = jnp.sin(ang_t)


def _rope_tables(positions, tm):
    L = positions.shape[-1]
    inv_freq = ROPE_THETA ** (-jnp.arange(0, ROPE, 2, dtype=F32) / ROPE)
    inv_freq = jnp.concatenate([inv_freq, inv_freq])
    row = pl.BlockSpec((tm, ROPE), lambda i: (i, 0))
    col = pl.BlockSpec((ROPE, tm), lambda i: (0, i))
    return pl.pallas_call(
        _rope_kernel,
        grid=(L // tm,),
        in_specs=[pl.BlockSpec((tm, 1), lambda i: (i, 0)), pl.BlockSpec((1, tm), lambda i: (0, i)),
                  _resident((ROPE, 1)), _resident((1, ROPE))],
        out_specs=(row, row, col, col),
        out_shape=(jax.ShapeDtypeStruct((L, ROPE), F32), jax.ShapeDtypeStruct((L, ROPE), F32),
                   jax.ShapeDtypeStruct((ROPE, L), F32), jax.ShapeDtypeStruct((ROPE, L), F32)),
        compiler_params=_params("parallel"),
        name="rope_tables",
    )(positions.reshape(L, 1), positions.reshape(1, L), inv_freq.reshape(ROPE, 1), inv_freq.reshape(1, ROPE))


def _qkv_kernel(cq_ref, ckv_ref, kr_ref, cosr_ref, sinr_ref, cost_ref, sint_ref,
                wqt_ref, wk_ref, wvt_ref, qt_ref, k_ref, vt_ref, *, q_scale):
    nt = (((1,), (1,)), ((), ()))
    cq = cq_ref[...]
    ckv = ckv_ref[...]
    half = ROPE // 2
    cos_r = cosr_ref[...]
    sin_r = sinr_ref[...]
    cos_t = cost_ref[...]
    sin_t = sint_ref[...]

    kr = kr_ref[:, :ROPE]
    kr_rot = jnp.concatenate([-kr[:, half:], kr[:, :half]], axis=-1)
    k_rope = (kr * cos_r + kr_rot * sin_r).astype(BF16)
    k_nope = jnp.dot(ckv, wk_ref[...], preferred_element_type=F32)

    pad_row = lax.broadcasted_iota(jnp.int32, (V_ROWS - DV, vt_ref.shape[-1]), 0)
    ones_rows = jnp.where(pad_row == 0, 1.0, 0.0).astype(BF16)

    for h in range(HEADS):
        qt = lax.dot_general(wqt_ref[h], cq, nt, preferred_element_type=F32)
        qr = qt[NOPE:]
        qr_rot = jnp.concatenate([-qr[half:], qr[:half]], axis=0)
        qr = qr * cos_t + qr_rot * sin_t
        qt_ref[h, :NOPE, :] = (qt[:NOPE] * q_scale).astype(BF16)
        qt_ref[h, NOPE:, :] = (qr * q_scale).astype(BF16)
        k_ref[h, :, :NOPE] = k_nope[:, h * NOPE:(h + 1) * NOPE].astype(BF16)
        k_ref[h, :, NOPE:] = k_rope
        vt = lax.dot_general(wvt_ref[h], ckv, nt, preferred_element_type=F32)
        vt_ref[h, 0, :DV, :] = vt.astype(BF16)
        vt_ref[h, 0, DV:, :] = ones_rows


def _qkv(cq, ckv, kr, rope, wqt, wk, wvt, tk):
    L = cq.shape[0]
    q_scale = math.log2(math.e) / math.sqrt(QK)
    row = lambda n: pl.BlockSpec((tk, n), lambda i: (i, 0))
    return pl.pallas_call(
        functools.partial(_qkv_kernel, q_scale=q_scale),
        grid=(L // tk,),
        in_specs=[row(Q_LORA), row(KV_LORA), row(LANES), row(ROPE), row(ROPE),
                  pl.BlockSpec((ROPE, tk), lambda i: (0, i)), pl.BlockSpec((ROPE, tk), lambda i: (0, i)),
                  _resident((HEADS, QK, Q_LORA)), _resident((KV_LORA, HEADS * NOPE)),
                  _resident((HEADS, DV, KV_LORA))],
        out_specs=(pl.BlockSpec((HEADS, QK, tk), lambda i: (0, 0, i)),
                   pl.BlockSpec((HEADS, tk, QK), lambda i: (0, i, 0)),
                   pl.BlockSpec((HEADS, 1, V_ROWS, tk), lambda i: (0, i, 0, 0))),
        out_shape=(jax.ShapeDtypeStruct((HEADS, QK, L), BF16),
                   jax.ShapeDtypeStruct((HEADS, L, QK), BF16),
                   jax.ShapeDtypeStruct((HEADS, L // tk, V_ROWS, tk), BF16)),
        compiler_params=_params("parallel"),
        name="qkv",
    )(cq, ckv, kr, *rope, wqt, wk, wvt)


def _attn_kernel(qt_ref, k_ref, vt_ref, gate_ref, o_ref, acc_ref, m_ref, alpha_ref, s_ref, p_ref, *, tq, tk):
    i = pl.program_id(1)
    m_ref[...] = jnp.full(m_ref.shape, -1e30, F32)
    acc_ref[...] = jnp.zeros(acc_ref.shape, F32)
    assert tq == 2 * tk
    n_chains = tq // MXU_COLS

    def cols(c):
        return slice(MXU_COLS * c, MXU_COLS * (c + 1))

    def visible(c, diag):
        return diag is None or diag * tk < MXU_COLS * (c + 1)

    def scores(j, slot, c, diag):
        if not visible(c, diag):
            return
        kc = k_ref[0, pl.ds(pl.multiple_of(j * tk, tk), tk), :]
        s = jnp.dot(kc, qt_ref[0, :, cols(c)], preferred_element_type=F32)
        if diag is not None and diag * tk + tk - 1 > MXU_COLS * c:
            key = diag * tk + lax.broadcasted_iota(jnp.int32, s.shape, 0)
            qry = MXU_COLS * c + lax.broadcasted_iota(jnp.int32, s.shape, 1)
            s = jnp.where(key <= qry, s, -1e30)
        s_ref[slot, c] = s.astype(BF16)

    def softmax(slot, c, diag):
        if not visible(c, diag):
            return
        s = s_ref[slot, c]
        m_blk = jnp.max(s.reshape(tk // BF16_ROWS, BF16_ROWS, MXU_COLS), axis=0)
        m_blk = jnp.max(m_blk.astype(F32), axis=0, keepdims=True)
        m_prev = m_ref[:, cols(c)]
        m_new = jnp.maximum(m_prev, m_blk)
        alpha_ref[slot, :, cols(c)] = jnp.exp2(m_prev - m_new)
        p_ref[slot, c] = jnp.exp2(s - m_new.astype(BF16))
        m_ref[:, cols(c)] = m_new

    def pv(j, slot, c, diag):
        if not visible(c, diag):
            return
        vc = vt_ref[0, j]
        upd = jnp.dot(vc, p_ref[slot, c], preferred_element_type=F32)
        acc_ref[:, cols(c)] = alpha_ref[slot, :, cols(c)] * acc_ref[:, cols(c)] + upd

    def step(score=None, soft=None, pvs=None):
        for c in range(n_chains):
            if soft is not None:
                softmax(soft[1], c, soft[2])
            if score is not None:
                scores(score[0], score[1], c, score[2])
            if pvs is not None:
                pv(pvs[0], pvs[1], c, pvs[2])

    @pl.when(i > 0)
    def _():
        step(score=(0, 0, None))
        step(score=(1, 1, None), soft=(0, 0, None))

        def pair(t):
            j = 2 * t
            step(score=(j + 2, 0, None), soft=(j + 1, 1, None), pvs=(j, 0, None))
            step(score=(j + 3, 1, None), soft=(j + 2, 0, None), pvs=(j + 1, 1, None))

        def body(t, carry):
            pair(2 * t)
            pair(2 * t + 1)
            return carry

        n_pairs = i - 1
        lax.fori_loop(0, n_pairs // 2, body, 0)

        @pl.when(n_pairs % 2 == 1)
        def _():
            pair(n_pairs - 1)

        j = 2 * i - 2
        step(score=(j + 2, 0, 0), soft=(j + 1, 1, None), pvs=(j, 0, None))
        step(score=(j + 3, 1, 1), soft=(j + 2, 0, 0), pvs=(j + 1, 1, None))

    @pl.when(i == 0)
    def _():
        step(score=(0, 0, 0))
        step(score=(1, 1, 1), soft=(0, 0, 0))

    step(soft=(2 * i + 1, 1, 1), pvs=(2 * i, 0, 0))
    step(pvs=(2 * i + 1, 1, 1))

    out = (acc_ref[:DV, :] * (1.0 / acc_ref[DV:DV + 1, :])).T
    g = gate_ref[...]
    o_ref[...] = (out * (g * _sigmoid(g))).astype(BF16)


def _attention(qt, k, vt, gate, tq, tk):
    L = k.shape[1]
    return pl.pallas_call(
        functools.partial(_attn_kernel, tq=tq, tk=tk),
        grid=(HEADS, L // tq),
        in_specs=[pl.BlockSpec((1, QK, tq), lambda h, i: (h, 0, i)),
                  pl.BlockSpec((1, L, QK), lambda h, i: (h, 0, 0)),
                  pl.BlockSpec((1, L // tk, V_ROWS, tk), lambda h, i: (h, 0, 0, 0)),
                  pl.BlockSpec((tq, DV), lambda h, i: (i, h))],
        out_specs=pl.BlockSpec((tq, DV), lambda h, i: (i, h)),
        out_shape=jax.ShapeDtypeStruct((L, D_MLA), BF16),
        scratch_shapes=[pltpu.VMEM((V_ROWS, tq), F32), pltpu.VMEM((1, tq), F32),
                        pltpu.VMEM((2, 1, tq), F32),
                        pltpu.VMEM((2, tq // MXU_COLS, tk, MXU_COLS), BF16),
                        pltpu.VMEM((2, tq // MXU_COLS, tk, MXU_COLS), BF16)],
        compiler_params=_params("parallel", "arbitrary"),
        name="attention",
    )(qt, k, vt, gate)


def _gelu_tanh(x):
    c = math.sqrt(2.0 / math.pi)
    return 0.5 * x * (1.0 + jnp.tanh(c * (x + 0.044715 * (x * x * x))))


S5_CHUNK = 256
PITCH = 68
N_JOBS = 32
JOBS_PER_SLAB = N_JOBS // N_SLABS
TILES_PER_JOB = 2 * STATE_TILES // N_JOBS


def _job_tiles(slab, kk):
    is_im = kk // (JOBS_PER_SLAB // 2)
    q = kk % (JOBS_PER_SLAB // 2)
    tile0 = is_im * STATE_TILES + SLAB_TILES * slab + TILES_PER_JOB * q
    return 2 * slab + q // 2, tile0


def _s5_kernel(un_ref, up_ref, g_ref, wbj_ref, are_ref, aim_ref, wcj_ref, d_ref, wglu_ref,
               o_ref, hs0_ref, hs1_ref, st_ref, acc_ref, ub_ref, *, tt):
    c = pl.program_id(0)
    hs_refs = (hs0_ref, hs1_ref)

    def col_tile(tile):
        return pl.ds(tile, tt, stride=PITCH)

    def load_u_slabs(u_ref):
        u = u_ref[...]
        for s in range(D_SSM // LANES):
            ub_ref[s] = u[:, LANES * s:LANES * (s + 1)].astype(BF16)

    def b_job(slab, kk, slot):
        s, tile0 = _job_tiles(slab, kk)
        r = jnp.dot(ub_ref[s], wbj_ref[slab * JOBS_PER_SLAB + kk], preferred_element_type=F32)
        for k in range(TILES_PER_JOB):
            hs_refs[slot][col_tile(tile0 + k), :] = r[:, LANES * k:LANES * (k + 1)]

    def c_job(slab, kk, slot):
        _, tile0 = _job_tiles(slab, kk)
        lhs = jnp.concatenate([hs_refs[slot][col_tile(tile0 + k), :] for k in range(TILES_PER_JOB)],
                              axis=-1).astype(BF16)
        return jnp.dot(lhs, wcj_ref[slab * JOBS_PER_SLAB + kk], preferred_element_type=F32)

    @pl.when(c == 0)
    def _():
        st_ref[...] = jnp.zeros(st_ref.shape, F32)
        hs1_ref[...] = jnp.zeros(hs1_ref.shape, F32)
        load_u_slabs(up_ref)
        for ti in range(N_JOBS):
            b_job(ti // JOBS_PER_SLAB, ti % JOBS_PER_SLAB, 0)

    load_u_slabs(un_ref)
    a_re = [are_ref[i] for i in range(N_SLABS)]
    a_im = [aim_ref[i] for i in range(N_SLABS)]
    steps_per_job = tt // N_JOBS

    def run(slot):
        hs_ref = hs_refs[slot]

        def scan_steps(t0, st):
            st = list(st)
            for dt in range(steps_per_job):
                base = (t0 + dt) * PITCH
                for i in range(N_SLABS):
                    re_rows = pl.ds(base + SLAB_TILES * i, SLAB_TILES)
                    im_rows = pl.ds(base + STATE_TILES + SLAB_TILES * i, SLAB_TILES)
                    h_re, h_im = st[i], st[N_SLABS + i]
                    n_re = a_re[i] * h_re - a_im[i] * h_im + hs_ref[re_rows, :]
                    n_im = a_re[i] * h_im + a_im[i] * h_re + hs_ref[im_rows, :]
                    hs_ref[re_rows, :] = n_re
                    hs_ref[im_rows, :] = n_im
                    st[i], st[N_SLABS + i] = n_re, n_im
            return st

        def body(slab, st):
            y = None
            for kk in range(JOBS_PER_SLAB):
                st = scan_steps((slab * JOBS_PER_SLAB + kk) * steps_per_job, st)
                part = c_job(slab, kk, 1 - slot)
                y = part if y is None else y + part
                b_job(slab, kk, 1 - slot)
            acc_ref[slab] = y
            return tuple(st)

        st = lax.fori_loop(0, N_SLABS, body, tuple(st_ref[i] for i in range(2 * N_SLABS)))
        for i in range(2 * N_SLABS):
            st_ref[i] = st[i]

    for slot in range(2):
        pl.when(c % 2 == slot)(functools.partial(run, slot))

    y = jnp.concatenate([acc_ref[i] for i in range(N_SLABS)], axis=-1) + d_ref[...] * up_ref[...]
    y = _gelu_tanh(y).astype(BF16)
    z = jnp.dot(y, wglu_ref[...], preferred_element_type=F32)
    out = z[:, :D_SSM] * _sigmoid(z[:, D_SSM:])
    g = g_ref[...]
    o_ref[...] = (out * (g * _sigmoid(g))).astype(BF16)


def _s5(u, gate, wbj, a_re, a_im, wcj, d_skip, w_glu, tt):
    L = u.shape[0]
    n = L // tt
    nxt = pl.BlockSpec((tt, D_SSM), lambda c: (jnp.minimum(c + 1, n - 1), 0))
    prv = pl.BlockSpec((tt, D_SSM), lambda c: (jnp.maximum(c - 1, 0), 0))
    job_cols = TILES_PER_JOB * LANES
    return pl.pallas_call(
        functools.partial(_s5_kernel, tt=tt),
        grid=(n + 1,),
        in_specs=[nxt, prv, prv,
                  _resident((N_JOBS, LANES, job_cols)),
                  _resident((N_SLABS, SUBLANES, LANES)), _resident((N_SLABS, SUBLANES, LANES)),
                  _resident((N_JOBS, job_cols, D_SSM // N_SLABS)),
                  _resident((1, D_SSM)), _resident((D_SSM, 2 * D_SSM))],
        out_specs=prv,
        out_shape=jax.ShapeDtypeStruct((L, D_SSM), BF16),
        scratch_shapes=[pltpu.VMEM((tt * PITCH, LANES), F32), pltpu.VMEM((tt * PITCH, LANES), F32),
                        pltpu.VMEM((2 * N_SLABS, SUBLANES, LANES), F32),
                        pltpu.VMEM((N_SLABS, tt, D_SSM // N_SLABS), F32),
                        pltpu.VMEM((D_SSM // LANES, tt, LANES), BF16)],
        compiler_params=_params("arbitrary"),
        name="s5",
    )(u, u, gate, wbj, a_re, a_im, wcj, d_skip, w_glu)


def _outproj_kernel(mla_ref, ssm_ref, w_ref, gpost_ref, x_ref, o_ref):
    out = jnp.dot(mla_ref[...], w_ref[:D_MLA, :], preferred_element_type=F32)
    out += jnp.dot(ssm_ref[...], w_ref[D_MLA:, :], preferred_element_type=F32)
    inv = lax.rsqrt(jnp.mean(out * out, axis=-1, keepdims=True) + NORM_EPS)
    o_ref[...] = x_ref[...] + out * inv * gpost_ref[...]


def _outproj(mla, ssm, w_out, g_post, x, tm):
    L = x.shape[0]
    row = lambda n: pl.BlockSpec((tm, n), lambda i: (i, 0))
    return pl.pallas_call(
        _outproj_kernel,
        grid=(L // tm,),
        in_specs=[row(D_MLA), row(D_SSM), _resident((D_MLA + D_SSM, D_MODEL)),
                  _resident((1, D_MODEL)), row(D_MODEL)],
        out_specs=row(D_MODEL),
        out_shape=jax.ShapeDtypeStruct((L, D_MODEL), F32),
        compiler_params=_params("parallel"),
        name="outproj",
    )(mla, ssm, w_out, g_post, x)


def _split_in_weights(w_in):
    s2 = Q_LORA + KV_LORA
    s3 = s2 + ROPE
    w_kr = jnp.pad(w_in[..., s2:s3].astype(BF16), ((0, 0), (0, 0), (0, LANES - ROPE)))
    return w_in[..., :s2].astype(BF16), w_kr, w_in[..., s3:].astype(BF16)


def _pack_b(bb_re, bb_im):
    depth = bb_re.shape[0]
    gpt = LANES // GROUP_CH
    eye = jnp.eye(gpt, dtype=F32)

    def pack(b):
        b = b.reshape(depth, GROUPS // gpt, gpt, NSTATE, GROUP_CH)
        return jnp.einsum('dsgph,gk->dsghkp', b, eye).reshape(depth, GROUPS // gpt, LANES, gpt * NSTATE)

    return jnp.concatenate([pack(bb_re), pack(bb_im)], axis=-1).astype(BF16)


def _pack_c(c):
    depth = c.shape[0]
    gps = SLAB_TILES * LANES // NSTATE
    eye = jnp.eye(gps, dtype=F32)
    c = c.reshape(depth, N_SLABS, gps, GROUP_CH, NSTATE)
    return jnp.einsum('dighp,gk->digpkh', c, eye).reshape(
        depth, N_SLABS, gps * NSTATE, gps * GROUP_CH).astype(BF16)


def kernel(x, positions, norm_pre, norm_post, w_in, q_norm, w_uq, kv_norm, w_ukv, ssm_a_re, ssm_a_im,
           ssm_b_re, ssm_b_im, ssm_c_re, ssm_c_im, ssm_d, ssm_log_step, w_glu, w_out):
    batch, L, _ = x.shape
    assert batch == 1
    depth = w_in.shape[0]
    tm = min(256, L)
    tm_out = min(512, L)
    tq = min(1024, L)
    tk = tq // 2
    tt = S5_CHUNK

    rope = _rope_tables(positions, tk)

    w_lat, w_kr, w_wide = _split_in_weights(w_in)
    wqt = w_uq.reshape(depth, Q_LORA, HEADS, QK).transpose(0, 2, 3, 1).astype(BF16)
    w_ukv4 = w_ukv.reshape(depth, KV_LORA, HEADS, NOPE + DV)
    wk = w_ukv4[..., :NOPE].reshape(depth, KV_LORA, HEADS * NOPE).astype(BF16)
    wvt = w_ukv4[..., NOPE:].transpose(0, 2, 3, 1).astype(BF16)
    w_glu_b = w_glu.astype(BF16)
    w_out_b = w_out.astype(BF16)

    ab_re, ab_im, bb_re, bb_im = _s5_prep(ssm_a_re, ssm_a_im, ssm_log_step, ssm_b_re, ssm_b_im)
    wb = _pack_b(bb_re, bb_im)
    half = wb.shape[-1] // 2
    job_cols = TILES_PER_JOB * LANES
    wbj = []
    for ti in range(N_JOBS):
        s, tile0 = _job_tiles(ti // JOBS_PER_SLAB, ti % JOBS_PER_SLAB)
        col = (tile0 // STATE_TILES) * half + (tile0 % STATE_TILES) * LANES - s * half
        wbj.append(wb[:, s, :, col:col + job_cols])
    wbj = jnp.stack(wbj, axis=1)
    slab_shape = (depth, N_SLABS, SUBLANES, LANES)
    a_re_s = ab_re.reshape(slab_shape)
    a_im_s = ab_im.reshape(slab_shape)
    wc_re = _pack_c(ssm_c_re).reshape(depth, N_SLABS, JOBS_PER_SLAB // 2, job_cols, -1)
    wc_im = _pack_c(-ssm_c_im).reshape(depth, N_SLABS, JOBS_PER_SLAB // 2, job_cols, -1)
    wcj = jnp.concatenate([wc_re, wc_im], axis=2).reshape(depth, N_JOBS, job_cols, -1)

    h = x[0]
    for i in range(depth):
        cq, ckv, kr, gate_mla, u, gate_ssm = _inproj(
            h, norm_pre[i][None], w_lat[i], w_kr[i], w_wide[i], q_norm[i][None], kv_norm[i][None], tm)
        qt, k, vt = _qkv(cq, ckv, kr, rope, wqt[i], wk[i], wvt[i], tk)
        mla = _attention(qt, k, vt, gate_mla, tq, tk)
        ssm = _s5(u, gate_ssm, wbj[i], a_re_s[i], a_im_s[i], wcj[i], ssm_d[i][None], w_glu_b[i], tt)
        h = _outproj(mla, ssm, w_out_b[i], norm_post[i][None], h, tm_out)
    return h[None]
```

```python
import functools
import math

import jax
import jax.numpy as jnp
from jax import lax
from jax.experimental import pallas as pl
from jax.experimental.pallas import tpu as pltpu

F32 = jnp.float32
BF16 = jnp.bfloat16

D_MODEL = 2048
HEADS = 8
NOPE = 128
ROPE = 64
QK = NOPE + ROPE
DV = 128
V_ROWS = DV + 16
Q_LORA = 512
KV_LORA = 256
D_MLA = HEADS * DV
D_SSM = 1024
GROUP_CH = 16
GROUPS = D_SSM // GROUP_CH
NSTATE = 64
ROPE_THETA = 10000.0
NORM_EPS = 1e-6
LANES = 128
SUBLANES = 8
MXU_COLS = 256
BF16_ROWS = 16
STATE_TILES = GROUPS * NSTATE // LANES
SLAB_TILES = SUBLANES
N_SLABS = STATE_TILES // SLAB_TILES
VMEM_LIMIT = 56 * 1024 * 1024


def _resident(shape):
    zeros = (0,) * len(shape)
    return pl.BlockSpec(shape, lambda *_: zeros, pipeline_mode=pl.Buffered(1))


def _sigmoid(x):
    return 1.0 / (1.0 + jnp.exp(-x))


def _params(*sem):
    return pltpu.CompilerParams(dimension_semantics=sem, vmem_limit_bytes=VMEM_LIMIT)


def _s5_prep_kernel(are_ref, aim_ref, lstep_ref, bre_ref, bim_ref,
                    abre_ref, abim_ref, bbre_ref, bbim_ref):
    a_re = are_ref[...]
    a_im = aim_ref[...]
    step = jnp.exp(lstep_ref[...])
    mag = jnp.exp(step * a_re)
    ab_re = mag * jnp.cos(step * a_im)
    ab_im = mag * jnp.sin(step * a_im)
    n_re = ab_re - 1.0
    den = a_re * a_re + a_im * a_im
    z_re = (n_re * a_re + ab_im * a_im) / den
    z_im = (ab_im * a_re - n_re * a_im) / den
    b_re = bre_ref[...]
    b_im = bim_ref[...]
    abre_ref[...] = ab_re
    abim_ref[...] = ab_im
    bbre_ref[...] = z_re * b_re - z_im * b_im
    bbim_ref[...] = z_re * b_im + z_im * b_re


def _s5_prep(a_re, a_im, log_step, b_re, b_im):
    depth = a_re.shape[0]
    rows = depth * GROUPS
    cols = NSTATE * GROUP_CH

    def expand(a):
        return jnp.broadcast_to(a[..., None], (depth, GROUPS, NSTATE, GROUP_CH)).reshape(rows, cols)

    step = jnp.broadcast_to(log_step[..., None], (depth, GROUPS, cols)).reshape(rows, cols)
    sds = jax.ShapeDtypeStruct((rows, cols), F32)
    ab_re, ab_im, bb_re, bb_im = pl.pallas_call(
        _s5_prep_kernel, out_shape=(sds, sds, sds, sds), name="s5_prep",
    )(expand(a_re), expand(a_im), step, b_re.reshape(rows, cols), b_im.reshape(rows, cols))
    shape4 = (depth, GROUPS, NSTATE, GROUP_CH)
    ab_re = ab_re.reshape(shape4)[..., 0]
    ab_im = ab_im.reshape(shape4)[..., 0]
    return ab_re, ab_im, bb_re.reshape(shape4), bb_im.reshape(shape4)


def _inproj_kernel(x_ref, gpre_ref, wlat_ref, wkr_ref, wwide_ref, gq_ref, gkv_ref,
                   cq_ref, ckv_ref, kr_ref, gm_ref, u_ref, gs_ref):
    x = x_ref[...]
    inv = lax.rsqrt(jnp.mean(x * x, axis=-1, keepdims=True) + NORM_EPS)
    h = (x * inv * gpre_ref[...]).astype(BF16)

    def proj(w_ref, lo, hi):
        return jnp.dot(h, w_ref[:, lo:hi], preferred_element_type=F32)

    def latent_norm(c, g_ref):
        inv_c = lax.rsqrt(jnp.mean(c * c, axis=-1, keepdims=True) + NORM_EPS)
        return (c * inv_c * g_ref[...]).astype(BF16)

    cq_ref[...] = latent_norm(proj(wlat_ref, 0, Q_LORA), gq_ref)
    ckv_ref[...] = latent_norm(proj(wlat_ref, Q_LORA, Q_LORA + KV_LORA), gkv_ref)
    kr_ref[...] = proj(wkr_ref, 0, LANES)
    gm_ref[...] = proj(wwide_ref, 0, D_MLA)
    u_ref[...] = proj(wwide_ref, D_MLA, D_MLA + D_SSM)
    gs_ref[...] = proj(wwide_ref, D_MLA + D_SSM, D_MLA + 2 * D_SSM)


def _inproj(x, g_pre, w_lat, w_kr, w_wide, g_q, g_kv, tm):
    L = x.shape[0]
    row = lambda n: pl.BlockSpec((tm, n), lambda i: (i, 0))
    out_shape = (
        jax.ShapeDtypeStruct((L, Q_LORA), BF16),
        jax.ShapeDtypeStruct((L, KV_LORA), BF16),
        jax.ShapeDtypeStruct((L, LANES), F32),
        jax.ShapeDtypeStruct((L, D_MLA), F32),
        jax.ShapeDtypeStruct((L, D_SSM), F32),
        jax.ShapeDtypeStruct((L, D_SSM), F32),
    )
    return pl.pallas_call(
        _inproj_kernel,
        grid=(L // tm,),
        in_specs=[row(D_MODEL), _resident((1, D_MODEL)), _resident((D_MODEL, Q_LORA + KV_LORA)),
                  _resident((D_MODEL, LANES)), _resident((D_MODEL, D_MLA + 2 * D_SSM)),
                  _resident((1, Q_LORA)), _resident((1, KV_LORA))],
        out_specs=(row(Q_LORA), row(KV_LORA), row(LANES), row(D_MLA), row(D_SSM), row(D_SSM)),
        out_shape=out_shape,
        compiler_params=_params("parallel"),
        name="inproj",
    )(x, g_pre, w_lat, w_kr, w_wide, g_q, g_kv)


def _rope_kernel(posc_ref, posr_ref, fcol_ref, frow_ref, cosr_ref, sinr_ref, cost_ref, sint_ref):
    ang = posc_ref[...].astype(F32) * frow_ref[...]
    cosr_ref[...] = jnp.cos(ang)
    sinr_ref[...] = jnp.sin(ang)
    ang_t = fcol_ref[...] * posr_ref[...].astype(F32)
    cost_ref[...] = jnp.cos(ang_t)
    sint_ref[...] = jnp.sin(ang_t)


def _rope_tables(positions, tm):
    L = positions.shape[-1]
    inv_freq = ROPE_THETA ** (-jnp.arange(0, ROPE, 2, dtype=F32) / ROPE)
    inv_freq = jnp.concatenate([inv_freq, inv_freq])
    row = pl.BlockSpec((tm, ROPE), lambda i: (i, 0))
    col = pl.BlockSpec((ROPE, tm), lambda i: (0, i))
    return pl.pallas_call(
        _rope_kernel,
        grid=(L // tm,),
        in_specs=[pl.BlockSpec((tm, 1), lambda i: (i, 0)), pl.BlockSpec((1, tm), lambda i: (0, i)),
                  _resident((ROPE, 1)), _resident((1, ROPE))],
        out_specs=(row, row, col, col),
        out_shape=(jax.ShapeDtypeStruct((L, ROPE), F32), jax.ShapeDtypeStruct((L, ROPE), F32),
                   jax.ShapeDtypeStruct((ROPE, L), F32), jax.ShapeDtypeStruct((ROPE, L), F32)),
        compiler_params=_params("parallel"),
        name="rope_tables",
    )(positions.reshape(L, 1), positions.reshape(1, L), inv_freq.reshape(ROPE, 1), inv_freq.reshape(1, ROPE))


def _qkv_kernel(cq_ref, ckv_ref, kr_ref, cosr_ref, sinr_ref, cost_ref, sint_ref,
                wqt_ref, wk_ref, wvt_ref, qt_ref, k_ref, vt_ref, *, q_scale):
    nt = (((1,), (1,)), ((), ()))
    cq = cq_ref[...]
    ckv = ckv_ref[...]
    half = ROPE // 2
    cos_r = cosr_ref[...]
    sin_r = sinr_ref[...]
    cos_t = cost_ref[...]
    sin_t = sint_ref[...]

    kr = kr_ref[:, :ROPE]
    kr_rot = jnp.concatenate([-kr[:, half:], kr[:, :half]], axis=-1)
    k_rope = (kr * cos_r + kr_rot * sin_r).astype(BF16)
    k_nope = jnp.dot(ckv, wk_ref[...], preferred_element_type=F32)

    pad_row = lax.broadcasted_iota(jnp.int32, (V_ROWS - DV, vt_ref.shape[-1]), 0)
    ones_rows = jnp.where(pad_row == 0, 1.0, 0.0).astype(BF16)

    for h in range(HEADS):
        qt = lax.dot_general(wqt_ref[h], cq, nt, preferred_element_type=F32)
        qr = qt[NOPE:]
        qr_rot = jnp.concatenate([-qr[half:], qr[:half]], axis=0)
        qr = qr * cos_t + qr_rot * sin_t
        qt_ref[h, :NOPE, :] = (qt[:NOPE] * q_scale).astype(BF16)
        qt_ref[h, NOPE:, :] = (qr * q_scale).astype(BF16)
        k_ref[h, :, :NOPE] = k_nope[:, h * NOPE:(h + 1) * NOPE].astype(BF16)
        k_ref[h, :, NOPE:] = k_rope
        vt = lax.dot_general(wvt_ref[h], ckv, nt, preferred_element_type=F32)
        vt_ref[h, 0, :DV, :] = vt.astype(BF16)
        vt_ref[h, 0, DV:, :] = ones_rows


def _qkv(cq, ckv, kr, rope, wqt, wk, wvt, tk):
    L = cq.shape[0]
    q_scale = math.log2(math.e) / math.sqrt(QK)
    row = lambda n: pl.BlockSpec((tk, n), lambda i: (i, 0))
    return pl.pallas_call(
        functools.partial(_qkv_kernel, q_scale=q_scale),
        grid=(L // tk,),
        in_specs=[row(Q_LORA), row(KV_LORA), row(LANES), row(ROPE), row(ROPE),
                  pl.BlockSpec((ROPE, tk), lambda i: (0, i)), pl.BlockSpec((ROPE, tk), lambda i: (0, i)),
                  _resident((HEADS, QK, Q_LORA)), _resident((KV_LORA, HEADS * NOPE)),
                  _resident((HEADS, DV, KV_LORA))],
        out_specs=(pl.BlockSpec((HEADS, QK, tk), lambda i: (0, 0, i)),
                   pl.BlockSpec((HEADS, tk, QK), lambda i: (0, i, 0)),
                   pl.BlockSpec((HEADS, 1, V_ROWS, tk), lambda i: (0, i, 0, 0))),
        out_shape=(jax.ShapeDtypeStruct((HEADS, QK, L), BF16),
                   jax.ShapeDtypeStruct((HEADS, L, QK), BF16),
                   jax.ShapeDtypeStruct((HEADS, L // tk, V_ROWS, tk), BF16)),
        compiler_params=_params("parallel"),
        name="qkv",
    )(cq, ckv, kr, *rope, wqt, wk, wvt)


def _attn_kernel(qt_ref, k_ref, vt_ref, gate_ref, o_ref, acc_ref, m_ref, alpha_ref, s_ref, p_ref, *, tq, tk):
    i = pl.program_id(1)
    m_ref[...] = jnp.full(m_ref.shape, -1e30, F32)
    acc_ref[...] = jnp.zeros(acc_ref.shape, F32)
    assert tq == 2 * tk
    n_chains = tq // MXU_COLS

    def cols(c):
        return slice(MXU_COLS * c, MXU_COLS * (c + 1))

    def visible(c, diag):
        return diag is None or diag * tk < MXU_COLS * (c + 1)

    def scores(j, slot, c, diag):
        if not visible(c, diag):
            return
        kc = k_ref[0, pl.ds(pl.multiple_of(j * tk, tk), tk), :]
        s = jnp.dot(kc, qt_ref[0, :, cols(c)], preferred_element_type=F32)
        if diag is not None and diag * tk + tk - 1 > MXU_COLS * c:
            key = diag * tk + lax.broadcasted_iota(jnp.int32, s.shape, 0)
            qry = MXU_COLS * c + lax.broadcasted_iota(jnp.int32, s.shape, 1)
            s = jnp.where(key <= qry, s, -1e30)
        s_ref[slot, c] = s.astype(BF16)

    def softmax(slot, c, diag):
        if not visible(c, diag):
            return
        s = s_ref[slot, c]
        m_blk = jnp.max(s.reshape(tk // BF16_ROWS, BF16_ROWS, MXU_COLS), axis=0)
        m_blk = jnp.max(m_blk.astype(F32), axis=0, keepdims=True)
        m_prev = m_ref[:, cols(c)]
        m_new = jnp.maximum(m_prev, m_blk)
        alpha_ref[slot, :, cols(c)] = jnp.exp2(m_prev - m_new)
        p_ref[slot, c] = jnp.exp2(s - m_new.astype(BF16))
        m_ref[:, cols(c)] = m_new

    def pv(j, slot, c, diag):
        if not visible(c, diag):
            return
        vc = vt_ref[0, j]
        upd = jnp.dot(vc, p_ref[slot, c], preferred_element_type=F32)
        acc_ref[:, cols(c)] = alpha_ref[slot, :, cols(c)] * acc_ref[:, cols(c)] + upd

    def step(score=None, soft=None, pvs=None):
        for c in range(n_chains):
            if soft is not None:
                softmax(soft[1], c, soft[2])
            if score is not None:
                scores(score[0], score[1], c, score[2])
            if pvs is not None:
                pv(pvs[0], pvs[1], c, pvs[2])

    @pl.when(i > 0)
    def _():
        step(score=(0, 0, None))
        step(score=(1, 1, None), soft=(0, 0, None))

        def pair(t):
            j = 2 * t
            step(score=(j + 2, 0, None), soft=(j + 1, 1, None), pvs=(j, 0, None))
            step(score=(j + 3, 1, None), soft=(j + 2, 0, None), pvs=(j + 1, 1, None))

        def body(t, carry):
            pair(2 * t)
            pair(2 * t + 1)
            return carry

        n_pairs = i - 1
        lax.fori_loop(0, n_pairs // 2, body, 0)

        @pl.when(n_pairs % 2 == 1)
        def _():
            pair(n_pairs - 1)

        j = 2 * i - 2
        step(score=(j + 2, 0, 0), soft=(j + 1, 1, None), pvs=(j, 0, None))
        step(score=(j + 3, 1, 1), soft=(j + 2, 0, 0), pvs=(j + 1, 1, None))

    @pl.when(i == 0)
    def _():
        step(score=(0, 0, 0))
        step(score=(1, 1, 1), soft=(0, 0, 0))

    step(soft=(2 * i + 1, 1, 1), pvs=(2 * i, 0, 0))
    step(pvs=(2 * i + 1, 1, 1))

    out = (acc_ref[:DV, :] * (1.0 / acc_ref[DV:DV + 1, :])).T
    g = gate_ref[...]
    o_ref[...] = (out * (g * _sigmoid(g))).astype(BF16)


def _attention(qt, k, vt, gate, tq, tk):
    L = k.shape[1]
    return pl.pallas_call(
        functools.partial(_attn_kernel, tq=tq, tk=tk),
        grid=(HEADS, L // tq),
        in_specs=[pl.BlockSpec((1, QK, tq), lambda h, i: (h, 0, i)),
                  pl.BlockSpec((1, L, QK), lambda h, i: (h, 0, 0)),
                  pl.BlockSpec((1, L // tk, V_ROWS, tk), lambda h, i: (h, 0, 0, 0)),
                  pl.BlockSpec((tq, DV), lambda h, i: (i, h))],
        out_specs=pl.BlockSpec((tq, DV), lambda h, i: (i, h)),
        out_shape=jax.ShapeDtypeStruct((L, D_MLA), BF16),
        scratch_shapes=[pltpu.VMEM((V_ROWS, tq), F32), pltpu.VMEM((1, tq), F32),
                        pltpu.VMEM((2, 1, tq), F32),
                        pltpu.VMEM((2, tq // MXU_COLS, tk, MXU_COLS), BF16),
                        pltpu.VMEM((2, tq // MXU_COLS, tk, MXU_COLS), BF16)],
        compiler_params=_params("parallel", "arbitrary"),
        name="attention",
    )(qt, k, vt, gate)


def _gelu_tanh(x):
    c = math.sqrt(2.0 / math.pi)
    return 0.5 * x * (1.0 + jnp.tanh(c * (x + 0.044715 * (x * x * x))))


S5_CHUNK = 256
PITCH = 68
N_JOBS = 32
JOBS_PER_SLAB = N_JOBS // N_SLABS
TILES_PER_JOB = 2 * STATE_TILES // N_JOBS


def _job_tiles(slab, kk):
    is_im = kk // (JOBS_PER_SLAB // 2)
    q = kk % (JOBS_PER_SLAB // 2)
    tile0 = is_im * STATE_TILES + SLAB_TILES * slab + TILES_PER_JOB * q
    return 2 * slab + q // 2, tile0


def _s5_kernel(un_ref, up_ref, g_ref, wbj_ref, are_ref, aim_ref, wcj_ref, d_ref, wglu_ref,
               o_ref, hs0_ref, hs1_ref, st_ref, acc_ref, ub_ref, *, tt):
    c = pl.program_id(0)
    hs_refs = (hs0_ref, hs1_ref)

    def col_tile(tile):
        return pl.ds(tile, tt, stride=PITCH)

    def load_u_slabs(u_ref):
        u = u_ref[...]
        for s in range(D_SSM // LANES):
            ub_ref[s] = u[:, LANES * s:LANES * (s + 1)].astype(BF16)

    def b_job(slab, kk, slot):
        s, tile0 = _job_tiles(slab, kk)
        r = jnp.dot(ub_ref[s], wbj_ref[slab * JOBS_PER_SLAB + kk], preferred_element_type=F32)
        for k in range(TILES_PER_JOB):
            hs_refs[slot][col_tile(tile0 + k), :] = r[:, LANES * k:LANES * (k + 1)]

    def c_job(slab, kk, slot):
        _, tile0 = _job_tiles(slab, kk)
        lhs = jnp.concatenate([hs_refs[slot][col_tile(tile0 + k), :] for k in range(TILES_PER_JOB)],
                              axis=-1).astype(BF16)
        return jnp.dot(lhs, wcj_ref[slab * JOBS_PER_SLAB + kk], preferred_element_type=F32)

    @pl.when(c == 0)
    def _():
        st_ref[...] = jnp.zeros(st_ref.shape, F32)
        hs1_ref[...] = jnp.zeros(hs1_ref.shape, F32)
        load_u_slabs(up_ref)
        for ti in range(N_JOBS):
            b_job(ti // JOBS_PER_SLAB, ti % JOBS_PER_SLAB, 0)

    load_u_slabs(un_ref)
    a_re = [are_ref[i] for i in range(N_SLABS)]
    a_im = [aim_ref[i] for i in range(N_SLABS)]
    steps_per_job = tt // N_JOBS

    def run(slot):
        hs_ref = hs_refs[slot]

        def scan_steps(t0, st):
            st = list(st)
            for dt in range(steps_per_job):
                base = (t0 + dt) * PITCH
                for i in range(N_SLABS):
                    re_rows = pl.ds(base + SLAB_TILES * i, SLAB_TILES)
                    im_rows = pl.ds(base + STATE_TILES + SLAB_TILES * i, SLAB_TILES)
                    h_re, h_im = st[i], st[N_SLABS + i]
                    n_re = a_re[i] * h_re - a_im[i] * h_im + hs_ref[re_rows, :]
                    n_im = a_re[i] * h_im + a_im[i] * h_re + hs_ref[im_rows, :]
                    hs_ref[re_rows, :] = n_re
                    hs_ref[im_rows, :] = n_im
                    st[i], st[N_SLABS + i] = n_re, n_im
            return st

        st = [st_ref[i] for i in range(2 * N_SLABS)]
        for slab in range(N_SLABS):
            y = None
            for kk in range(JOBS_PER_SLAB):
                st = scan_steps((slab * JOBS_PER_SLAB + kk) * steps_per_job, st)
                part = c_job(slab, kk, 1 - slot)
                y = part if y is None else y + part
                b_job(slab, kk, 1 - slot)
            acc_ref[slab] = y
        for i in range(2 * N_SLABS):
            st_ref[i] = st[i]

    for slot in range(2):
        pl.when(c % 2 == slot)(functools.partial(run, slot))

    y = jnp.concatenate([acc_ref[i] for i in range(N_SLABS)], axis=-1) + d_ref[...] * up_ref[...]
    y = _gelu_tanh(y).astype(BF16)
    z = jnp.dot(y, wglu_ref[...], preferred_element_type=F32)
    out = z[:, :D_SSM] * _sigmoid(z[:, D_SSM:])
    g = g_ref[...]
    o_ref[...] = (out * (g * _sigmoid(g))).astype(BF16)


def _s5(u, gate, wbj, a_re, a_im, wcj, d_skip, w_glu, tt):
    L = u.shape[0]
    n = L // tt
    nxt = pl.BlockSpec((tt, D_SSM), lambda c: (jnp.minimum(c + 1, n - 1), 0))
    prv = pl.BlockSpec((tt, D_SSM), lambda c: (jnp.maximum(c - 1, 0), 0))
    job_cols = TILES_PER_JOB * LANES
    return pl.pallas_call(
        functools.partial(_s5_kernel, tt=tt),
        grid=(n + 1,),
        in_specs=[nxt, prv, prv,
                  _resident((N_JOBS, LANES, job_cols)),
                  _resident((N_SLABS, SUBLANES, LANES)), _resident((N_SLABS, SUBLANES, LANES)),
                  _resident((N_JOBS, job_cols, D_SSM // N_SLABS)),
                  _resident((1, D_SSM)), _resident((D_SSM, 2 * D_SSM))],
        out_specs=prv,
        out_shape=jax.ShapeDtypeStruct((L, D_SSM), BF16),
        scratch_shapes=[pltpu.VMEM((tt * PITCH, LANES), F32), pltpu.VMEM((tt * PITCH, LANES), F32),
                        pltpu.VMEM((2 * N_SLABS, SUBLANES, LANES), F32),
                        pltpu.VMEM((N_SLABS, tt, D_SSM // N_SLABS), F32),
                        pltpu.VMEM((D_SSM // LANES, tt, LANES), BF16)],
        compiler_params=_params("arbitrary"),
        name="s5",
    )(u, u, gate, wbj, a_re, a_im, wcj, d_skip, w_glu)


def _outproj_kernel(mla_ref, ssm_ref, w_ref, gpost_ref, x_ref, o_ref):
    out = jnp.dot(mla_ref[...], w_ref[:D_MLA, :], preferred_element_type=F32)
    out += jnp.dot(ssm_ref[...], w_ref[D_MLA:, :], preferred_element_type=F32)
    inv = lax.rsqrt(jnp.mean(out * out, axis=-1, keepdims=True) + NORM_EPS)
    o_ref[...] = x_ref[...] + out * inv * gpost_ref[...]


def _outproj(mla, ssm, w_out, g_post, x, tm):
    L = x.shape[0]
    row = lambda n: pl.BlockSpec((tm, n), lambda i: (i, 0))
    return pl.pallas_call(
        _outproj_kernel,
        grid=(L // tm,),
        in_specs=[row(D_MLA), row(D_SSM), _resident((D_MLA + D_SSM, D_MODEL)),
                  _resident((1, D_MODEL)), row(D_MODEL)],
        out_specs=row(D_MODEL),
        out_shape=jax.ShapeDtypeStruct((L, D_MODEL), F32),
        compiler_params=_params("parallel"),
        name="outproj",
    )(mla, ssm, w_out, g_post, x)


def _split_in_weights(w_in):
    s2 = Q_LORA + KV_LORA
    s3 = s2 + ROPE
    w = w_in.astype(BF16)
    w_kr = jnp.pad(w[..., s2:s3], ((0, 0), (0, 0), (0, LANES - ROPE)))
    return w[..., :s2], w_kr, w[..., s3:]


def _pack_b(bb_re, bb_im):
    depth = bb_re.shape[0]
    gpt = LANES // GROUP_CH
    eye = jnp.eye(gpt, dtype=F32)

    def pack(b):
        b = b.reshape(depth, GROUPS // gpt, gpt, NSTATE, GROUP_CH)
        return jnp.einsum('dsgph,gk->dsghkp', b, eye).reshape(depth, GROUPS // gpt, LANES, gpt * NSTATE)

    return jnp.concatenate([pack(bb_re), pack(bb_im)], axis=-1).astype(BF16)


def _pack_c(c):
    depth = c.shape[0]
    gps = SLAB_TILES * LANES // NSTATE
    eye = jnp.eye(gps, dtype=F32)
    c = c.reshape(depth, N_SLABS, gps, GROUP_CH, NSTATE)
    return jnp.einsum('dighp,gk->digpkh', c, eye).reshape(
        depth, N_SLABS, gps * NSTATE, gps * GROUP_CH).astype(BF16)


def kernel(x, positions, norm_pre, norm_post, w_in, q_norm, w_uq, kv_norm, w_ukv, ssm_a_re, ssm_a_im,
           ssm_b_re, ssm_b_im, ssm_c_re, ssm_c_im, ssm_d, ssm_log_step, w_glu, w_out):
    batch, L, _ = x.shape
    assert batch == 1
    depth = w_in.shape[0]
    tm = min(256, L)
    tm_out = min(512, L)
    tq = min(1024, L)
    tk = tq // 2
    tt = S5_CHUNK

    rope = _rope_tables(positions, tk)

    w_lat, w_kr, w_wide = _split_in_weights(w_in)
    wqt = w_uq.astype(BF16).reshape(depth, Q_LORA, HEADS, QK).transpose(0, 2, 3, 1)
    w_ukv4 = w_ukv.astype(BF16).reshape(depth, KV_LORA, HEADS, NOPE + DV)
    wk = w_ukv4[..., :NOPE].reshape(depth, KV_LORA, HEADS * NOPE)
    wvt = w_ukv4[..., NOPE:].transpose(0, 2, 3, 1)
    w_glu_b = w_glu.astype(BF16)
    w_out_b = w_out.astype(BF16)

    ab_re, ab_im, bb_re, bb_im = _s5_prep(ssm_a_re, ssm_a_im, ssm_log_step, ssm_b_re, ssm_b_im)
    wb = _pack_b(bb_re, bb_im)
    half = wb.shape[-1] // 2
    job_cols = TILES_PER_JOB * LANES
    wbj = []
    for ti in range(N_JOBS):
        s, tile0 = _job_tiles(ti // JOBS_PER_SLAB, ti % JOBS_PER_SLAB)
        col = (tile0 // STATE_TILES) * half + (tile0 % STATE_TILES) * LANES - s * half
        wbj.append(wb[:, s, :, col:col + job_cols])
    wbj = jnp.stack(wbj, axis=1)
    slab_shape = (depth, N_SLABS, SUBLANES, LANES)
    a_re_s = ab_re.reshape(slab_shape)
    a_im_s = ab_im.reshape(slab_shape)
    wc_re = _pack_c(ssm_c_re).reshape(depth, N_SLABS, JOBS_PER_SLAB // 2, job_cols, -1)
    wc_im = _pack_c(-ssm_c_im).reshape(depth, N_SLABS, JOBS_PER_SLAB // 2, job_cols, -1)
    wcj = jnp.concatenate([wc_re, wc_im], axis=2).reshape(depth, N_JOBS, job_cols, -1)

    h = x[0]
    for i in range(depth):
        cq, ckv, kr, gate_mla, u, gate_ssm = _inproj(
            h, norm_pre[i][None], w_lat[i], w_kr[i], w_wide[i], q_norm[i][None], kv_norm[i][None], tm)
        qt, k, vt = _qkv(cq, ckv, kr, rope, wqt[i], wk[i], wvt[i], tk)
        mla = _attention(qt, k, vt, gate_mla, tq, tk)
        ssm = _s5(u, gate_ssm, wbj[i], a_re_s[i], a_im_s[i], wcj[i], ssm_d[i][None], w_glu_b[i], tt)
        h = _outproj(mla, ssm, w_out_b[i], norm_post[i][None], h, tm_out)
    return h[None]
```

```python
import functools
import math

import jax
import jax.numpy as jnp
from jax import lax
from jax.experimental import pallas as pl
from jax.experimental.pallas import tpu as pltpu

F32 = jnp.float32
BF16 = jnp.bfloat16

D_MODEL = 2048
HEADS = 8
NOPE = 128
ROPE = 64
QK = NOPE + ROPE
DV = 128
V_ROWS = DV + 16
Q_LORA = 512
KV_LORA = 256
D_MLA = HEADS * DV
D_SSM = 1024
GROUP_CH = 16
GROUPS = D_SSM // GROUP_CH
NSTATE = 64
ROPE_THETA = 10000.0
NORM_EPS = 1e-6
LANES = 128
SUBLANES = 8
MXU_COLS = 256
BF16_ROWS = 16
STATE_TILES = GROUPS * NSTATE // LANES
SLAB_TILES = SUBLANES
N_SLABS = STATE_TILES // SLAB_TILES
VMEM_LIMIT = 56 * 1024 * 1024


def _resident(shape):
    zeros = (0,) * len(shape)
    return pl.BlockSpec(shape, lambda *_: zeros, pipeline_mode=pl.Buffered(1))


def _sigmoid(x):
    return 1.0 / (1.0 + jnp.exp(-x))


def _params(*sem):
    return pltpu.CompilerParams(dimension_semantics=sem, vmem_limit_bytes=VMEM_LIMIT)


def _s5_prep_kernel(are_ref, aim_ref, lstep_ref, bre_ref, bim_ref,
                    abre_ref, abim_ref, bbre_ref, bbim_ref):
    a_re = are_ref[...]
    a_im = aim_ref[...]
    step = jnp.exp(lstep_ref[...])
    mag = jnp.exp(step * a_re)
    ab_re = mag * jnp.cos(step * a_im)
    ab_im = mag * jnp.sin(step * a_im)
    n_re = ab_re - 1.0
    den = a_re * a_re + a_im * a_im
    z_re = (n_re * a_re + ab_im * a_im) / den
    z_im = (ab_im * a_re - n_re * a_im) / den
    b_re = bre_ref[...]
    b_im = bim_ref[...]
    abre_ref[...] = ab_re
    abim_ref[...] = ab_im
    bbre_ref[...] = z_re * b_re - z_im * b_im
    bbim_ref[...] = z_re * b_im + z_im * b_re


def _s5_prep(a_re, a_im, log_step, b_re, b_im):
    depth = a_re.shape[0]
    rows = depth * GROUPS
    cols = NSTATE * GROUP_CH

    def expand(a):
        return jnp.broadcast_to(a[..., None], (depth, GROUPS, NSTATE, GROUP_CH)).reshape(rows, cols)

    step = jnp.broadcast_to(log_step[..., None], (depth, GROUPS, cols)).reshape(rows, cols)
    sds = jax.ShapeDtypeStruct((rows, cols), F32)
    ab_re, ab_im, bb_re, bb_im = pl.pallas_call(
        _s5_prep_kernel, out_shape=(sds, sds, sds, sds), name="s5_prep",
    )(expand(a_re), expand(a_im), step, b_re.reshape(rows, cols), b_im.reshape(rows, cols))
    shape4 = (depth, GROUPS, NSTATE, GROUP_CH)
    ab_re = ab_re.reshape(shape4)[..., 0]
    ab_im = ab_im.reshape(shape4)[..., 0]
    return ab_re, ab_im, bb_re.reshape(shape4), bb_im.reshape(shape4)


def _inproj_kernel(x_ref, gpre_ref, wlat_ref, wkr_ref, wwide_ref, gq_ref, gkv_ref,
                   cq_ref, ckv_ref, kr_ref, gm_ref, u_ref, gs_ref):
    x = x_ref[...]
    inv = lax.rsqrt(jnp.mean(x * x, axis=-1, keepdims=True) + NORM_EPS)
    h = (x * inv * gpre_ref[...]).astype(BF16)

    def proj(w_ref, lo, hi):
        return jnp.dot(h, w_ref[:, lo:hi], preferred_element_type=F32)

    def latent_norm(c, g_ref):
        inv_c = lax.rsqrt(jnp.mean(c * c, axis=-1, keepdims=True) + NORM_EPS)
        return (c * inv_c * g_ref[...]).astype(BF16)

    cq_ref[...] = latent_norm(proj(wlat_ref, 0, Q_LORA), gq_ref)
    ckv_ref[...] = latent_norm(proj(wlat_ref, Q_LORA, Q_LORA + KV_LORA), gkv_ref)
    kr_ref[...] = proj(wkr_ref, 0, LANES)
    gm_ref[...] = proj(wwide_ref, 0, D_MLA)
    u_ref[...] = proj(wwide_ref, D_MLA, D_MLA + D_SSM)
    gs_ref[...] = proj(wwide_ref, D_MLA + D_SSM, D_MLA + 2 * D_SSM)


def _inproj(x, g_pre, w_lat, w_kr, w_wide, g_q, g_kv, tm):
    L = x.shape[0]
    row = lambda n: pl.BlockSpec((tm, n), lambda i: (i, 0))
    out_shape = (
        jax.ShapeDtypeStruct((L, Q_LORA), BF16),
        jax.ShapeDtypeStruct((L, KV_LORA), BF16),
        jax.ShapeDtypeStruct((L, LANES), F32),
        jax.ShapeDtypeStruct((L, D_MLA), F32),
        jax.ShapeDtypeStruct((L, D_SSM), F32),
        jax.ShapeDtypeStruct((L, D_SSM), F32),
    )
    return pl.pallas_call(
        _inproj_kernel,
        grid=(L // tm,),
        in_specs=[row(D_MODEL), _resident((1, D_MODEL)), _resident((D_MODEL, Q_LORA + KV_LORA)),
                  _resident((D_MODEL, LANES)), _resident((D_MODEL, D_MLA + 2 * D_SSM)),
                  _resident((1, Q_LORA)), _resident((1, KV_LORA))],
        out_specs=(row(Q_LORA), row(KV_LORA), row(LANES), row(D_MLA), row(D_SSM), row(D_SSM)),
        out_shape=out_shape,
        compiler_params=_params("parallel"),
        name="inproj",
    )(x, g_pre, w_lat, w_kr, w_wide, g_q, g_kv)


def _rope_kernel(posc_ref, posr_ref, fcol_ref, frow_ref, cosr_ref, sinr_ref, cost_ref, sint_ref):
    ang = posc_ref[...].astype(F32) * frow_ref[...]
    cosr_ref[...] = jnp.cos(ang)
    sinr_ref[...] = jnp.sin(ang)
    ang_t = fcol_ref[...] * posr_ref[...].astype(F32)
    cost_ref[...] = jnp.cos(ang_t)
    sint_ref[...] = jnp.sin(ang_t)


def _rope_tables(positions, tm):
    L = positions.shape[-1]
    inv_freq = ROPE_THETA ** (-jnp.arange(0, ROPE, 2, dtype=F32) / ROPE)
    inv_freq = jnp.concatenate([inv_freq, inv_freq])
    row = pl.BlockSpec((tm, ROPE), lambda i: (i, 0))
    col = pl.BlockSpec((ROPE, tm), lambda i: (0, i))
    return pl.pallas_call(
        _rope_kernel,
        grid=(L // tm,),
        in_specs=[pl.BlockSpec((tm, 1), lambda i: (i, 0)), pl.BlockSpec((1, tm), lambda i: (0, i)),
                  _resident((ROPE, 1)), _resident((1, ROPE))],
        out_specs=(row, row, col, col),
        out_shape=(jax.ShapeDtypeStruct((L, ROPE), F32), jax.ShapeDtypeStruct((L, ROPE), F32),
                   jax.ShapeDtypeStruct((ROPE, L), F32), jax.ShapeDtypeStruct((ROPE, L), F32)),
        compiler_params=_params("parallel"),
        name="rope_tables",
    )(positions.reshape(L, 1), positions.reshape(1, L), inv_freq.reshape(ROPE, 1), inv_freq.reshape(1, ROPE))


def _qkv_kernel(cq_ref, ckv_ref, kr_ref, cosr_ref, sinr_ref, cost_ref, sint_ref,
                wqt_ref, wk_ref, wvt_ref, qt_ref, k_ref, vt_ref, *, q_scale):
    nt = (((1,), (1,)), ((), ()))
    cq = cq_ref[...]
    ckv = ckv_ref[...]
    half = ROPE // 2
    cos_r = cosr_ref[...]
    sin_r = sinr_ref[...]
    cos_t = cost_ref[...]
    sin_t = sint_ref[...]

    kr = kr_ref[:, :ROPE]
    kr_rot = jnp.concatenate([-kr[:, half:], kr[:, :half]], axis=-1)
    k_rope = (kr * cos_r + kr_rot * sin_r).astype(BF16)
    k_nope = jnp.dot(ckv, wk_ref[...], preferred_element_type=F32)

    pad_row = lax.broadcasted_iota(jnp.int32, (V_ROWS - DV, vt_ref.shape[-1]), 0)
    ones_rows = jnp.where(pad_row == 0, 1.0, 0.0).astype(BF16)

    for h in range(HEADS):
        qt = lax.dot_general(wqt_ref[h], cq, nt, preferred_element_type=F32)
        qr = qt[NOPE:]
        qr_rot = jnp.concatenate([-qr[half:], qr[:half]], axis=0)
        qr = qr * cos_t + qr_rot * sin_t
        qt_ref[h, :NOPE, :] = (qt[:NOPE] * q_scale).astype(BF16)
        qt_ref[h, NOPE:, :] = (qr * q_scale).astype(BF16)
        k_ref[h, :, :NOPE] = k_nope[:, h * NOPE:(h + 1) * NOPE].astype(BF16)
        k_ref[h, :, NOPE:] = k_rope
        vt = lax.dot_general(wvt_ref[h], ckv, nt, preferred_element_type=F32)
        vt_ref[h, 0, :DV, :] = vt.astype(BF16)
        vt_ref[h, 0, DV:, :] = ones_rows


def _qkv(cq, ckv, kr, rope, wqt, wk, wvt, tk):
    L = cq.shape[0]
    q_scale = math.log2(math.e) / math.sqrt(QK)
    row = lambda n: pl.BlockSpec((tk, n), lambda i: (i, 0))
    return pl.pallas_call(
        functools.partial(_qkv_kernel, q_scale=q_scale),
        grid=(L // tk,),
        in_specs=[row(Q_LORA), row(KV_LORA), row(LANES), row(ROPE), row(ROPE),
                  pl.BlockSpec((ROPE, tk), lambda i: (0, i)), pl.BlockSpec((ROPE, tk), lambda i: (0, i)),
                  _resident((HEADS, QK, Q_LORA)), _resident((KV_LORA, HEADS * NOPE)),
                  _resident((HEADS, DV, KV_LORA))],
        out_specs=(pl.BlockSpec((HEADS, QK, tk), lambda i: (0, 0, i)),
                   pl.BlockSpec((HEADS, tk, QK), lambda i: (0, i, 0)),
                   pl.BlockSpec((HEADS, 1, V_ROWS, tk), lambda i: (0, i, 0, 0))),
        out_shape=(jax.ShapeDtypeStruct((HEADS, QK, L), BF16),
                   jax.ShapeDtypeStruct((HEADS, L, QK), BF16),
                   jax.ShapeDtypeStruct((HEADS, L // tk, V_ROWS, tk), BF16)),
        compiler_params=_params("parallel"),
        name="qkv",
    )(cq, ckv, kr, *rope, wqt, wk, wvt)


def _attn_kernel(qt_ref, k_ref, vt_ref, gate_ref, o_ref, acc_ref, m_ref, alpha_ref, s_ref, p_ref, *, tq, tk):
    i = pl.program_id(1)
    m_ref[...] = jnp.full(m_ref.shape, -1e30, F32)
    acc_ref[...] = jnp.zeros(acc_ref.shape, F32)
    assert tq == 2 * tk
    n_chains = tq // MXU_COLS

    def cols(c):
        return slice(MXU_COLS * c, MXU_COLS * (c + 1))

    def visible(c, diag):
        return diag is None or diag * tk < MXU_COLS * (c + 1)

    def scores(j, slot, c, diag):
        if not visible(c, diag):
            return
        kc = k_ref[0, pl.ds(pl.multiple_of(j * tk, tk), tk), :]
        s = jnp.dot(kc, qt_ref[0, :, cols(c)], preferred_element_type=F32)
        if diag is not None and diag * tk + tk - 1 > MXU_COLS * c:
            key = diag * tk + lax.broadcasted_iota(jnp.int32, s.shape, 0)
            qry = MXU_COLS * c + lax.broadcasted_iota(jnp.int32, s.shape, 1)
            s = jnp.where(key <= qry, s, -1e30)
        s_ref[slot, c] = s.astype(BF16)

    def softmax(slot, c, diag):
        if not visible(c, diag):
            return
        s = s_ref[slot, c]
        m_blk = jnp.max(s.reshape(tk // BF16_ROWS, BF16_ROWS, MXU_COLS), axis=0)
        m_blk = jnp.max(m_blk.astype(F32), axis=0, keepdims=True)
        m_prev = m_ref[:, cols(c)]
        m_new = jnp.maximum(m_prev, m_blk)
        alpha_ref[slot, :, cols(c)] = jnp.exp2(m_prev - m_new)
        p_ref[slot, c] = jnp.exp2(s - m_new.astype(BF16))
        m_ref[:, cols(c)] = m_new

    def pv(j, slot, c, diag):
        if not visible(c, diag):
            return
        vc = vt_ref[0, j]
        upd = jnp.dot(vc, p_ref[slot, c], preferred_element_type=F32)
        acc_ref[:, cols(c)] = alpha_ref[slot, :, cols(c)] * acc_ref[:, cols(c)] + upd

    def step(score=None, soft=None, pvs=None):
        for c in range(n_chains):
            if soft is not None:
                softmax(soft[1], c, soft[2])
            if score is not None:
                scores(score[0], score[1], c, score[2])
            if pvs is not None:
                pv(pvs[0], pvs[1], c, pvs[2])

    @pl.when(i > 0)
    def _():
        step(score=(0, 0, None))
        step(score=(1, 1, None), soft=(0, 0, None))

        def pair(t):
            j = 2 * t
            step(score=(j + 2, 0, None), soft=(j + 1, 1, None), pvs=(j, 0, None))
            step(score=(j + 3, 1, None), soft=(j + 2, 0, None), pvs=(j + 1, 1, None))

        def body(t, carry):
            pair(t)
            return carry

        lax.fori_loop(0, i - 1, body, 0)

        j = 2 * i - 2
        step(score=(j + 2, 0, 0), soft=(j + 1, 1, None), pvs=(j, 0, None))
        step(score=(j + 3, 1, 1), soft=(j + 2, 0, 0), pvs=(j + 1, 1, None))

    @pl.when(i == 0)
    def _():
        step(score=(0, 0, 0))
        step(score=(1, 1, 1), soft=(0, 0, 0))

    step(soft=(2 * i + 1, 1, 1), pvs=(2 * i, 0, 0))
    step(pvs=(2 * i + 1, 1, 1))

    out = (acc_ref[:DV, :] * (1.0 / acc_ref[DV:DV + 1, :])).T
    g = gate_ref[...]
    o_ref[...] = (out * (g * _sigmoid(g))).astype(BF16)


def _attention(qt, k, vt, gate, tq, tk):
    L = k.shape[1]
    return pl.pallas_call(
        functools.partial(_attn_kernel, tq=tq, tk=tk),
        grid=(HEADS, L // tq),
        in_specs=[pl.BlockSpec((1, QK, tq), lambda h, i: (h, 0, i)),
                  pl.BlockSpec((1, L, QK), lambda h, i: (h, 0, 0)),
                  pl.BlockSpec((1, L // tk, V_ROWS, tk), lambda h, i: (h, 0, 0, 0)),
                  pl.BlockSpec((tq, DV), lambda h, i: (i, h))],
        out_specs=pl.BlockSpec((tq, DV), lambda h, i: (i, h)),
        out_shape=jax.ShapeDtypeStruct((L, D_MLA), BF16),
        scratch_shapes=[pltpu.VMEM((V_ROWS, tq), F32), pltpu.VMEM((1, tq), F32),
                        pltpu.VMEM((2, 1, tq), F32),
                        pltpu.VMEM((2, tq // MXU_COLS, tk, MXU_COLS), BF16),
                        pltpu.VMEM((2, tq // MXU_COLS, tk, MXU_COLS), BF16)],
        compiler_params=_params("parallel", "arbitrary"),
        name="attention",
    )(qt, k, vt, gate)


def _gelu_tanh(x):
    c = math.sqrt(2.0 / math.pi)
    return 0.5 * x * (1.0 + jnp.tanh(c * (x + 0.044715 * (x * x * x))))


S5_CHUNK = 256
PITCH = 68
N_JOBS = 32
JOBS_PER_SLAB = N_JOBS // N_SLABS
TILES_PER_JOB = 2 * STATE_TILES // N_JOBS


def _job_tiles(slab, kk):
    is_im = kk // (JOBS_PER_SLAB // 2)
    q = kk % (JOBS_PER_SLAB // 2)
    tile0 = is_im * STATE_TILES + SLAB_TILES * slab + TILES_PER_JOB * q
    return 2 * slab + q // 2, tile0


def _s5_kernel(un_ref, up_ref, g_ref, wbj_ref, are_ref, aim_ref, wcj_ref, d_ref, wglu_ref,
               o_ref, hs0_ref, hs1_ref, st_ref, acc_ref, ub_ref, *, tt):
    c = pl.program_id(0)
    hs_refs = (hs0_ref, hs1_ref)

    def col_tile(tile):
        return pl.ds(tile, tt, stride=PITCH)

    def load_u_slabs(u_ref):
        u = u_ref[...]
        for s in range(D_SSM // LANES):
            ub_ref[s] = u[:, LANES * s:LANES * (s + 1)].astype(BF16)

    def b_job(slab, kk, slot):
        s, tile0 = _job_tiles(slab, kk)
        r = jnp.dot(ub_ref[s], wbj_ref[slab * JOBS_PER_SLAB + kk], preferred_element_type=F32)
        for k in range(TILES_PER_JOB):
            hs_refs[slot][col_tile(tile0 + k), :] = r[:, LANES * k:LANES * (k + 1)]

    def c_job(slab, kk, slot):
        _, tile0 = _job_tiles(slab, kk)
        lhs = jnp.concatenate([hs_refs[slot][col_tile(tile0 + k), :] for k in range(TILES_PER_JOB)],
                              axis=-1).astype(BF16)
        return jnp.dot(lhs, wcj_ref[slab * JOBS_PER_SLAB + kk], preferred_element_type=F32)

    @pl.when(c == 0)
    def _():
        st_ref[...] = jnp.zeros(st_ref.shape, F32)
        hs1_ref[...] = jnp.zeros(hs1_ref.shape, F32)
        load_u_slabs(up_ref)
        for ti in range(N_JOBS):
            b_job(ti // JOBS_PER_SLAB, ti % JOBS_PER_SLAB, 0)

    load_u_slabs(un_ref)
    a_re = [are_ref[i] for i in range(N_SLABS)]
    a_im = [aim_ref[i] for i in range(N_SLABS)]
    steps_per_job = tt // N_JOBS

    def run(slot):
        hs_ref = hs_refs[slot]

        def scan_steps(t0, st):
            st = list(st)
            for dt in range(steps_per_job):
                base = (t0 + dt) * PITCH
                for i in range(N_SLABS):
                    re_rows = pl.ds(base + SLAB_TILES * i, SLAB_TILES)
                    im_rows = pl.ds(base + STATE_TILES + SLAB_TILES * i, SLAB_TILES)
                    h_re, h_im = st[i], st[N_SLABS + i]
                    n_re = a_re[i] * h_re - a_im[i] * h_im + hs_ref[re_rows, :]
                    n_im = a_re[i] * h_im + a_im[i] * h_re + hs_ref[im_rows, :]
                    hs_ref[re_rows, :] = n_re
                    hs_ref[im_rows, :] = n_im
                    st[i], st[N_SLABS + i] = n_re, n_im
            return st

        st = [st_ref[i] for i in range(2 * N_SLABS)]
        for slab in range(N_SLABS):
            y = None
            for kk in range(JOBS_PER_SLAB):
                st = scan_steps((slab * JOBS_PER_SLAB + kk) * steps_per_job, st)
                part = c_job(slab, kk, 1 - slot)
                y = part if y is None else y + part
                b_job(slab, kk, 1 - slot)
            acc_ref[slab] = y
        for i in range(2 * N_SLABS):
            st_ref[i] = st[i]

    for slot in range(2):
        pl.when(c % 2 == slot)(functools.partial(run, slot))

    y = jnp.concatenate([acc_ref[i] for i in range(N_SLABS)], axis=-1) + d_ref[...] * up_ref[...]
    y = _gelu_tanh(y).astype(BF16)
    z = jnp.dot(y, wglu_ref[...], preferred_element_type=F32)
    out = z[:, :D_SSM] * _sigmoid(z[:, D_SSM:])
    g = g_ref[...]
    o_ref[...] = (out * (g * _sigmoid(g))).astype(BF16)


def _s5(u, gate, wbj, a_re, a_im, wcj, d_skip, w_glu, tt):
    L = u.shape[0]
    n = L // tt
    nxt = pl.BlockSpec((tt, D_SSM), lambda c: (jnp.minimum(c + 1, n - 1), 0))
    prv = pl.BlockSpec((tt, D_SSM), lambda c: (jnp.maximum(c - 1, 0), 0))
    job_cols = TILES_PER_JOB * LANES
    return pl.pallas_call(
        functools.partial(_s5_kernel, tt=tt),
        grid=(n + 1,),
        in_specs=[nxt, prv, prv,
                  _resident((N_JOBS, LANES, job_cols)),
                  _resident((N_SLABS, SUBLANES, LANES)), _resident((N_SLABS, SUBLANES, LANES)),
                  _resident((N_JOBS, job_cols, D_SSM // N_SLABS)),
                  _resident((1, D_SSM)), _resident((D_SSM, 2 * D_SSM))],
        out_specs=prv,
        out_shape=jax.ShapeDtypeStruct((L, D_SSM), BF16),
        scratch_shapes=[pltpu.VMEM((tt * PITCH, LANES), F32), pltpu.VMEM((tt * PITCH, LANES), F32),
                        pltpu.VMEM((2 * N_SLABS, SUBLANES, LANES), F32),
                        pltpu.VMEM((N_SLABS, tt, D_SSM // N_SLABS), F32),
                        pltpu.VMEM((D_SSM // LANES, tt, LANES), BF16)],
        compiler_params=_params("arbitrary"),
        name="s5",
    )(u, u, gate, wbj, a_re, a_im, wcj, d_skip, w_glu)


def _outproj_kernel(mla_ref, ssm_ref, w_ref, gpost_ref, x_ref, o_ref):
    out = jnp.dot(mla_ref[...], w_ref[:D_MLA, :], preferred_element_type=F32)
    out += jnp.dot(ssm_ref[...], w_ref[D_MLA:, :], preferred_element_type=F32)
    inv = lax.rsqrt(jnp.mean(out * out, axis=-1, keepdims=True) + NORM_EPS)
    o_ref[...] = x_ref[...] + out * inv * gpost_ref[...]


def _outproj(mla, ssm, w_out, g_post, x, tm):
    L = x.shape[0]
    row = lambda n: pl.BlockSpec((tm, n), lambda i: (i, 0))
    return pl.pallas_call(
        _outproj_kernel,
        grid=(L // tm,),
        in_specs=[row(D_MLA), row(D_SSM), _resident((D_MLA + D_SSM, D_MODEL)),
                  _resident((1, D_MODEL)), row(D_MODEL)],
        out_specs=row(D_MODEL),
        out_shape=jax.ShapeDtypeStruct((L, D_MODEL), F32),
        compiler_params=_params("parallel"),
        name="outproj",
    )(mla, ssm, w_out, g_post, x)


def _split_in_weights(w_in):
    s2 = Q_LORA + KV_LORA
    s3 = s2 + ROPE
    w = lax.optimization_barrier(w_in.astype(BF16))
    w_kr = jnp.pad(w[..., s2:s3], ((0, 0), (0, 0), (0, LANES - ROPE)))
    return w[..., :s2], w_kr, w[..., s3:]


def _pack_b(bb_re, bb_im):
    depth = bb_re.shape[0]
    gpt = LANES // GROUP_CH
    eye = jnp.eye(gpt, dtype=F32)

    def pack(b):
        b = b.reshape(depth, GROUPS // gpt, gpt, NSTATE, GROUP_CH)
        return jnp.einsum('dsgph,gk->dsghkp', b, eye).reshape(depth, GROUPS // gpt, LANES, gpt * NSTATE)

    return jnp.concatenate([pack(bb_re), pack(bb_im)], axis=-1).astype(BF16)


def _pack_c(c):
    depth = c.shape[0]
    gps = SLAB_TILES * LANES // NSTATE
    eye = jnp.eye(gps, dtype=F32)
    c = c.reshape(depth, N_SLABS, gps, GROUP_CH, NSTATE)
    return jnp.einsum('dighp,gk->digpkh', c, eye).reshape(
        depth, N_SLABS, gps * NSTATE, gps * GROUP_CH).astype(BF16)


def kernel(x, positions, norm_pre, norm_post, w_in, q_norm, w_uq, kv_norm, w_ukv, ssm_a_re, ssm_a_im,
           ssm_b_re, ssm_b_im, ssm_c_re, ssm_c_im, ssm_d, ssm_log_step, w_glu, w_out):
    batch, L, _ = x.shape
    assert batch == 1
    depth = w_in.shape[0]
    tm = min(256, L)
    tm_out = min(512, L)
    tq = min(2048, L)
    tk = tq // 2
    tt = S5_CHUNK

    rope = _rope_tables(positions, tk)

    w_lat, w_kr, w_wide = _split_in_weights(w_in)
    wqt = w_uq.astype(BF16).reshape(depth, Q_LORA, HEADS, QK).transpose(0, 2, 3, 1)
    w_ukv4 = w_ukv.astype(BF16).reshape(depth, KV_LORA, HEADS, NOPE + DV)
    wk = w_ukv4[..., :NOPE].reshape(depth, KV_LORA, HEADS * NOPE)
    wvt = w_ukv4[..., NOPE:].transpose(0, 2, 3, 1)
    w_glu_b = w_glu.astype(BF16)
    w_out_b = w_out.astype(BF16)

    ab_re, ab_im, bb_re, bb_im = _s5_prep(ssm_a_re, ssm_a_im, ssm_log_step, ssm_b_re, ssm_b_im)
    wb = _pack_b(bb_re, bb_im)
    half = wb.shape[-1] // 2
    job_cols = TILES_PER_JOB * LANES
    wbj = []
    for ti in range(N_JOBS):
        s, tile0 = _job_tiles(ti // JOBS_PER_SLAB, ti % JOBS_PER_SLAB)
        col = (tile0 // STATE_TILES) * half + (tile0 % STATE_TILES) * LANES - s * half
        wbj.append(wb[:, s, :, col:col + job_cols])
    wbj = jnp.stack(wbj, axis=1)
    slab_shape = (depth, N_SLABS, SUBLANES, LANES)
    a_re_s = ab_re.reshape(slab_shape)
    a_im_s = ab_im.reshape(slab_shape)
    wc_re = _pack_c(ssm_c_re).reshape(depth, N_SLABS, JOBS_PER_SLAB // 2, job_cols, -1)
    wc_im = _pack_c(-ssm_c_im).reshape(depth, N_SLABS, JOBS_PER_SLAB // 2, job_cols, -1)
    wcj = jnp.concatenate([wc_re, wc_im], axis=2).reshape(depth, N_JOBS, job_cols, -1)

    h = x[0]
    for i in range(depth):
        cq, ckv, kr, gate_mla, u, gate_ssm = _inproj(
            h, norm_pre[i][None], w_lat[i], w_kr[i], w_wide[i], q_norm[i][None], kv_norm[i][None], tm)
        qt, k, vt = _qkv(cq, ckv, kr, rope, wqt[i], wk[i], wvt[i], tk)
        mla = _attention(qt, k, vt, gate_mla, tq, tk)
        ssm = _s5(u, gate_ssm, wbj[i], a_re_s[i], a_im_s[i], wcj[i], ssm_d[i][None], w_glu_b[i], tt)
        h = _outproj(mla, ssm, w_out_b[i], norm_post[i][None], h, tm_out)
    return h[None]
```

```python
import functools
import math

import jax
import jax.numpy as jnp
from jax import lax
from jax.experimental import pallas as pl
from jax.experimental.pallas import tpu as pltpu

F32 = jnp.float32
BF16 = jnp.bfloat16

D_MODEL = 2048
HEADS = 8
NOPE = 128
ROPE = 64
QK = NOPE + ROPE
DV = 128
V_ROWS = DV + 16
Q_LORA = 512
KV_LORA = 256
D_MLA = HEADS * DV
D_SSM = 1024
GROUP_CH = 16
GROUPS = D_SSM // GROUP_CH
NSTATE = 64
ROPE_THETA = 10000.0
NORM_EPS = 1e-6
LANES = 128
SUBLANES = 8
MXU_COLS = 256
BF16_ROWS = 16
STATE_TILES = GROUPS * NSTATE // LANES
SLAB_TILES = SUBLANES
N_SLABS = STATE_TILES // SLAB_TILES
VMEM_LIMIT = 56 * 1024 * 1024


def _resident(shape):
    zeros = (0,) * len(shape)
    return pl.BlockSpec(shape, lambda *_: zeros, pipeline_mode=pl.Buffered(1))


def _sigmoid(x):
    return 1.0 / (1.0 + jnp.exp(-x))


def _params(*sem):
    return pltpu.CompilerParams(dimension_semantics=sem, vmem_limit_bytes=VMEM_LIMIT)


def _s5_prep_kernel(are_ref, aim_ref, lstep_ref, bre_ref, bim_ref,
                    abre_ref, abim_ref, bbre_ref, bbim_ref):
    a_re = are_ref[...]
    a_im = aim_ref[...]
    step = jnp.exp(lstep_ref[...])
    mag = jnp.exp(step * a_re)
    ab_re = mag * jnp.cos(step * a_im)
    ab_im = mag * jnp.sin(step * a_im)
    n_re = ab_re - 1.0
    den = a_re * a_re + a_im * a_im
    z_re = (n_re * a_re + ab_im * a_im) / den
    z_im = (ab_im * a_re - n_re * a_im) / den
    b_re = bre_ref[...]
    b_im = bim_ref[...]
    abre_ref[...] = ab_re
    abim_ref[...] = ab_im
    bbre_ref[...] = z_re * b_re - z_im * b_im
    bbim_ref[...] = z_re * b_im + z_im * b_re


def _s5_prep(a_re, a_im, log_step, b_re, b_im):
    depth = a_re.shape[0]
    rows = depth * GROUPS
    cols = NSTATE * GROUP_CH

    def expand(a):
        return jnp.broadcast_to(a[..., None], (depth, GROUPS, NSTATE, GROUP_CH)).reshape(rows, cols)

    step = jnp.broadcast_to(log_step[..., None], (depth, GROUPS, cols)).reshape(rows, cols)
    sds = jax.ShapeDtypeStruct((rows, cols), F32)
    ab_re, ab_im, bb_re, bb_im = pl.pallas_call(
        _s5_prep_kernel, out_shape=(sds, sds, sds, sds), name="s5_prep",
    )(expand(a_re), expand(a_im), step, b_re.reshape(rows, cols), b_im.reshape(rows, cols))
    shape4 = (depth, GROUPS, NSTATE, GROUP_CH)
    ab_re = ab_re.reshape(shape4)[..., 0]
    ab_im = ab_im.reshape(shape4)[..., 0]
    return ab_re, ab_im, bb_re.reshape(shape4), bb_im.reshape(shape4)


def _inproj_kernel(x_ref, gpre_ref, wlat_ref, wkr_ref, wwide_ref, gq_ref, gkv_ref,
                   cq_ref, ckv_ref, kr_ref, gm_ref, u_ref, gs_ref):
    x = x_ref[...]
    inv = lax.rsqrt(jnp.mean(x * x, axis=-1, keepdims=True) + NORM_EPS)
    h = (x * inv * gpre_ref[...]).astype(BF16)

    def proj(w_ref, lo, hi):
        return jnp.dot(h, w_ref[:, lo:hi], preferred_element_type=F32)

    def latent_norm(c, g_ref):
        inv_c = lax.rsqrt(jnp.mean(c * c, axis=-1, keepdims=True) + NORM_EPS)
        return (c * inv_c * g_ref[...]).astype(BF16)

    cq_ref[...] = latent_norm(proj(wlat_ref, 0, Q_LORA), gq_ref)
    ckv_ref[...] = latent_norm(proj(wlat_ref, Q_LORA, Q_LORA + KV_LORA), gkv_ref)
    kr_ref[...] = proj(wkr_ref, 0, LANES)
    gm_ref[...] = proj(wwide_ref, 0, D_MLA)
    u_ref[...] = proj(wwide_ref, D_MLA, D_MLA + D_SSM)
    gs_ref[...] = proj(wwide_ref, D_MLA + D_SSM, D_MLA + 2 * D_SSM)


def _inproj(x, g_pre, w_lat, w_kr, w_wide, g_q, g_kv, tm):
    L = x.shape[0]
    row = lambda n: pl.BlockSpec((tm, n), lambda i: (i, 0))
    out_shape = (
        jax.ShapeDtypeStruct((L, Q_LORA), BF16),
        jax.ShapeDtypeStruct((L, KV_LORA), BF16),
        jax.ShapeDtypeStruct((L, LANES), F32),
        jax.ShapeDtypeStruct((L, D_MLA), F32),
        jax.ShapeDtypeStruct((L, D_SSM), F32),
        jax.ShapeDtypeStruct((L, D_SSM), F32),
    )
    return pl.pallas_call(
        _inproj_kernel,
        grid=(L // tm,),
        in_specs=[row(D_MODEL), _resident((1, D_MODEL)), _resident((D_MODEL, Q_LORA + KV_LORA)),
                  _resident((D_MODEL, LANES)), _resident((D_MODEL, D_MLA + 2 * D_SSM)),
                  _resident((1, Q_LORA)), _resident((1, KV_LORA))],
        out_specs=(row(Q_LORA), row(KV_LORA), row(LANES), row(D_MLA), row(D_SSM), row(D_SSM)),
        out_shape=out_shape,
        compiler_params=_params("parallel"),
        name="inproj",
    )(x, g_pre, w_lat, w_kr, w_wide, g_q, g_kv)


def _rope_kernel(posc_ref, posr_ref, fcol_ref, frow_ref, cosr_ref, sinr_ref, cost_ref, sint_ref):
    ang = posc_ref[...].astype(F32) * frow_ref[...]
    cosr_ref[...] = jnp.cos(ang)
    sinr_ref[...] = jnp.sin(ang)
    ang_t = fcol_ref[...] * posr_ref[...].astype(F32)
    cost_ref[...] = jnp.cos(ang_t)
    sint_ref[...] = jnp.sin(ang_t)


def _rope_tables(positions, tm):
    L = positions.shape[-1]
    inv_freq = ROPE_THETA ** (-jnp.arange(0, ROPE, 2, dtype=F32) / ROPE)
    inv_freq = jnp.concatenate([inv_freq, inv_freq])
    row = pl.BlockSpec((tm, ROPE), lambda i: (i, 0))
    col = pl.BlockSpec((ROPE, tm), lambda i: (0, i))
    return pl.pallas_call(
        _rope_kernel,
        grid=(L // tm,),
        in_specs=[pl.BlockSpec((tm, 1), lambda i: (i, 0)), pl.BlockSpec((1, tm), lambda i: (0, i)),
                  _resident((ROPE, 1)), _resident((1, ROPE))],
        out_specs=(row, row, col, col),
        out_shape=(jax.ShapeDtypeStruct((L, ROPE), F32), jax.ShapeDtypeStruct((L, ROPE), F32),
                   jax.ShapeDtypeStruct((ROPE, L), F32), jax.ShapeDtypeStruct((ROPE, L), F32)),
        compiler_params=_params("parallel"),
        name="rope_tables",
    )(positions.reshape(L, 1), positions.reshape(1, L), inv_freq.reshape(ROPE, 1), inv_freq.reshape(1, ROPE))


def _qkv_kernel(cq_ref, ckv_ref, kr_ref, cosr_ref, sinr_ref, cost_ref, sint_ref,
                wqt_ref, wk_ref, wvt_ref, qt_ref, k_ref, vt_ref, *, q_scale):
    nt = (((1,), (1,)), ((), ()))
    cq = cq_ref[...]
    ckv = ckv_ref[...]
    half = ROPE // 2
    cos_r = cosr_ref[...]
    sin_r = sinr_ref[...]
    cos_t = cost_ref[...]
    sin_t = sint_ref[...]

    kr = kr_ref[:, :ROPE]
    kr_rot = jnp.concatenate([-kr[:, half:], kr[:, :half]], axis=-1)
    k_rope = (kr * cos_r + kr_rot * sin_r).astype(BF16)
    k_nope = jnp.dot(ckv, wk_ref[...], preferred_element_type=F32)

    pad_row = lax.broadcasted_iota(jnp.int32, (V_ROWS - DV, vt_ref.shape[-1]), 0)
    ones_rows = jnp.where(pad_row == 0, 1.0, 0.0).astype(BF16)

    for h in range(HEADS):
        qt = lax.dot_general(wqt_ref[h], cq, nt, preferred_element_type=F32)
        qr = qt[NOPE:]
        qr_rot = jnp.concatenate([-qr[half:], qr[:half]], axis=0)
        qr = qr * cos_t + qr_rot * sin_t
        qt_ref[h, :NOPE, :] = (qt[:NOPE] * q_scale).astype(BF16)
        qt_ref[h, NOPE:, :] = (qr * q_scale).astype(BF16)
        k_ref[h, :, :NOPE] = k_nope[:, h * NOPE:(h + 1) * NOPE].astype(BF16)
        k_ref[h, :, NOPE:] = k_rope
        vt = lax.dot_general(wvt_ref[h], ckv, nt, preferred_element_type=F32)
        vt_ref[h, 0, :DV, :] = vt.astype(BF16)
        vt_ref[h, 0, DV:, :] = ones_rows


def _qkv(cq, ckv, kr, rope, wqt, wk, wvt, tk):
    L = cq.shape[0]
    q_scale = math.log2(math.e) / math.sqrt(QK)
    row = lambda n: pl.BlockSpec((tk, n), lambda i: (i, 0))
    return pl.pallas_call(
        functools.partial(_qkv_kernel, q_scale=q_scale),
        grid=(L // tk,),
        in_specs=[row(Q_LORA), row(KV_LORA), row(LANES), row(ROPE), row(ROPE),
                  pl.BlockSpec((ROPE, tk), lambda i: (0, i)), pl.BlockSpec((ROPE, tk), lambda i: (0, i)),
                  _resident((HEADS, QK, Q_LORA)), _resident((KV_LORA, HEADS * NOPE)),
                  _resident((HEADS, DV, KV_LORA))],
        out_specs=(pl.BlockSpec((HEADS, QK, tk), lambda i: (0, 0, i)),
                   pl.BlockSpec((HEADS, tk, QK), lambda i: (0, i, 0)),
                   pl.BlockSpec((HEADS, 1, V_ROWS, tk), lambda i: (0, i, 0, 0))),
        out_shape=(jax.ShapeDtypeStruct((HEADS, QK, L), BF16),
                   jax.ShapeDtypeStruct((HEADS, L, QK), BF16),
                   jax.ShapeDtypeStruct((HEADS, L // tk, V_ROWS, tk), BF16)),
        compiler_params=_params("parallel"),
        name="qkv",
    )(cq, ckv, kr, *rope, wqt, wk, wvt)


def _attn_kernel(qt_ref, k_ref, vt_ref, gate_ref, o_ref, acc_ref, m_ref, alpha_ref, s_ref, p_ref, *, tq, tk):
    i = pl.program_id(1)
    m_ref[...] = jnp.full(m_ref.shape, -1e30, F32)
    acc_ref[...] = jnp.zeros(acc_ref.shape, F32)
    assert tq == 2 * tk
    n_chains = tq // MXU_COLS

    def cols(c):
        return slice(MXU_COLS * c, MXU_COLS * (c + 1))

    def n_keys(c, diag):
        return tk if diag is None else max(0, min(tk, MXU_COLS * (c + 1) - diag * tk))

    def scores(j, slot, c, diag):
        rows = n_keys(c, diag)
        if rows == 0:
            return
        kc = k_ref[0, pl.ds(pl.multiple_of(j * tk, tk), rows), :]
        s = jnp.dot(kc, qt_ref[0, :, cols(c)], preferred_element_type=F32)
        if diag is not None and diag * tk + rows - 1 > MXU_COLS * c:
            key = diag * tk + lax.broadcasted_iota(jnp.int32, s.shape, 0)
            qry = MXU_COLS * c + lax.broadcasted_iota(jnp.int32, s.shape, 1)
            s = jnp.where(key <= qry, s, -1e30)
        s_ref[slot, c, :rows, :] = s.astype(BF16)

    def softmax(slot, c, diag):
        rows = n_keys(c, diag)
        if rows == 0:
            return
        s = s_ref[slot, c, :rows, :]
        m_blk = jnp.max(s.reshape(rows // BF16_ROWS, BF16_ROWS, MXU_COLS), axis=0)
        m_blk = jnp.max(m_blk.astype(F32), axis=0, keepdims=True)
        m_prev = m_ref[:, cols(c)]
        m_new = jnp.maximum(m_prev, m_blk)
        alpha_ref[slot, :, cols(c)] = jnp.exp2(m_prev - m_new)
        p_ref[slot, c, :rows, :] = jnp.exp2(s - m_new.astype(BF16))
        m_ref[:, cols(c)] = m_new

    def pv(j, slot, c, diag):
        rows = n_keys(c, diag)
        if rows == 0:
            return
        vc = vt_ref[0, j, :, :rows]
        upd = jnp.dot(vc, p_ref[slot, c, :rows, :], preferred_element_type=F32)
        acc_ref[:, cols(c)] = alpha_ref[slot, :, cols(c)] * acc_ref[:, cols(c)] + upd

    def step(score=None, soft=None, pvs=None):
        for c in range(n_chains):
            if soft is not None:
                softmax(soft[1], c, soft[2])
            if score is not None:
                scores(score[0], score[1], c, score[2])
            if pvs is not None:
                pv(pvs[0], pvs[1], c, pvs[2])

    @pl.when(i > 0)
    def _():
        step(score=(0, 0, None))
        step(score=(1, 1, None), soft=(0, 0, None))

        def pair(t):
            j = 2 * t
            step(score=(j + 2, 0, None), soft=(j + 1, 1, None), pvs=(j, 0, None))
            step(score=(j + 3, 1, None), soft=(j + 2, 0, None), pvs=(j + 1, 1, None))

        def body(t, carry):
            pair(t)
            return carry

        lax.fori_loop(0, i - 1, body, 0)

        j = 2 * i - 2
        step(score=(j + 2, 0, 0), soft=(j + 1, 1, None), pvs=(j, 0, None))
        step(score=(j + 3, 1, 1), soft=(j + 2, 0, 0), pvs=(j + 1, 1, None))

    @pl.when(i == 0)
    def _():
        step(score=(0, 0, 0))
        step(score=(1, 1, 1), soft=(0, 0, 0))

    step(soft=(2 * i + 1, 1, 1), pvs=(2 * i, 0, 0))
    step(pvs=(2 * i + 1, 1, 1))

    out = (acc_ref[:DV, :] * (1.0 / acc_ref[DV:DV + 1, :])).T
    g = gate_ref[...]
    o_ref[...] = (out * (g * _sigmoid(g))).astype(BF16)


def _attention(qt, k, vt, gate, tq, tk):
    L = k.shape[1]
    return pl.pallas_call(
        functools.partial(_attn_kernel, tq=tq, tk=tk),
        grid=(HEADS, L // tq),
        in_specs=[pl.BlockSpec((1, QK, tq), lambda h, i: (h, 0, i)),
                  pl.BlockSpec((1, L, QK), lambda h, i: (h, 0, 0)),
                  pl.BlockSpec((1, L // tk, V_ROWS, tk), lambda h, i: (h, 0, 0, 0)),
                  pl.BlockSpec((tq, DV), lambda h, i: (i, h))],
        out_specs=pl.BlockSpec((tq, DV), lambda h, i: (i, h)),
        out_shape=jax.ShapeDtypeStruct((L, D_MLA), BF16),
        scratch_shapes=[pltpu.VMEM((V_ROWS, tq), F32), pltpu.VMEM((1, tq), F32),
                        pltpu.VMEM((2, 1, tq), F32),
                        pltpu.VMEM((2, tq // MXU_COLS, tk, MXU_COLS), BF16),
                        pltpu.VMEM((2, tq // MXU_COLS, tk, MXU_COLS), BF16)],
        compiler_params=_params("parallel", "arbitrary"),
        name="attention",
    )(qt, k, vt, gate)


def _gelu_tanh(x):
    c = math.sqrt(2.0 / math.pi)
    return 0.5 * x * (1.0 + jnp.tanh(c * (x + 0.044715 * (x * x * x))))


S5_CHUNK = 256
PITCH = 68
N_JOBS = 32
JOBS_PER_SLAB = N_JOBS // N_SLABS
TILES_PER_JOB = 2 * STATE_TILES // N_JOBS


def _job_tiles(slab, kk):
    is_im = kk // (JOBS_PER_SLAB // 2)
    q = kk % (JOBS_PER_SLAB // 2)
    tile0 = is_im * STATE_TILES + SLAB_TILES * slab + TILES_PER_JOB * q
    return 2 * slab + q // 2, tile0


def _s5_kernel(un_ref, up_ref, g_ref, wbj_ref, are_ref, aim_ref, wcj_ref, d_ref, wglu_ref,
               o_ref, hs0_ref, hs1_ref, st_ref, acc_ref, ub_ref, *, tt):
    c = pl.program_id(0)
    hs_refs = (hs0_ref, hs1_ref)

    def col_tile(tile):
        return pl.ds(tile, tt, stride=PITCH)

    def load_u_slabs(u_ref):
        u = u_ref[...]
        for s in range(D_SSM // LANES):
            ub_ref[s] = u[:, LANES * s:LANES * (s + 1)].astype(BF16)

    def b_job(slab, kk, slot):
        s, tile0 = _job_tiles(slab, kk)
        r = jnp.dot(ub_ref[s], wbj_ref[slab * JOBS_PER_SLAB + kk], preferred_element_type=F32)
        for k in range(TILES_PER_JOB):
            hs_refs[slot][col_tile(tile0 + k), :] = r[:, LANES * k:LANES * (k + 1)]

    def c_job(slab, kk, slot):
        _, tile0 = _job_tiles(slab, kk)
        lhs = jnp.concatenate([hs_refs[slot][col_tile(tile0 + k), :] for k in range(TILES_PER_JOB)],
                              axis=-1).astype(BF16)
        return jnp.dot(lhs, wcj_ref[slab * JOBS_PER_SLAB + kk], preferred_element_type=F32)

    @pl.when(c == 0)
    def _():
        st_ref[...] = jnp.zeros(st_ref.shape, F32)
        hs1_ref[...] = jnp.zeros(hs1_ref.shape, F32)
        load_u_slabs(up_ref)
        for ti in range(N_JOBS):
            b_job(ti // JOBS_PER_SLAB, ti % JOBS_PER_SLAB, 0)

    load_u_slabs(un_ref)
    a_re = [are_ref[i] for i in range(N_SLABS)]
    a_im = [aim_ref[i] for i in range(N_SLABS)]
    steps_per_job = tt // N_JOBS

    def run(slot):
        hs_ref = hs_refs[slot]

        def scan_steps(t0, st):
            st = list(st)
            for dt in range(steps_per_job):
                base = (t0 + dt) * PITCH
                for i in range(N_SLABS):
                    re_rows = pl.ds(base + SLAB_TILES * i, SLAB_TILES)
                    im_rows = pl.ds(base + STATE_TILES + SLAB_TILES * i, SLAB_TILES)
                    h_re, h_im = st[i], st[N_SLABS + i]
                    n_re = a_re[i] * h_re - a_im[i] * h_im + hs_ref[re_rows, :]
                    n_im = a_re[i] * h_im + a_im[i] * h_re + hs_ref[im_rows, :]
                    hs_ref[re_rows, :] = n_re
                    hs_ref[im_rows, :] = n_im
                    st[i], st[N_SLABS + i] = n_re, n_im
            return st

        st = [st_ref[i] for i in range(2 * N_SLABS)]
        for slab in range(N_SLABS):
            y = None
            for kk in range(JOBS_PER_SLAB):
                st = scan_steps((slab * JOBS_PER_SLAB + kk) * steps_per_job, st)
                part = c_job(slab, kk, 1 - slot)
                y = part if y is None else y + part
                b_job(slab, kk, 1 - slot)
            acc_ref[slab] = y
        for i in range(2 * N_SLABS):
            st_ref[i] = st[i]

    for slot in range(2):
        pl.when(c % 2 == slot)(functools.partial(run, slot))

    y = jnp.concatenate([acc_ref[i] for i in range(N_SLABS)], axis=-1) + d_ref[...] * up_ref[...]
    y = _gelu_tanh(y).astype(BF16)
    z = jnp.dot(y, wglu_ref[...], preferred_element_type=F32)
    out = z[:, :D_SSM] * _sigmoid(z[:, D_SSM:])
    g = g_ref[...]
    o_ref[...] = (out * (g * _sigmoid(g))).astype(BF16)


def _s5(u, gate, wbj, a_re, a_im, wcj, d_skip, w_glu, tt):
    L = u.shape[0]
    n = L // tt
    nxt = pl.BlockSpec((tt, D_SSM), lambda c: (jnp.minimum(c + 1, n - 1), 0))
    prv = pl.BlockSpec((tt, D_SSM), lambda c: (jnp.maximum(c - 1, 0), 0))
    job_cols = TILES_PER_JOB * LANES
    return pl.pallas_call(
        functools.partial(_s5_kernel, tt=tt),
        grid=(n + 1,),
        in_specs=[nxt, prv, prv,
                  _resident((N_JOBS, LANES, job_cols)),
                  _resident((N_SLABS, SUBLANES, LANES)), _resident((N_SLABS, SUBLANES, LANES)),
                  _resident((N_JOBS, job_cols, D_SSM // N_SLABS)),
                  _resident((1, D_SSM)), _resident((D_SSM, 2 * D_SSM))],
        out_specs=prv,
        out_shape=jax.ShapeDtypeStruct((L, D_SSM), BF16),
        scratch_shapes=[pltpu.VMEM((tt * PITCH, LANES), F32), pltpu.VMEM((tt * PITCH, LANES), F32),
                        pltpu.VMEM((2 * N_SLABS, SUBLANES, LANES), F32),
                        pltpu.VMEM((N_SLABS, tt, D_SSM // N_SLABS), F32),
                        pltpu.VMEM((D_SSM // LANES, tt, LANES), BF16)],
        compiler_params=_params("arbitrary"),
        name="s5",
    )(u, u, gate, wbj, a_re, a_im, wcj, d_skip, w_glu)


def _outproj_kernel(mla_ref, ssm_ref, w_ref, gpost_ref, x_ref, o_ref):
    out = jnp.dot(mla_ref[...], w_ref[:D_MLA, :], preferred_element_type=F32)
    out += jnp.dot(ssm_ref[...], w_ref[D_MLA:, :], preferred_element_type=F32)
    inv = lax.rsqrt(jnp.mean(out * out, axis=-1, keepdims=True) + NORM_EPS)
    o_ref[...] = x_ref[...] + out * inv * gpost_ref[...]


def _outproj(mla, ssm, w_out, g_post, x, tm):
    L = x.shape[0]
    row = lambda n: pl.BlockSpec((tm, n), lambda i: (i, 0))
    return pl.pallas_call(
        _outproj_kernel,
        grid=(L // tm,),
        in_specs=[row(D_MLA), row(D_SSM), _resident((D_MLA + D_SSM, D_MODEL)),
                  _resident((1, D_MODEL)), row(D_MODEL)],
        out_specs=row(D_MODEL),
        out_shape=jax.ShapeDtypeStruct((L, D_MODEL), F32),
        compiler_params=_params("parallel"),
        name="outproj",
    )(mla, ssm, w_out, g_post, x)


def _split_in_weights(w_in):
    s2 = Q_LORA + KV_LORA
    s3 = s2 + ROPE
    w = lax.optimization_barrier(w_in.astype(BF16))
    w_kr = jnp.pad(w[..., s2:s3], ((0, 0), (0, 0), (0, LANES - ROPE)))
    return w[..., :s2], w_kr, w[..., s3:]


def _pack_b(bb_re, bb_im):
    depth = bb_re.shape[0]
    gpt = LANES // GROUP_CH
    eye = jnp.eye(gpt, dtype=F32)

    def pack(b):
        b = b.reshape(depth, GROUPS // gpt, gpt, NSTATE, GROUP_CH)
        return jnp.einsum('dsgph,gk->dsghkp', b, eye).reshape(depth, GROUPS // gpt, LANES, gpt * NSTATE)

    return jnp.concatenate([pack(bb_re), pack(bb_im)], axis=-1).astype(BF16)


def _pack_c(c):
    depth = c.shape[0]
    gps = SLAB_TILES * LANES // NSTATE
    eye = jnp.eye(gps, dtype=F32)
    c = c.reshape(depth, N_SLABS, gps, GROUP_CH, NSTATE)
    return jnp.einsum('dighp,gk->digpkh', c, eye).reshape(
        depth, N_SLABS, gps * NSTATE, gps * GROUP_CH).astype(BF16)


def kernel(x, positions, norm_pre, norm_post, w_in, q_norm, w_uq, kv_norm, w_ukv, ssm_a_re, ssm_a_im,
           ssm_b_re, ssm_b_im, ssm_c_re, ssm_c_im, ssm_d, ssm_log_step, w_glu, w_out):
    batch, L, _ = x.shape
    assert batch == 1
    depth = w_in.shape[0]
    tm = min(256, L)
    tm_out = min(512, L)
    tq = min(2048, L)
    tk = tq // 2
    tt = S5_CHUNK

    rope = _rope_tables(positions, tk)

    w_lat, w_kr, w_wide = _split_in_weights(w_in)
    wqt = w_uq.astype(BF16).reshape(depth, Q_LORA, HEADS, QK).transpose(0, 2, 3, 1)
    w_ukv4 = w_ukv.astype(BF16).reshape(depth, KV_LORA, HEADS, NOPE + DV)
    wk = w_ukv4[..., :NOPE].reshape(depth, KV_LORA, HEADS * NOPE)
    wvt = w_ukv4[..., NOPE:].transpose(0, 2, 3, 1)
    w_glu_b = w_glu.astype(BF16)
    w_out_b = w_out.astype(BF16)

    ab_re, ab_im, bb_re, bb_im = _s5_prep(ssm_a_re, ssm_a_im, ssm_log_step, ssm_b_re, ssm_b_im)
    wb = _pack_b(bb_re, bb_im)
    half = wb.shape[-1] // 2
    job_cols = TILES_PER_JOB * LANES
    wbj = []
    for ti in range(N_JOBS):
        s, tile0 = _job_tiles(ti // JOBS_PER_SLAB, ti % JOBS_PER_SLAB)
        col = (tile0 // STATE_TILES) * half + (tile0 % STATE_TILES) * LANES - s * half
        wbj.append(wb[:, s, :, col:col + job_cols])
    wbj = jnp.stack(wbj, axis=1)
    slab_shape = (depth, N_SLABS, SUBLANES, LANES)
    a_re_s = ab_re.reshape(slab_shape)
    a_im_s = ab_im.reshape(slab_shape)
    wc_re = _pack_c(ssm_c_re).reshape(depth, N_SLABS, JOBS_PER_SLAB // 2, job_cols, -1)
    wc_im = _pack_c(-ssm_c_im).reshape(depth, N_SLABS, JOBS_PER_SLAB // 2, job_cols, -1)
    wcj = jnp.concatenate([wc_re, wc_im], axis=2).reshape(depth, N_JOBS, job_cols, -1)

    h = x[0]
    for i in range(depth):
        cq, ckv, kr, gate_mla, u, gate_ssm = _inproj(
            h, norm_pre[i][None], w_lat[i], w_kr[i], w_wide[i], q_norm[i][None], kv_norm[i][None], tm)
        qt, k, vt = _qkv(cq, ckv, kr, rope, wqt[i], wk[i], wvt[i], tk)
        mla = _attention(qt, k, vt, gate_mla, tq, tk)
        ssm = _s5(u, gate_ssm, wbj[i], a_re_s[i], a_im_s[i], wcj[i], ssm_d[i][None], w_glu_b[i], tt)
        h = _outproj(mla, ssm, w_out_b[i], norm_post[i][None], h, tm_out)
    return h[None]
```

```python
import functools
import math

import jax
import jax.numpy as jnp
from jax import lax
from jax.experimental import pallas as pl
from jax.experimental.pallas import tpu as pltpu

F32 = jnp.float32
BF16 = jnp.bfloat16

D_MODEL = 2048
HEADS = 8
NOPE = 128
ROPE = 64
QK = NOPE + ROPE
DV = 128
V_ROWS = DV + 16
Q_LORA = 512
KV_LORA = 256
D_MLA = HEADS * DV
D_SSM = 1024
GROUP_CH = 16
GROUPS = D_SSM // GROUP_CH
NSTATE = 64
ROPE_THETA = 10000.0
NORM_EPS = 1e-6
LANES = 128
SUBLANES = 8
MXU_COLS = 256
BF16_ROWS = 16
STATE_TILES = GROUPS * NSTATE // LANES
SLAB_TILES = SUBLANES
N_SLABS = STATE_TILES // SLAB_TILES
VMEM_LIMIT = 56 * 1024 * 1024


def _resident(shape):
    zeros = (0,) * len(shape)
    return pl.BlockSpec(shape, lambda *_: zeros, pipeline_mode=pl.Buffered(1))


def _sigmoid(x):
    return 1.0 / (1.0 + jnp.exp(-x))


def _params(*sem):
    return pltpu.CompilerParams(dimension_semantics=sem, vmem_limit_bytes=VMEM_LIMIT)


def _s5_prep_kernel(are_ref, aim_ref, lstep_ref, bre_ref, bim_ref,
                    abre_ref, abim_ref, bbre_ref, bbim_ref):
    a_re = are_ref[...]
    a_im = aim_ref[...]
    step = jnp.exp(lstep_ref[...])
    mag = jnp.exp(step * a_re)
    ab_re = mag * jnp.cos(step * a_im)
    ab_im = mag * jnp.sin(step * a_im)
    n_re = ab_re - 1.0
    den = a_re * a_re + a_im * a_im
    z_re = (n_re * a_re + ab_im * a_im) / den
    z_im = (ab_im * a_re - n_re * a_im) / den
    b_re = bre_ref[...]
    b_im = bim_ref[...]
    abre_ref[...] = ab_re
    abim_ref[...] = ab_im
    bbre_ref[...] = z_re * b_re - z_im * b_im
    bbim_ref[...] = z_re * b_im + z_im * b_re


def _s5_prep(a_re, a_im, log_step, b_re, b_im):
    depth = a_re.shape[0]
    rows = depth * GROUPS
    cols = NSTATE * GROUP_CH

    def expand(a):
        return jnp.broadcast_to(a[..., None], (depth, GROUPS, NSTATE, GROUP_CH)).reshape(rows, cols)

    step = jnp.broadcast_to(log_step[..., None], (depth, GROUPS, cols)).reshape(rows, cols)
    sds = jax.ShapeDtypeStruct((rows, cols), F32)
    ab_re, ab_im, bb_re, bb_im = pl.pallas_call(
        _s5_prep_kernel, out_shape=(sds, sds, sds, sds), name="s5_prep",
    )(expand(a_re), expand(a_im), step, b_re.reshape(rows, cols), b_im.reshape(rows, cols))
    shape4 = (depth, GROUPS, NSTATE, GROUP_CH)
    ab_re = ab_re.reshape(shape4)[..., 0]
    ab_im = ab_im.reshape(shape4)[..., 0]
    return ab_re, ab_im, bb_re.reshape(shape4), bb_im.reshape(shape4)


def _inproj_kernel(x_ref, gpre_ref, wlat_ref, wkr_ref, wwide_ref, gq_ref, gkv_ref,
                   cq_ref, ckv_ref, kr_ref, gm_ref, u_ref, gs_ref):
    x = x_ref[...]
    inv = lax.rsqrt(jnp.mean(x * x, axis=-1, keepdims=True) + NORM_EPS)
    h = (x * inv * gpre_ref[...]).astype(BF16)

    def proj(w_ref, lo, hi):
        return jnp.dot(h, w_ref[:, lo:hi], preferred_element_type=F32)

    def latent_norm(c, g_ref):
        inv_c = lax.rsqrt(jnp.mean(c * c, axis=-1, keepdims=True) + NORM_EPS)
        return (c * inv_c * g_ref[...]).astype(BF16)

    cq_ref[...] = latent_norm(proj(wlat_ref, 0, Q_LORA), gq_ref)
    ckv_ref[...] = latent_norm(proj(wlat_ref, Q_LORA, Q_LORA + KV_LORA), gkv_ref)
    kr_ref[...] = proj(wkr_ref, 0, LANES)
    gm_ref[...] = proj(wwide_ref, 0, D_MLA)
    u_ref[...] = proj(wwide_ref, D_MLA, D_MLA + D_SSM)
    gs_ref[...] = proj(wwide_ref, D_MLA + D_SSM, D_MLA + 2 * D_SSM)


def _inproj(x, g_pre, w_lat, w_kr, w_wide, g_q, g_kv, tm):
    L = x.shape[0]
    row = lambda n: pl.BlockSpec((tm, n), lambda i: (i, 0))
    out_shape = (
        jax.ShapeDtypeStruct((L, Q_LORA), BF16),
        jax.ShapeDtypeStruct((L, KV_LORA), BF16),
        jax.ShapeDtypeStruct((L, LANES), F32),
        jax.ShapeDtypeStruct((L, D_MLA), F32),
        jax.ShapeDtypeStruct((L, D_SSM), F32),
        jax.ShapeDtypeStruct((L, D_SSM), F32),
    )
    return pl.pallas_call(
        _inproj_kernel,
        grid=(L // tm,),
        in_specs=[row(D_MODEL), _resident((1, D_MODEL)), _resident((D_MODEL, Q_LORA + KV_LORA)),
                  _resident((D_MODEL, LANES)), _resident((D_MODEL, D_MLA + 2 * D_SSM)),
                  _resident((1, Q_LORA)), _resident((1, KV_LORA))],
        out_specs=(row(Q_LORA), row(KV_LORA), row(LANES), row(D_MLA), row(D_SSM), row(D_SSM)),
        out_shape=out_shape,
        compiler_params=_params("parallel"),
        name="inproj",
    )(x, g_pre, w_lat, w_kr, w_wide, g_q, g_kv)


def _rope_kernel(pos_ref, freq_ref, cos_ref, sin_ref):
    ang = freq_ref[...] * pos_ref[...].astype(F32)
    cos = jnp.cos(ang)
    sin = jnp.sin(ang)
    cos_ref[...] = jnp.concatenate([cos, cos], axis=0)
    sin_ref[...] = jnp.concatenate([sin, sin], axis=0)


def _rope_tables(positions, tm):
    L = positions.shape[-1]
    inv_freq = ROPE_THETA ** (-jnp.arange(0, ROPE, 2, dtype=F32) / ROPE)
    col = pl.BlockSpec((ROPE, tm), lambda i: (0, i))
    return pl.pallas_call(
        _rope_kernel,
        grid=(L // tm,),
        in_specs=[pl.BlockSpec((1, tm), lambda i: (0, i)), _resident((ROPE // 2, 1))],
        out_specs=(col, col),
        out_shape=(jax.ShapeDtypeStruct((ROPE, L), F32), jax.ShapeDtypeStruct((ROPE, L), F32)),
        compiler_params=_params("parallel"),
        name="rope_tables",
    )(positions.reshape(1, L), inv_freq.reshape(ROPE // 2, 1))


def _qkv_kernel(cq_ref, ckv_ref, kr_ref, cost_ref, sint_ref,
                wqt_ref, wk_ref, wvt_ref, qt_ref, k_ref, vt_ref, *, q_scale):
    nt = (((1,), (1,)), ((), ()))
    cq = cq_ref[...]
    ckv = ckv_ref[...]
    half = ROPE // 2
    cos_t = cost_ref[...]
    sin_t = sint_ref[...]

    kt = kr_ref[...].T[:ROPE]
    kt_rot = jnp.concatenate([-kt[half:], kt[:half]], axis=0)
    kt = kt * cos_t + kt_rot * sin_t
    k_rope = jnp.concatenate([kt, jnp.zeros_like(kt)], axis=0).T[:, :ROPE].astype(BF16)
    k_nope = jnp.dot(ckv, wk_ref[...], preferred_element_type=F32)

    pad_row = lax.broadcasted_iota(jnp.int32, (V_ROWS - DV, vt_ref.shape[-1]), 0)
    ones_rows = jnp.where(pad_row == 0, 1.0, 0.0).astype(BF16)

    qt_all = lax.dot_general(wqt_ref[...], cq, nt, preferred_element_type=F32)
    vt_all = lax.dot_general(wvt_ref[...], ckv, nt, preferred_element_type=F32)
    for h in range(HEADS):
        qt = qt_all[h * QK:(h + 1) * QK]
        qr = qt[NOPE:]
        qr_rot = jnp.concatenate([-qr[half:], qr[:half]], axis=0)
        qr = qr * cos_t + qr_rot * sin_t
        qt_ref[h, :NOPE, :] = (qt[:NOPE] * q_scale).astype(BF16)
        qt_ref[h, NOPE:, :] = (qr * q_scale).astype(BF16)
        k_ref[h, :, :NOPE] = k_nope[:, h * NOPE:(h + 1) * NOPE].astype(BF16)
        k_ref[h, :, NOPE:] = k_rope
        vt_ref[h, 0, :DV, :] = vt_all[h * DV:(h + 1) * DV].astype(BF16)
        vt_ref[h, 0, DV:, :] = ones_rows


def _qkv(cq, ckv, kr, rope, wqt, wk, wvt, tk):
    L = cq.shape[0]
    q_scale = math.log2(math.e) / math.sqrt(QK)
    row = lambda n: pl.BlockSpec((tk, n), lambda i: (i, 0))
    return pl.pallas_call(
        functools.partial(_qkv_kernel, q_scale=q_scale),
        grid=(L // tk,),
        in_specs=[row(Q_LORA), row(KV_LORA), row(LANES),
                  pl.BlockSpec((ROPE, tk), lambda i: (0, i)), pl.BlockSpec((ROPE, tk), lambda i: (0, i)),
                  _resident((HEADS * QK, Q_LORA)), _resident((KV_LORA, HEADS * NOPE)),
                  _resident((HEADS * DV, KV_LORA))],
        out_specs=(pl.BlockSpec((HEADS, QK, tk), lambda i: (0, 0, i)),
                   pl.BlockSpec((HEADS, tk, QK), lambda i: (0, i, 0)),
                   pl.BlockSpec((HEADS, 1, V_ROWS, tk), lambda i: (0, i, 0, 0))),
        out_shape=(jax.ShapeDtypeStruct((HEADS, QK, L), BF16),
                   jax.ShapeDtypeStruct((HEADS, L, QK), BF16),
                   jax.ShapeDtypeStruct((HEADS, L // tk, V_ROWS, tk), BF16)),
        compiler_params=_params("parallel"),
        name="qkv",
    )(cq, ckv, kr, *rope, wqt, wk, wvt)


def _attn_kernel(qt_ref, k_ref, vt_ref, gate_ref, o_ref, acc_ref, m_ref, alpha_ref, s_ref, p_ref, *, tq, tk):
    i = pl.program_id(1)
    m_ref[...] = jnp.full(m_ref.shape, -1e30, F32)
    acc_ref[...] = jnp.zeros(acc_ref.shape, F32)
    assert tq == 2 * tk
    n_chains = tq // MXU_COLS

    def cols(c):
        return slice(MXU_COLS * c, MXU_COLS * (c + 1))

    def n_keys(c, diag):
        return tk if diag is None else max(0, min(tk, MXU_COLS * (c + 1) - diag * tk))

    def scores(j, slot, c, diag):
        rows = n_keys(c, diag)
        if rows == 0:
            return
        kc = k_ref[0, pl.ds(pl.multiple_of(j * tk, tk), rows), :]
        s = jnp.dot(kc, qt_ref[0, :, cols(c)], preferred_element_type=F32)
        if diag is not None and diag * tk + rows - 1 > MXU_COLS * c:
            key = diag * tk + lax.broadcasted_iota(jnp.int32, s.shape, 0)
            qry = MXU_COLS * c + lax.broadcasted_iota(jnp.int32, s.shape, 1)
            s = jnp.where(key <= qry, s, -1e30)
        s_ref[slot, c, :rows, :] = s.astype(BF16)

    def softmax(slot, c, diag):
        rows = n_keys(c, diag)
        if rows == 0:
            return
        s = s_ref[slot, c, :rows, :]
        m_blk = jnp.max(s.reshape(rows // BF16_ROWS, BF16_ROWS, MXU_COLS), axis=0)
        m_blk = jnp.max(m_blk.astype(F32), axis=0, keepdims=True)
        m_prev = m_ref[:, cols(c)]
        m_new = jnp.maximum(m_prev, m_blk)
        alpha_ref[slot, :, cols(c)] = jnp.exp2(m_prev - m_new)
        p_ref[slot, c, :rows, :] = jnp.exp2(s - m_new.astype(BF16))
        m_ref[:, cols(c)] = m_new

    def pv(j, slot, c, diag):
        rows = n_keys(c, diag)
        if rows == 0:
            return
        vc = vt_ref[0, j, :, :rows]
        upd = jnp.dot(vc, p_ref[slot, c, :rows, :], preferred_element_type=F32)
        acc_ref[:, cols(c)] = alpha_ref[slot, :, cols(c)] * acc_ref[:, cols(c)] + upd

    def step(score=None, soft=None, pvs=None):
        for c in range(n_chains):
            if soft is not None:
                softmax(soft[1], c, soft[2])
            if score is not None:
                scores(score[0], score[1], c, score[2])
            if pvs is not None:
                pv(pvs[0], pvs[1], c, pvs[2])

    @pl.when(i > 0)
    def _():
        step(score=(0, 0, None))
        step(score=(1, 1, None), soft=(0, 0, None))

        def pair(t):
            j = 2 * t
            step(score=(j + 2, 0, None), soft=(j + 1, 1, None), pvs=(j, 0, None))
            step(score=(j + 3, 1, None), soft=(j + 2, 0, None), pvs=(j + 1, 1, None))

        def body(t, carry):
            pair(t)
            return carry

        lax.fori_loop(0, i - 1, body, 0)

        j = 2 * i - 2
        step(score=(j + 2, 0, 0), soft=(j + 1, 1, None), pvs=(j, 0, None))
        step(score=(j + 3, 1, 1), soft=(j + 2, 0, 0), pvs=(j + 1, 1, None))

    @pl.when(i == 0)
    def _():
        step(score=(0, 0, 0))
        step(score=(1, 1, 1), soft=(0, 0, 0))

    step(soft=(2 * i + 1, 1, 1), pvs=(2 * i, 0, 0))
    step(pvs=(2 * i + 1, 1, 1))

    out = (acc_ref[:DV, :] * (1.0 / acc_ref[DV:DV + 1, :])).T
    g = gate_ref[...]
    o_ref[...] = (out * (g * _sigmoid(g))).astype(BF16)


def _attention(qt, k, vt, gate, tq, tk):
    L = k.shape[1]
    return pl.pallas_call(
        functools.partial(_attn_kernel, tq=tq, tk=tk),
        grid=(HEADS, L // tq),
        in_specs=[pl.BlockSpec((1, QK, tq), lambda h, i: (h, 0, i)),
                  pl.BlockSpec((1, L, QK), lambda h, i: (h, 0, 0)),
                  pl.BlockSpec((1, L // tk, V_ROWS, tk), lambda h, i: (h, 0, 0, 0)),
                  pl.BlockSpec((tq, DV), lambda h, i: (i, h))],
        out_specs=pl.BlockSpec((tq, DV), lambda h, i: (i, h)),
        out_shape=jax.ShapeDtypeStruct((L, D_MLA), BF16),
        scratch_shapes=[pltpu.VMEM((V_ROWS, tq), F32), pltpu.VMEM((1, tq), F32),
                        pltpu.VMEM((2, 1, tq), F32),
                        pltpu.VMEM((2, tq // MXU_COLS, tk, MXU_COLS), BF16),
                        pltpu.VMEM((2, tq // MXU_COLS, tk, MXU_COLS), BF16)],
        compiler_params=_params("parallel", "arbitrary"),
        name="attention",
    )(qt, k, vt, gate)


def _gelu_tanh(x):
    c = math.sqrt(2.0 / math.pi)
    return 0.5 * x * (1.0 + jnp.tanh(c * (x + 0.044715 * (x * x * x))))


S5_CHUNK = 256
PITCH = 68
N_JOBS = 32
JOBS_PER_SLAB = N_JOBS // N_SLABS
TILES_PER_JOB = 2 * STATE_TILES // N_JOBS


def _job_tiles(slab, kk):
    is_im = kk // (JOBS_PER_SLAB // 2)
    q = kk % (JOBS_PER_SLAB // 2)
    tile0 = is_im * STATE_TILES + SLAB_TILES * slab + TILES_PER_JOB * q
    return 2 * slab + q // 2, tile0


def _s5_kernel(un_ref, up_ref, g_ref, wbj_ref, are_ref, aim_ref, wcj_ref, d_ref, wglu_ref,
               o_ref, hs0_ref, hs1_ref, st_ref, acc_ref, ub_ref, *, tt):
    c = pl.program_id(0)
    hs_refs = (hs0_ref, hs1_ref)

    def col_tile(tile):
        return pl.ds(tile, tt, stride=PITCH)

    def load_u_slabs(u_ref):
        u = u_ref[...]
        for s in range(D_SSM // LANES):
            ub_ref[s] = u[:, LANES * s:LANES * (s + 1)].astype(BF16)

    def b_job(slab, kk, slot):
        s, tile0 = _job_tiles(slab, kk)
        r = jnp.dot(ub_ref[s], wbj_ref[slab * JOBS_PER_SLAB + kk], preferred_element_type=F32)
        for k in range(TILES_PER_JOB):
            hs_refs[slot][col_tile(tile0 + k), :] = r[:, LANES * k:LANES * (k + 1)]

    def c_job(slab, kk, slot):
        _, tile0 = _job_tiles(slab, kk)
        lhs = jnp.concatenate([hs_refs[slot][col_tile(tile0 + k), :] for k in range(TILES_PER_JOB)],
                              axis=-1).astype(BF16)
        return jnp.dot(lhs, wcj_ref[slab * JOBS_PER_SLAB + kk], preferred_element_type=F32)

    @pl.when(c == 0)
    def _():
        st_ref[...] = jnp.zeros(st_ref.shape, F32)
        hs1_ref[...] = jnp.zeros(hs1_ref.shape, F32)
        load_u_slabs(up_ref)
        for ti in range(N_JOBS):
            b_job(ti // JOBS_PER_SLAB, ti % JOBS_PER_SLAB, 0)

    load_u_slabs(un_ref)
    a_re = [are_ref[i] for i in range(N_SLABS)]
    a_im = [aim_ref[i] for i in range(N_SLABS)]
    steps_per_job = tt // N_JOBS

    def run(slot):
        hs_ref = hs_refs[slot]

        def scan_steps(t0, st):
            st = list(st)
            for dt in range(steps_per_job):
                base = (t0 + dt) * PITCH
                for i in range(N_SLABS):
                    re_rows = pl.ds(base + SLAB_TILES * i, SLAB_TILES)
                    im_rows = pl.ds(base + STATE_TILES + SLAB_TILES * i, SLAB_TILES)
                    h_re, h_im = st[i], st[N_SLABS + i]
                    n_re = a_re[i] * h_re - a_im[i] * h_im + hs_ref[re_rows, :]
                    n_im = a_re[i] * h_im + a_im[i] * h_re + hs_ref[im_rows, :]
                    hs_ref[re_rows, :] = n_re
                    hs_ref[im_rows, :] = n_im
                    st[i], st[N_SLABS + i] = n_re, n_im
            return st

        st = [st_ref[i] for i in range(2 * N_SLABS)]
        for slab in range(N_SLABS):
            y = None
            for kk in range(JOBS_PER_SLAB):
                st = scan_steps((slab * JOBS_PER_SLAB + kk) * steps_per_job, st)
                part = c_job(slab, kk, 1 - slot)
                y = part if y is None else y + part
                b_job(slab, kk, 1 - slot)
            acc_ref[slab] = y
        for i in range(2 * N_SLABS):
            st_ref[i] = st[i]

    for slot in range(2):
        pl.when(c % 2 == slot)(functools.partial(run, slot))

    y = jnp.concatenate([acc_ref[i] for i in range(N_SLABS)], axis=-1) + d_ref[...] * up_ref[...]
    y = _gelu_tanh(y).astype(BF16)
    z = jnp.dot(y, wglu_ref[...], preferred_element_type=F32)
    out = z[:, :D_SSM] * _sigmoid(z[:, D_SSM:])
    g = g_ref[...]
    o_ref[...] = (out * (g * _sigmoid(g))).astype(BF16)


def _s5(u, gate, wbj, a_re, a_im, wcj, d_skip, w_glu, tt):
    L = u.shape[0]
    n = L // tt
    nxt = pl.BlockSpec((tt, D_SSM), lambda c: (jnp.minimum(c + 1, n - 1), 0))
    prv = pl.BlockSpec((tt, D_SSM), lambda c: (jnp.maximum(c - 1, 0), 0))
    job_cols = TILES_PER_JOB * LANES
    return pl.pallas_call(
        functools.partial(_s5_kernel, tt=tt),
        grid=(n + 1,),
        in_specs=[nxt, prv, prv,
                  _resident((N_JOBS, LANES, job_cols)),
                  _resident((N_SLABS, SUBLANES, LANES)), _resident((N_SLABS, SUBLANES, LANES)),
                  _resident((N_JOBS, job_cols, D_SSM // N_SLABS)),
                  _resident((1, D_SSM)), _resident((D_SSM, 2 * D_SSM))],
        out_specs=prv,
        out_shape=jax.ShapeDtypeStruct((L, D_SSM), BF16),
        scratch_shapes=[pltpu.VMEM((tt * PITCH, LANES), F32), pltpu.VMEM((tt * PITCH, LANES), F32),
                        pltpu.VMEM((2 * N_SLABS, SUBLANES, LANES), F32),
                        pltpu.VMEM((N_SLABS, tt, D_SSM // N_SLABS), F32),
                        pltpu.VMEM((D_SSM // LANES, tt, LANES), BF16)],
        compiler_params=_params("arbitrary"),
        name="s5",
    )(u, u, gate, wbj, a_re, a_im, wcj, d_skip, w_glu)


def _outproj_kernel(mla_ref, ssm_ref, w_ref, gpost_ref, x_ref, o_ref):
    out = jnp.dot(mla_ref[...], w_ref[:D_MLA, :], preferred_element_type=F32)
    out += jnp.dot(ssm_ref[...], w_ref[D_MLA:, :], preferred_element_type=F32)
    inv = lax.rsqrt(jnp.mean(out * out, axis=-1, keepdims=True) + NORM_EPS)
    o_ref[...] = x_ref[...] + out * inv * gpost_ref[...]


def _outproj(mla, ssm, w_out, g_post, x, tm):
    L = x.shape[0]
    row = lambda n: pl.BlockSpec((tm, n), lambda i: (i, 0))
    return pl.pallas_call(
        _outproj_kernel,
        grid=(L // tm,),
        in_specs=[row(D_MLA), row(D_SSM), _resident((D_MLA + D_SSM, D_MODEL)),
                  _resident((1, D_MODEL)), row(D_MODEL)],
        out_specs=row(D_MODEL),
        out_shape=jax.ShapeDtypeStruct((L, D_MODEL), F32),
        compiler_params=_params("parallel"),
        name="outproj",
    )(mla, ssm, w_out, g_post, x)


def _split_in_weights(w_in):
    s2 = Q_LORA + KV_LORA
    s3 = s2 + ROPE
    w = lax.optimization_barrier(w_in.astype(BF16))
    w_kr = jnp.pad(w[..., s2:s3], ((0, 0), (0, 0), (0, LANES - ROPE)))
    return w[..., :s2], w_kr, w[..., s3:]


def _pack_b(bb_re, bb_im):
    depth = bb_re.shape[0]
    gpt = LANES // GROUP_CH
    eye = jnp.eye(gpt, dtype=F32)

    def pack(b):
        b = b.reshape(depth, GROUPS // gpt, gpt, NSTATE, GROUP_CH)
        return jnp.einsum('dsgph,gk->dsghkp', b, eye).reshape(depth, GROUPS // gpt, LANES, gpt * NSTATE)

    return jnp.concatenate([pack(bb_re), pack(bb_im)], axis=-1).astype(BF16)


def _pack_c(c):
    depth = c.shape[0]
    gps = SLAB_TILES * LANES // NSTATE
    eye = jnp.eye(gps, dtype=F32)
    c = c.reshape(depth, N_SLABS, gps, GROUP_CH, NSTATE)
    return jnp.einsum('dighp,gk->digpkh', c, eye).reshape(
        depth, N_SLABS, gps * NSTATE, gps * GROUP_CH).astype(BF16)


def kernel(x, positions, norm_pre, norm_post, w_in, q_norm, w_uq, kv_norm, w_ukv, ssm_a_re, ssm_a_im,
           ssm_b_re, ssm_b_im, ssm_c_re, ssm_c_im, ssm_d, ssm_log_step, w_glu, w_out):
    batch, L, _ = x.shape
    assert batch == 1
    depth = w_in.shape[0]
    tm = min(512, L)
    tq = min(2048, L)
    tk = tq // 2
    tt = S5_CHUNK

    rope = _rope_tables(positions, tk)

    w_lat, w_kr, w_wide = _split_in_weights(w_in)
    wqt = w_uq.astype(BF16).transpose(0, 2, 1)
    w_ukv4 = w_ukv.astype(BF16).reshape(depth, KV_LORA, HEADS, NOPE + DV)
    wk = w_ukv4[..., :NOPE].reshape(depth, KV_LORA, HEADS * NOPE)
    wvt = w_ukv4[..., NOPE:].transpose(0, 2, 3, 1).reshape(depth, HEADS * DV, KV_LORA)
    w_glu_b = w_glu.astype(BF16)
    w_out_b = w_out.astype(BF16)

    ab_re, ab_im, bb_re, bb_im = _s5_prep(ssm_a_re, ssm_a_im, ssm_log_step, ssm_b_re, ssm_b_im)
    wb = _pack_b(bb_re, bb_im)
    half = wb.shape[-1] // 2
    job_cols = TILES_PER_JOB * LANES
    wbj = []
    for ti in range(N_JOBS):
        s, tile0 = _job_tiles(ti // JOBS_PER_SLAB, ti % JOBS_PER_SLAB)
        col = (tile0 // STATE_TILES) * half + (tile0 % STATE_TILES) * LANES - s * half
        wbj.append(wb[:, s, :, col:col + job_cols])
    wbj = jnp.stack(wbj, axis=1)
    slab_shape = (depth, N_SLABS, SUBLANES, LANES)
    a_re_s = ab_re.reshape(slab_shape)
    a_im_s = ab_im.reshape(slab_shape)
    wc_re = _pack_c(ssm_c_re).reshape(depth, N_SLABS, JOBS_PER_SLAB // 2, job_cols, -1)
    wc_im = _pack_c(-ssm_c_im).reshape(depth, N_SLABS, JOBS_PER_SLAB // 2, job_cols, -1)
    wcj = jnp.concatenate([wc_re, wc_im], axis=2).reshape(depth, N_JOBS, job_cols, -1)

    h = x[0]
    for i in range(depth):
        cq, ckv, kr, gate_mla, u, gate_ssm = _inproj(
            h, norm_pre[i][None], w_lat[i], w_kr[i], w_wide[i], q_norm[i][None], kv_norm[i][None], tm)
        qt, k, vt = _qkv(cq, ckv, kr, rope, wqt[i], wk[i], wvt[i], tk)
        mla = _attention(qt, k, vt, gate_mla, tq, tk)
        ssm = _s5(u, gate_ssm, wbj[i], a_re_s[i], a_im_s[i], wcj[i], ssm_d[i][None], w_glu_b[i], tt)
        h = _outproj(mla, ssm, w_out_b[i], norm_post[i][None], h, tm)
    return h[None]
```

```python
import functools
import math

import jax
import jax.numpy as jnp
from jax import lax
from jax.experimental import pallas as pl
from jax.experimental.pallas import tpu as pltpu

F32 = jnp.float32
BF16 = jnp.bfloat16

D_MODEL = 2048
HEADS = 8
NOPE = 128
ROPE = 64
QK = NOPE + ROPE
DV = 128
V_ROWS = DV + 16
Q_LORA = 512
KV_LORA = 256
D_MLA = HEADS * DV
D_SSM = 1024
GROUP_CH = 16
GROUPS = D_SSM // GROUP_CH
NSTATE = 64
ROPE_THETA = 10000.0
NORM_EPS = 1e-6
LANES = 128
SUBLANES = 8
MXU_COLS = 256
BF16_ROWS = 16
STATE_TILES = GROUPS * NSTATE // LANES
SLAB_TILES = SUBLANES
N_SLABS = STATE_TILES // SLAB_TILES
VMEM_LIMIT = 56 * 1024 * 1024


def _resident(shape):
    zeros = (0,) * len(shape)
    return pl.BlockSpec(shape, lambda *_: zeros, pipeline_mode=pl.Buffered(1))


def _sigmoid(x):
    return 1.0 / (1.0 + jnp.exp(-x))


def _params(*sem):
    return pltpu.CompilerParams(dimension_semantics=sem, vmem_limit_bytes=VMEM_LIMIT)


def _s5_prep_kernel(are_ref, aim_ref, lstep_ref, bre_ref, bim_ref,
                    abre_ref, abim_ref, bbre_ref, bbim_ref):
    a_re = are_ref[...]
    a_im = aim_ref[...]
    step = jnp.exp(lstep_ref[...])
    mag = jnp.exp(step * a_re)
    ab_re = mag * jnp.cos(step * a_im)
    ab_im = mag * jnp.sin(step * a_im)
    n_re = ab_re - 1.0
    den = a_re * a_re + a_im * a_im
    z_re = (n_re * a_re + ab_im * a_im) / den
    z_im = (ab_im * a_re - n_re * a_im) / den
    b_re = bre_ref[...]
    b_im = bim_ref[...]
    abre_ref[...] = ab_re
    abim_ref[...] = ab_im
    bbre_ref[...] = z_re * b_re - z_im * b_im
    bbim_ref[...] = z_re * b_im + z_im * b_re


def _s5_prep(a_re, a_im, log_step, b_re, b_im):
    depth = a_re.shape[0]
    rows = depth * GROUPS
    cols = NSTATE * GROUP_CH

    def expand(a):
        return jnp.broadcast_to(a[..., None], (depth, GROUPS, NSTATE, GROUP_CH)).reshape(rows, cols)

    step = jnp.broadcast_to(log_step[..., None], (depth, GROUPS, cols)).reshape(rows, cols)
    sds = jax.ShapeDtypeStruct((rows, cols), F32)
    ab_re, ab_im, bb_re, bb_im = pl.pallas_call(
        _s5_prep_kernel, out_shape=(sds, sds, sds, sds), name="s5_prep",
    )(expand(a_re), expand(a_im), step, b_re.reshape(rows, cols), b_im.reshape(rows, cols))
    shape4 = (depth, GROUPS, NSTATE, GROUP_CH)
    ab_re = ab_re.reshape(shape4)[..., 0]
    ab_im = ab_im.reshape(shape4)[..., 0]
    return ab_re, ab_im, bb_re.reshape(shape4), bb_im.reshape(shape4)


def _inproj_kernel(x_ref, gpre_ref, wlat_ref, wkr_ref, wwide_ref, gq_ref, gkv_ref,
                   cq_ref, ckv_ref, kr_ref, gm_ref, u_ref, gs_ref):
    x = x_ref[...]
    inv = lax.rsqrt(jnp.mean(x * x, axis=-1, keepdims=True) + NORM_EPS)
    h = (x * inv * gpre_ref[...]).astype(BF16)

    def proj(w_ref, lo, hi):
        return jnp.dot(h, w_ref[:, lo:hi], preferred_element_type=F32)

    def latent_norm(c, g_ref):
        inv_c = lax.rsqrt(jnp.mean(c * c, axis=-1, keepdims=True) + NORM_EPS)
        return (c * inv_c * g_ref[...]).astype(BF16)

    cq_ref[...] = latent_norm(proj(wlat_ref, 0, Q_LORA), gq_ref)
    ckv_ref[...] = latent_norm(proj(wlat_ref, Q_LORA, Q_LORA + KV_LORA), gkv_ref)
    kr_ref[...] = proj(wkr_ref, 0, LANES)
    gm_ref[...] = proj(wwide_ref, 0, D_MLA)
    u_ref[...] = proj(wwide_ref, D_MLA, D_MLA + D_SSM)
    gs_ref[...] = proj(wwide_ref, D_MLA + D_SSM, D_MLA + 2 * D_SSM)


def _inproj(x, g_pre, w_lat, w_kr, w_wide, g_q, g_kv, tm):
    L = x.shape[0]
    row = lambda n: pl.BlockSpec((tm, n), lambda i: (i, 0))
    out_shape = (
        jax.ShapeDtypeStruct((L, Q_LORA), BF16),
        jax.ShapeDtypeStruct((L, KV_LORA), BF16),
        jax.ShapeDtypeStruct((L, LANES), F32),
        jax.ShapeDtypeStruct((L, D_MLA), F32),
        jax.ShapeDtypeStruct((L, D_SSM), F32),
        jax.ShapeDtypeStruct((L, D_SSM), F32),
    )
    return pl.pallas_call(
        _inproj_kernel,
        grid=(L // tm,),
        in_specs=[row(D_MODEL), _resident((1, D_MODEL)), _resident((D_MODEL, Q_LORA + KV_LORA)),
                  _resident((D_MODEL, LANES)), _resident((D_MODEL, D_MLA + 2 * D_SSM)),
                  _resident((1, Q_LORA)), _resident((1, KV_LORA))],
        out_specs=(row(Q_LORA), row(KV_LORA), row(LANES), row(D_MLA), row(D_SSM), row(D_SSM)),
        out_shape=out_shape,
        compiler_params=_params("parallel"),
        name="inproj",
    )(x, g_pre, w_lat, w_kr, w_wide, g_q, g_kv)


def _rope_kernel(pos_ref, freq_ref, cos_ref, sin_ref):
    ang = freq_ref[...] * pos_ref[...].astype(F32)
    cos = jnp.cos(ang)
    sin = jnp.sin(ang)
    cos_ref[...] = jnp.concatenate([cos, cos], axis=0)
    sin_ref[...] = jnp.concatenate([sin, sin], axis=0)


def _rope_tables(positions, tm):
    L = positions.shape[-1]
    inv_freq = ROPE_THETA ** (-jnp.arange(0, ROPE, 2, dtype=F32) / ROPE)
    col = pl.BlockSpec((ROPE, tm), lambda i: (0, i))
    return pl.pallas_call(
        _rope_kernel,
        grid=(L // tm,),
        in_specs=[pl.BlockSpec((1, tm), lambda i: (0, i)), _resident((ROPE // 2, 1))],
        out_specs=(col, col),
        out_shape=(jax.ShapeDtypeStruct((ROPE, L), F32), jax.ShapeDtypeStruct((ROPE, L), F32)),
        compiler_params=_params("parallel"),
        name="rope_tables",
    )(positions.reshape(1, L), inv_freq.reshape(ROPE // 2, 1))


def _qkv_kernel(cq_ref, ckv_ref, kr_ref, cost_ref, sint_ref,
                wqt_ref, wk_ref, wvt_ref, qt_ref, k_ref, vt_ref, *, q_scale):
    nt = (((1,), (1,)), ((), ()))
    cq = cq_ref[...]
    ckv = ckv_ref[...]
    half = ROPE // 2
    cos_t = cost_ref[...]
    sin_t = sint_ref[...]

    kt = kr_ref[...].T[:ROPE]
    kt_rot = jnp.concatenate([-kt[half:], kt[:half]], axis=0)
    kt = kt * cos_t + kt_rot * sin_t
    k_rope = jnp.concatenate([kt, jnp.zeros_like(kt)], axis=0).T[:, :ROPE].astype(BF16)
    k_nope = jnp.dot(ckv, wk_ref[...], preferred_element_type=F32)

    pad_row = lax.broadcasted_iota(jnp.int32, (V_ROWS - DV, vt_ref.shape[-1]), 0)
    ones_rows = jnp.where(pad_row == 0, 1.0, 0.0).astype(BF16)

    qt_all = lax.dot_general(wqt_ref[...], cq, nt, preferred_element_type=F32)
    vt_all = lax.dot_general(wvt_ref[...], ckv, nt, preferred_element_type=F32)
    for h in range(HEADS):
        qt = qt_all[h * QK:(h + 1) * QK]
        qr = qt[NOPE:]
        qr_rot = jnp.concatenate([-qr[half:], qr[:half]], axis=0)
        qr = qr * cos_t + qr_rot * sin_t
        qt_ref[h, :NOPE, :] = (qt[:NOPE] * q_scale).astype(BF16)
        qt_ref[h, NOPE:, :] = (qr * q_scale).astype(BF16)
        k_ref[h, :, :NOPE] = k_nope[:, h * NOPE:(h + 1) * NOPE].astype(BF16)
        k_ref[h, :, NOPE:] = k_rope
        vt_ref[h, 0, :DV, :] = vt_all[h * DV:(h + 1) * DV].astype(BF16)
        vt_ref[h, 0, DV:, :] = ones_rows


def _qkv(cq, ckv, kr, rope, wqt, wk, wvt, tk):
    L = cq.shape[0]
    q_scale = math.log2(math.e) / math.sqrt(QK)
    row = lambda n: pl.BlockSpec((tk, n), lambda i: (i, 0))
    return pl.pallas_call(
        functools.partial(_qkv_kernel, q_scale=q_scale),
        grid=(L // tk,),
        in_specs=[row(Q_LORA), row(KV_LORA), row(LANES),
                  pl.BlockSpec((ROPE, tk), lambda i: (0, i)), pl.BlockSpec((ROPE, tk), lambda i: (0, i)),
                  _resident((HEADS * QK, Q_LORA)), _resident((KV_LORA, HEADS * NOPE)),
                  _resident((HEADS * DV, KV_LORA))],
        out_specs=(pl.BlockSpec((HEADS, QK, tk), lambda i: (0, 0, i)),
                   pl.BlockSpec((HEADS, tk, QK), lambda i: (0, i, 0)),
                   pl.BlockSpec((HEADS, 1, V_ROWS, tk), lambda i: (0, i, 0, 0))),
        out_shape=(jax.ShapeDtypeStruct((HEADS, QK, L), BF16),
                   jax.ShapeDtypeStruct((HEADS, L, QK), BF16),
                   jax.ShapeDtypeStruct((HEADS, L // tk, V_ROWS, tk), BF16)),
        compiler_params=_params("parallel"),
        name="qkv",
    )(cq, ckv, kr, *rope, wqt, wk, wvt)


def _attn_kernel(qt_ref, k_ref, vt_ref, gate_ref, o_ref, acc_ref, m_ref, alpha_ref, s_ref, p_ref, *, tq, tk):
    i = pl.program_id(1)
    m_ref[...] = jnp.full(m_ref.shape, -1e30, F32)
    acc_ref[...] = jnp.zeros(acc_ref.shape, F32)
    assert tq == 2 * tk
    n_chains = tq // MXU_COLS

    def cols(c):
        return slice(MXU_COLS * c, MXU_COLS * (c + 1))

    def n_keys(c, diag):
        return tk if diag is None else max(0, min(tk, MXU_COLS * (c + 1) - diag * tk))

    def scores(j, slot, c, diag):
        rows = n_keys(c, diag)
        if rows == 0:
            return
        kc = k_ref[0, pl.ds(pl.multiple_of(j * tk, tk), rows), :]
        s = jnp.dot(kc, qt_ref[0, :, cols(c)], preferred_element_type=F32)
        if diag is not None and diag * tk + rows - 1 > MXU_COLS * c:
            key = diag * tk + lax.broadcasted_iota(jnp.int32, s.shape, 0)
            qry = MXU_COLS * c + lax.broadcasted_iota(jnp.int32, s.shape, 1)
            s = jnp.where(key <= qry, s, -1e30)
        s_ref[slot, c, :rows, :] = s.astype(BF16)

    def softmax(slot, c, diag):
        rows = n_keys(c, diag)
        if rows == 0:
            return
        s = s_ref[slot, c, :rows, :]
        m_blk = jnp.max(s.reshape(rows // BF16_ROWS, BF16_ROWS, MXU_COLS), axis=0)
        m_blk = jnp.max(m_blk.astype(F32), axis=0, keepdims=True)
        m_prev = m_ref[:, cols(c)]
        m_new = jnp.maximum(m_prev, m_blk)
        alpha_ref[slot, :, cols(c)] = jnp.exp2(m_prev - m_new)
        p_ref[slot, c, :rows, :] = jnp.exp2(s - m_new.astype(BF16))
        m_ref[:, cols(c)] = m_new

    def pv(j, slot, c, diag):
        rows = n_keys(c, diag)
        if rows == 0:
            return
        vc = vt_ref[0, j, :, :rows]
        upd = jnp.dot(vc, p_ref[slot, c, :rows, :], preferred_element_type=F32)
        acc_ref[:, cols(c)] = alpha_ref[slot, :, cols(c)] * acc_ref[:, cols(c)] + upd

    def step(score=None, soft=None, pvs=None):
        for c in range(n_chains):
            if soft is not None:
                softmax(soft[1], c, soft[2])
            if score is not None:
                scores(score[0], score[1], c, score[2])
            if pvs is not None:
                pv(pvs[0], pvs[1], c, pvs[2])

    @pl.when(i > 0)
    def _():
        step(score=(0, 0, None))
        step(score=(1, 1, None), soft=(0, 0, None))

        def pair(t):
            j = 2 * t
            step(score=(j + 2, 0, None), soft=(j + 1, 1, None), pvs=(j, 0, None))
            step(score=(j + 3, 1, None), soft=(j + 2, 0, None), pvs=(j + 1, 1, None))

        def body(t, carry):
            pair(t)
            return carry

        lax.fori_loop(0, i - 1, body, 0)

        j = 2 * i - 2
        step(score=(j + 2, 0, 0), soft=(j + 1, 1, None), pvs=(j, 0, None))
        step(score=(j + 3, 1, 1), soft=(j + 2, 0, 0), pvs=(j + 1, 1, None))

    @pl.when(i == 0)
    def _():
        step(score=(0, 0, 0))
        step(score=(1, 1, 1), soft=(0, 0, 0))

    step(soft=(2 * i + 1, 1, 1), pvs=(2 * i, 0, 0))
    step(pvs=(2 * i + 1, 1, 1))

    out = (acc_ref[:DV, :] * (1.0 / acc_ref[DV:DV + 1, :])).T
    g = gate_ref[...]
    o_ref[...] = (out * (g * _sigmoid(g))).astype(BF16)


def _attention(qt, k, vt, gate, tq, tk):
    L = k.shape[1]
    return pl.pallas_call(
        functools.partial(_attn_kernel, tq=tq, tk=tk),
        grid=(HEADS, L // tq),
        in_specs=[pl.BlockSpec((1, QK, tq), lambda h, i: (h, 0, i)),
                  pl.BlockSpec((1, L, QK), lambda h, i: (h, 0, 0)),
                  pl.BlockSpec((1, L // tk, V_ROWS, tk), lambda h, i: (h, 0, 0, 0)),
                  pl.BlockSpec((tq, DV), lambda h, i: (i, h))],
        out_specs=pl.BlockSpec((tq, DV), lambda h, i: (i, h)),
        out_shape=jax.ShapeDtypeStruct((L, D_MLA), BF16),
        scratch_shapes=[pltpu.VMEM((V_ROWS, tq), F32), pltpu.VMEM((1, tq), F32),
                        pltpu.VMEM((2, 1, tq), F32),
                        pltpu.VMEM((2, tq // MXU_COLS, tk, MXU_COLS), BF16),
                        pltpu.VMEM((2, tq // MXU_COLS, tk, MXU_COLS), BF16)],
        compiler_params=_params("parallel", "arbitrary"),
        name="attention",
    )(qt, k, vt, gate)


def _gelu_tanh(x):
    c = math.sqrt(2.0 / math.pi)
    return 0.5 * x * (1.0 + jnp.tanh(c * (x + 0.044715 * (x * x * x))))


S5_CHUNK = 256
PITCH = 68
N_JOBS = 32
JOBS_PER_SLAB = N_JOBS // N_SLABS
TILES_PER_JOB = 2 * STATE_TILES // N_JOBS


def _job_tiles(slab, kk):
    is_im = kk // (JOBS_PER_SLAB // 2)
    q = kk % (JOBS_PER_SLAB // 2)
    tile0 = is_im * STATE_TILES + SLAB_TILES * slab + TILES_PER_JOB * q
    return 2 * slab + q // 2, tile0


def _s5_kernel(un_ref, up_ref, g_ref, wbj_ref, are_ref, aim_ref, wcj_ref, d_ref, wglu_ref,
               o_ref, hs0_ref, hs1_ref, st_ref, acc_ref, ub_ref, *, tt):
    c = pl.program_id(0)
    hs_refs = (hs0_ref, hs1_ref)

    def col_tile(tile):
        return pl.ds(tile, tt, stride=PITCH)

    def load_u_slabs(u_ref):
        u = u_ref[...]
        for s in range(D_SSM // LANES):
            ub_ref[s] = u[:, LANES * s:LANES * (s + 1)].astype(BF16)

    def b_job(slab, kk, slot):
        s, tile0 = _job_tiles(slab, kk)
        r = jnp.dot(ub_ref[s], wbj_ref[slab * JOBS_PER_SLAB + kk], preferred_element_type=F32)
        for k in range(TILES_PER_JOB):
            hs_refs[slot][col_tile(tile0 + k), :] = r[:, LANES * k:LANES * (k + 1)]

    def c_job(slab, kk, slot):
        _, tile0 = _job_tiles(slab, kk)
        lhs = jnp.concatenate([hs_refs[slot][col_tile(tile0 + k), :] for k in range(TILES_PER_JOB)],
                              axis=-1).astype(BF16)
        return jnp.dot(lhs, wcj_ref[slab * JOBS_PER_SLAB + kk], preferred_element_type=F32)

    @pl.when(c == 0)
    def _():
        st_ref[...] = jnp.zeros(st_ref.shape, F32)
        hs1_ref[...] = jnp.zeros(hs1_ref.shape, F32)
        load_u_slabs(up_ref)
        for ti in range(N_JOBS):
            b_job(ti // JOBS_PER_SLAB, ti % JOBS_PER_SLAB, 0)

    load_u_slabs(un_ref)
    a_re = [are_ref[i] for i in range(N_SLABS)]
    a_im = [aim_ref[i] for i in range(N_SLABS)]
    steps_per_job = tt // N_JOBS

    def run(slot):
        hs_ref = hs_refs[slot]

        def scan_steps(t0, st):
            st = list(st)
            for dt in range(steps_per_job):
                base = (t0 + dt) * PITCH
                for i in range(N_SLABS):
                    re_rows = pl.ds(base + SLAB_TILES * i, SLAB_TILES)
                    im_rows = pl.ds(base + STATE_TILES + SLAB_TILES * i, SLAB_TILES)
                    h_re, h_im = st[i], st[N_SLABS + i]
                    n_re = a_re[i] * h_re - a_im[i] * h_im + hs_ref[re_rows, :]
                    n_im = a_re[i] * h_im + a_im[i] * h_re + hs_ref[im_rows, :]
                    hs_ref[re_rows, :] = n_re
                    hs_ref[im_rows, :] = n_im
                    st[i], st[N_SLABS + i] = n_re, n_im
            return st

        st = [st_ref[i] for i in range(2 * N_SLABS)]
        for slab in range(N_SLABS):
            y = None
            for kk in range(JOBS_PER_SLAB):
                st = scan_steps((slab * JOBS_PER_SLAB + kk) * steps_per_job, st)
                part = c_job(slab, kk, 1 - slot)
                y = part if y is None else y + part
                b_job(slab, kk, 1 - slot)
            acc_ref[slab] = y
        for i in range(2 * N_SLABS):
            st_ref[i] = st[i]

    for slot in range(2):
        pl.when(c % 2 == slot)(functools.partial(run, slot))

    y = jnp.concatenate([acc_ref[i] for i in range(N_SLABS)], axis=-1) + d_ref[...] * up_ref[...]
    y = _gelu_tanh(y).astype(BF16)
    tile = D_SSM // N_SLABS
    for n in range(N_SLABS):
        val_cols = slice(tile * n, tile * (n + 1))
        gate_cols = slice(D_SSM + tile * n, D_SSM + tile * (n + 1))
        val = jnp.dot(y, wglu_ref[:, val_cols], preferred_element_type=F32)
        gate = jnp.dot(y, wglu_ref[:, gate_cols], preferred_element_type=F32)
        g = g_ref[:, val_cols]
        o_ref[:, val_cols] = (val * _sigmoid(gate) * (g * _sigmoid(g))).astype(BF16)


def _s5(u, gate, wbj, a_re, a_im, wcj, d_skip, w_glu, tt):
    L = u.shape[0]
    n = L // tt
    nxt = pl.BlockSpec((tt, D_SSM), lambda c: (jnp.minimum(c + 1, n - 1), 0))
    prv = pl.BlockSpec((tt, D_SSM), lambda c: (jnp.maximum(c - 1, 0), 0))
    job_cols = TILES_PER_JOB * LANES
    return pl.pallas_call(
        functools.partial(_s5_kernel, tt=tt),
        grid=(n + 1,),
        in_specs=[nxt, prv, prv,
                  _resident((N_JOBS, LANES, job_cols)),
                  _resident((N_SLABS, SUBLANES, LANES)), _resident((N_SLABS, SUBLANES, LANES)),
                  _resident((N_JOBS, job_cols, D_SSM // N_SLABS)),
                  _resident((1, D_SSM)), _resident((D_SSM, 2 * D_SSM))],
        out_specs=prv,
        out_shape=jax.ShapeDtypeStruct((L, D_SSM), BF16),
        scratch_shapes=[pltpu.VMEM((tt * PITCH, LANES), F32), pltpu.VMEM((tt * PITCH, LANES), F32),
                        pltpu.VMEM((2 * N_SLABS, SUBLANES, LANES), F32),
                        pltpu.VMEM((N_SLABS, tt, D_SSM // N_SLABS), F32),
                        pltpu.VMEM((D_SSM // LANES, tt, LANES), BF16)],
        compiler_params=_params("arbitrary"),
        name="s5",
    )(u, u, gate, wbj, a_re, a_im, wcj, d_skip, w_glu)


def _outproj_kernel(mla_ref, ssm_ref, w_ref, gpost_ref, x_ref, o_ref):
    out = jnp.dot(mla_ref[...], w_ref[:D_MLA, :], preferred_element_type=F32)
    out += jnp.dot(ssm_ref[...], w_ref[D_MLA:, :], preferred_element_type=F32)
    inv = lax.rsqrt(jnp.mean(out * out, axis=-1, keepdims=True) + NORM_EPS)
    o_ref[...] = x_ref[...] + out * inv * gpost_ref[...]


def _outproj(mla, ssm, w_out, g_post, x, tm):
    L = x.shape[0]
    row = lambda n: pl.BlockSpec((tm, n), lambda i: (i, 0))
    return pl.pallas_call(
        _outproj_kernel,
        grid=(L // tm,),
        in_specs=[row(D_MLA), row(D_SSM), _resident((D_MLA + D_SSM, D_MODEL)),
                  _resident((1, D_MODEL)), row(D_MODEL)],
        out_specs=row(D_MODEL),
        out_shape=jax.ShapeDtypeStruct((L, D_MODEL), F32),
        compiler_params=_params("parallel"),
        name="outproj",
    )(mla, ssm, w_out, g_post, x)


def _split_in_weights(w_in):
    s2 = Q_LORA + KV_LORA
    s3 = s2 + ROPE
    w = lax.optimization_barrier(w_in.astype(BF16))
    w_kr = jnp.pad(w[..., s2:s3], ((0, 0), (0, 0), (0, LANES - ROPE)))
    return w[..., :s2], w_kr, w[..., s3:]


def _pack_b(bb_re, bb_im):
    depth = bb_re.shape[0]
    gpt = LANES // GROUP_CH
    same_group = jnp.eye(gpt, dtype=bool)[None, None, :, None, :, None]

    def pack(b):
        b = b.astype(BF16).reshape(depth, GROUPS // gpt, gpt, NSTATE, GROUP_CH).transpose(0, 1, 2, 4, 3)
        blocks = jnp.where(same_group, b[:, :, :, :, None, :], 0)
        return blocks.reshape(depth, GROUPS // gpt, LANES, gpt * NSTATE)

    return jnp.concatenate([pack(bb_re), pack(bb_im)], axis=-1)


def _pack_c(c):
    depth = c.shape[0]
    gps = SLAB_TILES * LANES // NSTATE
    same_group = jnp.eye(gps, dtype=bool)[None, None, :, None, :, None]
    c = c.astype(BF16).reshape(depth, N_SLABS, gps, GROUP_CH, NSTATE).transpose(0, 1, 2, 4, 3)
    blocks = jnp.where(same_group, c[:, :, :, :, None, :], 0)
    return blocks.reshape(depth, N_SLABS, gps * NSTATE, gps * GROUP_CH)


def kernel(x, positions, norm_pre, norm_post, w_in, q_norm, w_uq, kv_norm, w_ukv, ssm_a_re, ssm_a_im,
           ssm_b_re, ssm_b_im, ssm_c_re, ssm_c_im, ssm_d, ssm_log_step, w_glu, w_out):
    batch, L, _ = x.shape
    assert batch == 1
    depth = w_in.shape[0]
    tm = min(512, L)
    tq = min(2048, L)
    tk = tq // 2
    tt = S5_CHUNK

    rope = _rope_tables(positions, tk)

    w_lat, w_kr, w_wide = _split_in_weights(w_in)
    wqt = w_uq.astype(BF16).transpose(0, 2, 1)
    w_ukv4 = w_ukv.astype(BF16).reshape(depth, KV_LORA, HEADS, NOPE + DV)
    wk = w_ukv4[..., :NOPE].reshape(depth, KV_LORA, HEADS * NOPE)
    wvt = w_ukv4[..., NOPE:].transpose(0, 2, 3, 1).reshape(depth, HEADS * DV, KV_LORA)
    w_glu_b = w_glu.astype(BF16)
    w_out_b = w_out.astype(BF16)

    ab_re, ab_im, bb_re, bb_im = _s5_prep(ssm_a_re, ssm_a_im, ssm_log_step, ssm_b_re, ssm_b_im)
    wb = _pack_b(bb_re, bb_im)
    half = wb.shape[-1] // 2
    job_cols = TILES_PER_JOB * LANES
    wbj = []
    for ti in range(N_JOBS):
        s, tile0 = _job_tiles(ti // JOBS_PER_SLAB, ti % JOBS_PER_SLAB)
        col = (tile0 // STATE_TILES) * half + (tile0 % STATE_TILES) * LANES - s * half
        wbj.append(wb[:, s, :, col:col + job_cols])
    wbj = jnp.stack(wbj, axis=1)
    slab_shape = (depth, N_SLABS, SUBLANES, LANES)
    a_re_s = ab_re.reshape(slab_shape)
    a_im_s = ab_im.reshape(slab_shape)
    wc_re = _pack_c(ssm_c_re).reshape(depth, N_SLABS, JOBS_PER_SLAB // 2, job_cols, -1)
    wc_im = _pack_c(-ssm_c_im).reshape(depth, N_SLABS, JOBS_PER_SLAB // 2, job_cols, -1)
    wcj = jnp.concatenate([wc_re, wc_im], axis=2).reshape(depth, N_JOBS, job_cols, -1)

    h = x[0]
    for i in range(depth):
        cq, ckv, kr, gate_mla, u, gate_ssm = _inproj(
            h, norm_pre[i][None], w_lat[i], w_kr[i], w_wide[i], q_norm[i][None], kv_norm[i][None], tm)
        qt, k, vt = _qkv(cq, ckv, kr, rope, wqt[i], wk[i], wvt[i], tk)
        mla = _attention(qt, k, vt, gate_mla, tq, tk)
        ssm = _s5(u, gate_ssm, wbj[i], a_re_s[i], a_im_s[i], wcj[i], ssm_d[i][None], w_glu_b[i], tt)
        h = _outproj(mla, ssm, w_out_b[i], norm_post[i][None], h, tm)
    return h[None]
```

```python
import functools
import math

import jax
import jax.numpy as jnp
from jax import lax
from jax.experimental import pallas as pl
from jax.experimental.pallas import tpu as pltpu

F32 = jnp.float32
BF16 = jnp.bfloat16

D_MODEL = 2048
HEADS = 8
NOPE = 128
ROPE = 64
QK = NOPE + ROPE
DV = 128
V_ROWS = DV + 16
Q_LORA = 512
KV_LORA = 256
D_MLA = HEADS * DV
D_SSM = 1024
GROUP_CH = 16
GROUPS = D_SSM // GROUP_CH
NSTATE = 64
ROPE_THETA = 10000.0
NORM_EPS = 1e-6
LANES = 128
SUBLANES = 8
MXU_COLS = 256
BF16_ROWS = 16
STATE_TILES = GROUPS * NSTATE // LANES
SLAB_TILES = SUBLANES
N_SLABS = STATE_TILES // SLAB_TILES
VMEM_LIMIT = 56 * 1024 * 1024


def _resident(shape):
    zeros = (0,) * len(shape)
    return pl.BlockSpec(shape, lambda *_: zeros, pipeline_mode=pl.Buffered(1))


def _sigmoid(x):
    return 1.0 / (1.0 + jnp.exp(-x))


def _params(*sem):
    return pltpu.CompilerParams(dimension_semantics=sem, vmem_limit_bytes=VMEM_LIMIT)


def _s5_prep_kernel(are_ref, aim_ref, lstep_ref, bre_ref, bim_ref,
                    abre_ref, abim_ref, bbre_ref, bbim_ref):
    a_re = are_ref[...]
    a_im = aim_ref[...]
    step = jnp.exp(lstep_ref[...])
    mag = jnp.exp(step * a_re)
    ab_re = mag * jnp.cos(step * a_im)
    ab_im = mag * jnp.sin(step * a_im)
    n_re = ab_re - 1.0
    den = a_re * a_re + a_im * a_im
    z_re = (n_re * a_re + ab_im * a_im) / den
    z_im = (ab_im * a_re - n_re * a_im) / den
    b_re = bre_ref[...]
    b_im = bim_ref[...]
    abre_ref[...] = ab_re
    abim_ref[...] = ab_im
    bbre_ref[...] = z_re * b_re - z_im * b_im
    bbim_ref[...] = z_re * b_im + z_im * b_re


def _s5_prep(a_re, a_im, log_step, b_re, b_im):
    depth = a_re.shape[0]
    rows = depth * GROUPS
    cols = NSTATE * GROUP_CH

    def expand(a):
        return jnp.broadcast_to(a[..., None], (depth, GROUPS, NSTATE, GROUP_CH)).reshape(rows, cols)

    step = jnp.broadcast_to(log_step[..., None], (depth, GROUPS, cols)).reshape(rows, cols)
    sds = jax.ShapeDtypeStruct((rows, cols), F32)
    ab_re, ab_im, bb_re, bb_im = pl.pallas_call(
        _s5_prep_kernel, out_shape=(sds, sds, sds, sds), name="s5_prep",
    )(expand(a_re), expand(a_im), step, b_re.reshape(rows, cols), b_im.reshape(rows, cols))
    shape4 = (depth, GROUPS, NSTATE, GROUP_CH)
    ab_re = ab_re.reshape(shape4)[..., 0]
    ab_im = ab_im.reshape(shape4)[..., 0]
    return ab_re, ab_im, bb_re.reshape(shape4), bb_im.reshape(shape4)


def _inproj_kernel(x_ref, gpre_ref, wlat_ref, wkr_ref, wwide_ref, gq_ref, gkv_ref,
                   cq_ref, ckv_ref, kr_ref, gm_ref, u_ref, gs_ref):
    x = x_ref[...]
    inv = lax.rsqrt(jnp.mean(x * x, axis=-1, keepdims=True) + NORM_EPS)
    h = (x * inv * gpre_ref[...]).astype(BF16)

    def proj(w_ref, lo, hi):
        return jnp.dot(h, w_ref[:, lo:hi], preferred_element_type=F32)

    def latent_norm(c, g_ref):
        inv_c = lax.rsqrt(jnp.mean(c * c, axis=-1, keepdims=True) + NORM_EPS)
        return (c * inv_c * g_ref[...]).astype(BF16)

    cq_ref[...] = latent_norm(proj(wlat_ref, 0, Q_LORA), gq_ref)
    ckv_ref[...] = latent_norm(proj(wlat_ref, Q_LORA, Q_LORA + KV_LORA), gkv_ref)
    kr_ref[...] = proj(wkr_ref, 0, LANES)
    gm_ref[...] = proj(wwide_ref, 0, D_MLA)
    u_ref[...] = proj(wwide_ref, D_MLA, D_MLA + D_SSM)
    gs_ref[...] = proj(wwide_ref, D_MLA + D_SSM, D_MLA + 2 * D_SSM)


def _inproj(x, g_pre, w_lat, w_kr, w_wide, g_q, g_kv, tm):
    L = x.shape[0]
    row = lambda n: pl.BlockSpec((tm, n), lambda i: (i, 0))
    out_shape = (
        jax.ShapeDtypeStruct((L, Q_LORA), BF16),
        jax.ShapeDtypeStruct((L, KV_LORA), BF16),
        jax.ShapeDtypeStruct((L, LANES), F32),
        jax.ShapeDtypeStruct((L, D_MLA), F32),
        jax.ShapeDtypeStruct((L, D_SSM), F32),
        jax.ShapeDtypeStruct((L, D_SSM), F32),
    )
    return pl.pallas_call(
        _inproj_kernel,
        grid=(L // tm,),
        in_specs=[row(D_MODEL), _resident((1, D_MODEL)), _resident((D_MODEL, Q_LORA + KV_LORA)),
                  _resident((D_MODEL, LANES)), _resident((D_MODEL, D_MLA + 2 * D_SSM)),
                  _resident((1, Q_LORA)), _resident((1, KV_LORA))],
        out_specs=(row(Q_LORA), row(KV_LORA), row(LANES), row(D_MLA), row(D_SSM), row(D_SSM)),
        out_shape=out_shape,
        compiler_params=_params("parallel"),
        name="inproj",
    )(x, g_pre, w_lat, w_kr, w_wide, g_q, g_kv)


def _rope_kernel(pos_ref, freq_ref, cos_ref, sin_ref):
    ang = freq_ref[...] * pos_ref[...].astype(F32)
    cos = jnp.cos(ang)
    sin = jnp.sin(ang)
    cos_ref[...] = jnp.concatenate([cos, cos], axis=0)
    sin_ref[...] = jnp.concatenate([sin, sin], axis=0)


def _rope_tables(positions, tm):
    L = positions.shape[-1]
    inv_freq = ROPE_THETA ** (-jnp.arange(0, ROPE, 2, dtype=F32) / ROPE)
    col = pl.BlockSpec((ROPE, tm), lambda i: (0, i))
    return pl.pallas_call(
        _rope_kernel,
        grid=(L // tm,),
        in_specs=[pl.BlockSpec((1, tm), lambda i: (0, i)), _resident((ROPE // 2, 1))],
        out_specs=(col, col),
        out_shape=(jax.ShapeDtypeStruct((ROPE, L), F32), jax.ShapeDtypeStruct((ROPE, L), F32)),
        compiler_params=_params("parallel"),
        name="rope_tables",
    )(positions.reshape(1, L), inv_freq.reshape(ROPE // 2, 1))


def _qkv_kernel(cq_ref, ckv_ref, kr_ref, cost_ref, sint_ref,
                wq_ref, wk_ref, wv_ref, qt_ref, k_ref, vt_ref, *, q_scale):
    cq = cq_ref[...]
    ckv = ckv_ref[...]
    half = ROPE // 2
    cos_t = cost_ref[...]
    sin_t = sint_ref[...]

    kt = kr_ref[...].T[:ROPE]
    kt_rot = jnp.concatenate([-kt[half:], kt[:half]], axis=0)
    kt = kt * cos_t + kt_rot * sin_t
    k_rope = jnp.concatenate([kt, jnp.zeros_like(kt)], axis=0).T[:, :ROPE].astype(BF16)
    k_nope = jnp.dot(ckv, wk_ref[...], preferred_element_type=F32)

    pad_row = lax.broadcasted_iota(jnp.int32, (V_ROWS - DV, vt_ref.shape[-1]), 0)
    ones_rows = jnp.where(pad_row == 0, 1.0, 0.0).astype(BF16)

    tn = (((0,), (1,)), ((), ()))
    qt_all = lax.dot_general(wq_ref[...], cq, tn, preferred_element_type=F32)
    vt_all = lax.dot_general(wv_ref[...], ckv, tn, preferred_element_type=F32)
    for h in range(HEADS):
        qt = qt_all[h * QK:(h + 1) * QK]
        qr = qt[NOPE:]
        qr_rot = jnp.concatenate([-qr[half:], qr[:half]], axis=0)
        qr = qr * cos_t + qr_rot * sin_t
        qt_ref[h, :NOPE, :] = (qt[:NOPE] * q_scale).astype(BF16)
        qt_ref[h, NOPE:, :] = (qr * q_scale).astype(BF16)
        k_ref[h, :, :NOPE] = k_nope[:, h * NOPE:(h + 1) * NOPE].astype(BF16)
        k_ref[h, :, NOPE:] = k_rope
        vt_ref[h, 0, :DV, :] = vt_all[h * DV:(h + 1) * DV].astype(BF16)
        vt_ref[h, 0, DV:, :] = ones_rows


def _qkv(cq, ckv, kr, rope, wq, wk, wv, tk):
    L = cq.shape[0]
    q_scale = math.log2(math.e) / math.sqrt(QK)
    row = lambda n: pl.BlockSpec((tk, n), lambda i: (i, 0))
    return pl.pallas_call(
        functools.partial(_qkv_kernel, q_scale=q_scale),
        grid=(L // tk,),
        in_specs=[row(Q_LORA), row(KV_LORA), row(LANES),
                  pl.BlockSpec((ROPE, tk), lambda i: (0, i)), pl.BlockSpec((ROPE, tk), lambda i: (0, i)),
                  _resident((Q_LORA, HEADS * QK)), _resident((KV_LORA, HEADS * NOPE)),
                  _resident((KV_LORA, HEADS * DV))],
        out_specs=(pl.BlockSpec((HEADS, QK, tk), lambda i: (0, 0, i)),
                   pl.BlockSpec((HEADS, tk, QK), lambda i: (0, i, 0)),
                   pl.BlockSpec((HEADS, 1, V_ROWS, tk), lambda i: (0, i, 0, 0))),
        out_shape=(jax.ShapeDtypeStruct((HEADS, QK, L), BF16),
                   jax.ShapeDtypeStruct((HEADS, L, QK), BF16),
                   jax.ShapeDtypeStruct((HEADS, L // tk, V_ROWS, tk), BF16)),
        compiler_params=_params("parallel"),
        name="qkv",
    )(cq, ckv, kr, *rope, wq, wk, wv)


def _attn_kernel(qt_ref, k_ref, vt_ref, gate_ref, o_ref, acc_ref, m_ref, alpha_ref, s_ref, p_ref, *, tq, tk):
    i = pl.program_id(1)
    m_ref[...] = jnp.full(m_ref.shape, -1e30, F32)
    acc_ref[...] = jnp.zeros(acc_ref.shape, F32)
    assert tq == 2 * tk
    n_chains = tq // MXU_COLS

    def cols(c):
        return slice(MXU_COLS * c, MXU_COLS * (c + 1))

    def n_keys(c, diag):
        return tk if diag is None else max(0, min(tk, MXU_COLS * (c + 1) - diag * tk))

    def scores(j, slot, c, diag):
        rows = n_keys(c, diag)
        if rows == 0:
            return
        kc = k_ref[0, pl.ds(pl.multiple_of(j * tk, tk), rows), :]
        s = jnp.dot(kc, qt_ref[0, :, cols(c)], preferred_element_type=F32)
        if diag is not None and diag * tk + rows - 1 > MXU_COLS * c:
            key = diag * tk + lax.broadcasted_iota(jnp.int32, s.shape, 0)
            qry = MXU_COLS * c + lax.broadcasted_iota(jnp.int32, s.shape, 1)
            s = jnp.where(key <= qry, s, -1e30)
        s_ref[slot, c, :rows, :] = s.astype(BF16)

    def softmax(slot, c, diag):
        rows = n_keys(c, diag)
        if rows == 0:
            return
        s = s_ref[slot, c, :rows, :]
        m_blk = jnp.max(s.reshape(rows // BF16_ROWS, BF16_ROWS, MXU_COLS), axis=0)
        m_blk = jnp.max(m_blk.astype(F32), axis=0, keepdims=True)
        m_prev = m_ref[:, cols(c)]
        m_new = jnp.maximum(m_prev, m_blk)
        alpha_ref[slot, :, cols(c)] = jnp.exp2(m_prev - m_new)
        p_ref[slot, c, :rows, :] = jnp.exp2(s - m_new.astype(BF16))
        m_ref[:, cols(c)] = m_new

    def pv(j, slot, c, diag):
        rows = n_keys(c, diag)
        if rows == 0:
            return
        vc = vt_ref[0, j, :, :rows]
        upd = jnp.dot(vc, p_ref[slot, c, :rows, :], preferred_element_type=F32)
        acc_ref[:, cols(c)] = alpha_ref[slot, :, cols(c)] * acc_ref[:, cols(c)] + upd

    def step(score=None, soft=None, pvs=None):
        for c in range(n_chains):
            if soft is not None:
                softmax(soft[1], c, soft[2])
            if score is not None:
                scores(score[0], score[1], c, score[2])
            if pvs is not None:
                pv(pvs[0], pvs[1], c, pvs[2])

    @pl.when(i > 0)
    def _():
        step(score=(0, 0, None))
        step(score=(1, 1, None), soft=(0, 0, None))

        def pair(t):
            j = 2 * t
            step(score=(j + 2, 0, None), soft=(j + 1, 1, None), pvs=(j, 0, None))
            step(score=(j + 3, 1, None), soft=(j + 2, 0, None), pvs=(j + 1, 1, None))

        def body(t, carry):
            pair(t)
            return carry

        lax.fori_loop(0, i - 1, body, 0)

        j = 2 * i - 2
        step(score=(j + 2, 0, 0), soft=(j + 1, 1, None), pvs=(j, 0, None))
        step(score=(j + 3, 1, 1), soft=(j + 2, 0, 0), pvs=(j + 1, 1, None))

    @pl.when(i == 0)
    def _():
        step(score=(0, 0, 0))
        step(score=(1, 1, 1), soft=(0, 0, 0))

    step(soft=(2 * i + 1, 1, 1), pvs=(2 * i, 0, 0))
    step(pvs=(2 * i + 1, 1, 1))

    out = (acc_ref[:DV, :] * (1.0 / acc_ref[DV:DV + 1, :])).T
    g = gate_ref[...]
    o_ref[...] = (out * (g * _sigmoid(g))).astype(BF16)


def _attention(qt, k, vt, gate, tq, tk):
    L = k.shape[1]
    return pl.pallas_call(
        functools.partial(_attn_kernel, tq=tq, tk=tk),
        grid=(HEADS, L // tq),
        in_specs=[pl.BlockSpec((1, QK, tq), lambda h, i: (h, 0, i)),
                  pl.BlockSpec((1, L, QK), lambda h, i: (h, 0, 0)),
                  pl.BlockSpec((1, L // tk, V_ROWS, tk), lambda h, i: (h, 0, 0, 0)),
                  pl.BlockSpec((tq, DV), lambda h, i: (i, h))],
        out_specs=pl.BlockSpec((tq, DV), lambda h, i: (i, h)),
        out_shape=jax.ShapeDtypeStruct((L, D_MLA), BF16),
        scratch_shapes=[pltpu.VMEM((V_ROWS, tq), F32), pltpu.VMEM((1, tq), F32),
                        pltpu.VMEM((2, 1, tq), F32),
                        pltpu.VMEM((2, tq // MXU_COLS, tk, MXU_COLS), BF16),
                        pltpu.VMEM((2, tq // MXU_COLS, tk, MXU_COLS), BF16)],
        compiler_params=_params("parallel", "arbitrary"),
        name="attention",
    )(qt, k, vt, gate)


def _gelu_tanh(x):
    c = math.sqrt(2.0 / math.pi)
    return 0.5 * x * (1.0 + jnp.tanh(c * (x + 0.044715 * (x * x * x))))


S5_CHUNK = 256
PITCH = 68
N_JOBS = 32
JOBS_PER_SLAB = N_JOBS // N_SLABS
TILES_PER_JOB = 2 * STATE_TILES // N_JOBS


def _job_tiles(slab, kk):
    is_im = kk // (JOBS_PER_SLAB // 2)
    q = kk % (JOBS_PER_SLAB // 2)
    tile0 = is_im * STATE_TILES + SLAB_TILES * slab + TILES_PER_JOB * q
    return 2 * slab + q // 2, tile0


def _s5_kernel(un_ref, up_ref, g_ref, wbj_ref, are_ref, aim_ref, wcj_ref, d_ref, wglu_ref,
               o_ref, hs0_ref, hs1_ref, st_ref, acc_ref, ub_ref, *, tt):
    c = pl.program_id(0)
    hs_refs = (hs0_ref, hs1_ref)

    def col_tile(tile):
        return pl.ds(tile, tt, stride=PITCH)

    def load_u_slabs(u_ref):
        u = u_ref[...]
        for s in range(D_SSM // LANES):
            ub_ref[s] = u[:, LANES * s:LANES * (s + 1)].astype(BF16)

    def b_job(slab, kk, slot):
        s, tile0 = _job_tiles(slab, kk)
        r = jnp.dot(ub_ref[s], wbj_ref[slab * JOBS_PER_SLAB + kk], preferred_element_type=F32)
        for k in range(TILES_PER_JOB):
            hs_refs[slot][col_tile(tile0 + k), :] = r[:, LANES * k:LANES * (k + 1)]

    def c_job(slab, kk, slot):
        _, tile0 = _job_tiles(slab, kk)
        lhs = jnp.concatenate([hs_refs[slot][col_tile(tile0 + k), :] for k in range(TILES_PER_JOB)],
                              axis=-1).astype(BF16)
        return jnp.dot(lhs, wcj_ref[slab * JOBS_PER_SLAB + kk], preferred_element_type=F32)

    @pl.when(c == 0)
    def _():
        st_ref[...] = jnp.zeros(st_ref.shape, F32)
        hs1_ref[...] = jnp.zeros(hs1_ref.shape, F32)
        load_u_slabs(up_ref)
        for ti in range(N_JOBS):
            b_job(ti // JOBS_PER_SLAB, ti % JOBS_PER_SLAB, 0)

    load_u_slabs(un_ref)
    a_re = [are_ref[i] for i in range(N_SLABS)]
    a_im = [aim_ref[i] for i in range(N_SLABS)]
    steps_per_job = tt // N_JOBS

    def run(slot):
        hs_ref = hs_refs[slot]

        def scan_steps(t0, st):
            st = list(st)
            for dt in range(steps_per_job):
                base = (t0 + dt) * PITCH
                for i in range(N_SLABS):
                    re_rows = pl.ds(base + SLAB_TILES * i, SLAB_TILES)
                    im_rows = pl.ds(base + STATE_TILES + SLAB_TILES * i, SLAB_TILES)
                    h_re, h_im = st[i], st[N_SLABS + i]
                    n_re = a_re[i] * h_re - a_im[i] * h_im + hs_ref[re_rows, :]
                    n_im = a_re[i] * h_im + a_im[i] * h_re + hs_ref[im_rows, :]
                    hs_ref[re_rows, :] = n_re
                    hs_ref[im_rows, :] = n_im
                    st[i], st[N_SLABS + i] = n_re, n_im
            return st

        st = [st_ref[i] for i in range(2 * N_SLABS)]
        for slab in range(N_SLABS):
            y = None
            for kk in range(JOBS_PER_SLAB):
                st = scan_steps((slab * JOBS_PER_SLAB + kk) * steps_per_job, st)
                part = c_job(slab, kk, 1 - slot)
                y = part if y is None else y + part
                b_job(slab, kk, 1 - slot)
            acc_ref[slab] = y
        for i in range(2 * N_SLABS):
            st_ref[i] = st[i]

    for slot in range(2):
        pl.when(c % 2 == slot)(functools.partial(run, slot))

    y = jnp.concatenate([acc_ref[i] for i in range(N_SLABS)], axis=-1) + d_ref[...] * up_ref[...]
    y = _gelu_tanh(y).astype(BF16)
    tile = D_SSM // N_SLABS
    for n in range(N_SLABS):
        val_cols = slice(tile * n, tile * (n + 1))
        gate_cols = slice(D_SSM + tile * n, D_SSM + tile * (n + 1))
        val = jnp.dot(y, wglu_ref[:, val_cols], preferred_element_type=F32)
        gate = jnp.dot(y, wglu_ref[:, gate_cols], preferred_element_type=F32)
        g = g_ref[:, val_cols]
        o_ref[:, val_cols] = (val * _sigmoid(gate) * (g * _sigmoid(g))).astype(BF16)


def _s5(u, gate, wbj, a_re, a_im, wcj, d_skip, w_glu, tt):
    L = u.shape[0]
    n = L // tt
    nxt = pl.BlockSpec((tt, D_SSM), lambda c: (jnp.minimum(c + 1, n - 1), 0))
    prv = pl.BlockSpec((tt, D_SSM), lambda c: (jnp.maximum(c - 1, 0), 0))
    job_cols = TILES_PER_JOB * LANES
    return pl.pallas_call(
        functools.partial(_s5_kernel, tt=tt),
        grid=(n + 1,),
        in_specs=[nxt, prv, prv,
                  _resident((N_JOBS, LANES, job_cols)),
                  _resident((N_SLABS, SUBLANES, LANES)), _resident((N_SLABS, SUBLANES, LANES)),
                  _resident((N_JOBS, job_cols, D_SSM // N_SLABS)),
                  _resident((1, D_SSM)), _resident((D_SSM, 2 * D_SSM))],
        out_specs=prv,
        out_shape=jax.ShapeDtypeStruct((L, D_SSM), BF16),
        scratch_shapes=[pltpu.VMEM((tt * PITCH, LANES), F32), pltpu.VMEM((tt * PITCH, LANES), F32),
                        pltpu.VMEM((2 * N_SLABS, SUBLANES, LANES), F32),
                        pltpu.VMEM((N_SLABS, tt, D_SSM // N_SLABS), F32),
                        pltpu.VMEM((D_SSM // LANES, tt, LANES), BF16)],
        compiler_params=_params("arbitrary"),
        name="s5",
    )(u, u, gate, wbj, a_re, a_im, wcj, d_skip, w_glu)


def _outproj_kernel(mla_ref, ssm_ref, w_ref, gpost_ref, x_ref, o_ref):
    out = jnp.dot(mla_ref[...], w_ref[:D_MLA, :], preferred_element_type=F32)
    out += jnp.dot(ssm_ref[...], w_ref[D_MLA:, :], preferred_element_type=F32)
    inv = lax.rsqrt(jnp.mean(out * out, axis=-1, keepdims=True) + NORM_EPS)
    o_ref[...] = x_ref[...] + out * inv * gpost_ref[...]


def _outproj(mla, ssm, w_out, g_post, x, tm):
    L = x.shape[0]
    row = lambda n: pl.BlockSpec((tm, n), lambda i: (i, 0))
    return pl.pallas_call(
        _outproj_kernel,
        grid=(L // tm,),
        in_specs=[row(D_MLA), row(D_SSM), _resident((D_MLA + D_SSM, D_MODEL)),
                  _resident((1, D_MODEL)), row(D_MODEL)],
        out_specs=row(D_MODEL),
        out_shape=jax.ShapeDtypeStruct((L, D_MODEL), F32),
        compiler_params=_params("parallel"),
        name="outproj",
    )(mla, ssm, w_out, g_post, x)


def _split_in_weights(w_in):
    s2 = Q_LORA + KV_LORA
    s3 = s2 + ROPE
    w = lax.optimization_barrier(w_in.astype(BF16))
    w_kr = jnp.pad(w[..., s2:s3], ((0, 0), (0, 0), (0, LANES - ROPE)))
    return w[..., :s2], w_kr, w[..., s3:]


def _pack_b(bb_re, bb_im):
    depth = bb_re.shape[0]
    gpt = LANES // GROUP_CH
    same_group = jnp.eye(gpt, dtype=bool)[None, None, :, None, :, None]

    def pack(b):
        b = b.astype(BF16).reshape(depth, GROUPS // gpt, gpt, NSTATE, GROUP_CH).transpose(0, 1, 2, 4, 3)
        blocks = jnp.where(same_group, b[:, :, :, :, None, :], 0)
        return blocks.reshape(depth, GROUPS // gpt, LANES, gpt * NSTATE)

    return jnp.concatenate([pack(bb_re), pack(bb_im)], axis=-1)


def _pack_c(c):
    depth = c.shape[0]
    gps = SLAB_TILES * LANES // NSTATE
    same_group = jnp.eye(gps, dtype=bool)[None, None, :, None, :, None]
    c = c.astype(BF16).reshape(depth, N_SLABS, gps, GROUP_CH, NSTATE).transpose(0, 1, 2, 4, 3)
    blocks = jnp.where(same_group, c[:, :, :, :, None, :], 0)
    return blocks.reshape(depth, N_SLABS, gps * NSTATE, gps * GROUP_CH)


def kernel(x, positions, norm_pre, norm_post, w_in, q_norm, w_uq, kv_norm, w_ukv, ssm_a_re, ssm_a_im,
           ssm_b_re, ssm_b_im, ssm_c_re, ssm_c_im, ssm_d, ssm_log_step, w_glu, w_out):
    batch, L, _ = x.shape
    assert batch == 1
    depth = w_in.shape[0]
    tm = min(512, L)
    tq = min(2048, L)
    tk = tq // 2
    tt = S5_CHUNK

    rope = _rope_tables(positions, tk)

    w_lat, w_kr, w_wide = _split_in_weights(w_in)
    wq = w_uq.astype(BF16)
    w_ukv4 = w_ukv.astype(BF16).reshape(depth, KV_LORA, HEADS, NOPE + DV)
    wk = w_ukv4[..., :NOPE].reshape(depth, KV_LORA, HEADS * NOPE)
    wv = w_ukv4[..., NOPE:].reshape(depth, KV_LORA, HEADS * DV)
    w_glu_b = w_glu.astype(BF16)
    w_out_b = w_out.astype(BF16)

    ab_re, ab_im, bb_re, bb_im = _s5_prep(ssm_a_re, ssm_a_im, ssm_log_step, ssm_b_re, ssm_b_im)
    wb = _pack_b(bb_re, bb_im)
    half = wb.shape[-1] // 2
    job_cols = TILES_PER_JOB * LANES
    wbj = []
    for ti in range(N_JOBS):
        s, tile0 = _job_tiles(ti // JOBS_PER_SLAB, ti % JOBS_PER_SLAB)
        col = (tile0 // STATE_TILES) * half + (tile0 % STATE_TILES) * LANES - s * half
        wbj.append(wb[:, s, :, col:col + job_cols])
    wbj = jnp.stack(wbj, axis=1)
    slab_shape = (depth, N_SLABS, SUBLANES, LANES)
    a_re_s = ab_re.reshape(slab_shape)
    a_im_s = ab_im.reshape(slab_shape)
    wc_re = _pack_c(ssm_c_re).reshape(depth, N_SLABS, JOBS_PER_SLAB // 2, job_cols, -1)
    wc_im = _pack_c(-ssm_c_im).reshape(depth, N_SLABS, JOBS_PER_SLAB // 2, job_cols, -1)
    wcj = jnp.concatenate([wc_re, wc_im], axis=2).reshape(depth, N_JOBS, job_cols, -1)

    h = x[0]
    for i in range(depth):
        cq, ckv, kr, gate_mla, u, gate_ssm = _inproj(
            h, norm_pre[i][None], w_lat[i], w_kr[i], w_wide[i], q_norm[i][None], kv_norm[i][None], tm)
        qt, k, vt = _qkv(cq, ckv, kr, rope, wq[i], wk[i], wv[i], tk)
        mla = _attention(qt, k, vt, gate_mla, tq, tk)
        ssm = _s5(u, gate_ssm, wbj[i], a_re_s[i], a_im_s[i], wcj[i], ssm_d[i][None], w_glu_b[i], tt)
        h = _outproj(mla, ssm, w_out_b[i], norm_post[i][None], h, tm)
    return h[None]
```

```python
import functools
import math

import jax
import jax.numpy as jnp
from jax import lax
from jax.experimental import pallas as pl
from jax.experimental.pallas import tpu as pltpu

F32 = jnp.float32
BF16 = jnp.bfloat16

D_MODEL = 2048
HEADS = 8
NOPE = 128
ROPE = 64
QK = NOPE + ROPE
DV = 128
V_ROWS = DV + 16
Q_LORA = 512
KV_LORA = 256
D_MLA = HEADS * DV
D_SSM = 1024
GROUP_CH = 16
GROUPS = D_SSM // GROUP_CH
NSTATE = 64
ROPE_THETA = 10000.0
NORM_EPS = 1e-6
LANES = 128
SUBLANES = 8
MXU_COLS = 256
BF16_ROWS = 16
STATE_TILES = GROUPS * NSTATE // LANES
SLAB_TILES = SUBLANES
N_SLABS = STATE_TILES // SLAB_TILES
VMEM_LIMIT = 56 * 1024 * 1024


def _resident(shape):
    zeros = (0,) * len(shape)
    return pl.BlockSpec(shape, lambda *_: zeros, pipeline_mode=pl.Buffered(1))


def _layer(stack, layer):
    shape = stack.shape[1:]
    index = (layer,) + (0,) * len(shape)
    return pl.BlockSpec((None,) + shape, lambda *_: index, pipeline_mode=pl.Buffered(1))


def _sigmoid(x):
    return 1.0 / (1.0 + jnp.exp(-x))


def _params(*sem):
    return pltpu.CompilerParams(dimension_semantics=sem, vmem_limit_bytes=VMEM_LIMIT)


def _s5_prep_kernel(are_ref, aim_ref, lstep_ref, bre_ref, bim_ref,
                    abre_ref, abim_ref, bbre_ref, bbim_ref):
    a_re = are_ref[...]
    a_im = aim_ref[...]
    step = jnp.exp(lstep_ref[...])
    mag = jnp.exp(step * a_re)
    ab_re = mag * jnp.cos(step * a_im)
    ab_im = mag * jnp.sin(step * a_im)
    n_re = ab_re - 1.0
    den = a_re * a_re + a_im * a_im
    z_re = (n_re * a_re + ab_im * a_im) / den
    z_im = (ab_im * a_re - n_re * a_im) / den
    b_re = bre_ref[...]
    b_im = bim_ref[...]
    abre_ref[...] = ab_re
    abim_ref[...] = ab_im
    bbre_ref[...] = z_re * b_re - z_im * b_im
    bbim_ref[...] = z_re * b_im + z_im * b_re


def _s5_prep(a_re, a_im, log_step, b_re, b_im):
    depth = a_re.shape[0]
    rows = depth * GROUPS
    cols = NSTATE * GROUP_CH

    def expand(a):
        return jnp.broadcast_to(a[..., None], (depth, GROUPS, NSTATE, GROUP_CH)).reshape(rows, cols)

    step = jnp.broadcast_to(log_step[..., None], (depth, GROUPS, cols)).reshape(rows, cols)
    sds = jax.ShapeDtypeStruct((rows, cols), F32)
    ab_re, ab_im, bb_re, bb_im = pl.pallas_call(
        _s5_prep_kernel, out_shape=(sds, sds, sds, sds), name="s5_prep",
    )(expand(a_re), expand(a_im), step, b_re.reshape(rows, cols), b_im.reshape(rows, cols))
    shape4 = (depth, GROUPS, NSTATE, GROUP_CH)
    ab_re = ab_re.reshape(shape4)[..., 0]
    ab_im = ab_im.reshape(shape4)[..., 0]
    return ab_re, ab_im, bb_re.reshape(shape4), bb_im.reshape(shape4)


def _inproj_kernel(x_ref, gpre_ref, wlat_ref, wkr_ref, wwide_ref, gq_ref, gkv_ref,
                   cq_ref, ckv_ref, kr_ref, gm_ref, u_ref, gs_ref):
    x = x_ref[...]
    inv = lax.rsqrt(jnp.mean(x * x, axis=-1, keepdims=True) + NORM_EPS)
    h = (x * inv * gpre_ref[...]).astype(BF16)

    def proj(w_ref, lo, hi):
        return jnp.dot(h, w_ref[:, lo:hi], preferred_element_type=F32)

    def latent_norm(c, g_ref):
        inv_c = lax.rsqrt(jnp.mean(c * c, axis=-1, keepdims=True) + NORM_EPS)
        return (c * inv_c * g_ref[...]).astype(BF16)

    cq_ref[...] = latent_norm(proj(wlat_ref, 0, Q_LORA), gq_ref)
    ckv_ref[...] = latent_norm(proj(wlat_ref, Q_LORA, Q_LORA + KV_LORA), gkv_ref)
    kr_ref[...] = proj(wkr_ref, 0, LANES)
    gm_ref[...] = proj(wwide_ref, 0, D_MLA)
    u_ref[...] = proj(wwide_ref, D_MLA, D_MLA + D_SSM)
    gs_ref[...] = proj(wwide_ref, D_MLA + D_SSM, D_MLA + 2 * D_SSM)


def _inproj(x, layer, g_pre, w_lat, w_kr, w_wide, g_q, g_kv, tm):
    L = x.shape[0]
    row = lambda n: pl.BlockSpec((tm, n), lambda i: (i, 0))
    out_shape = (
        jax.ShapeDtypeStruct((L, Q_LORA), BF16),
        jax.ShapeDtypeStruct((L, KV_LORA), BF16),
        jax.ShapeDtypeStruct((L, LANES), F32),
        jax.ShapeDtypeStruct((L, D_MLA), F32),
        jax.ShapeDtypeStruct((L, D_SSM), F32),
        jax.ShapeDtypeStruct((L, D_SSM), F32),
    )
    return pl.pallas_call(
        _inproj_kernel,
        grid=(L // tm,),
        in_specs=[row(D_MODEL)] + [_layer(p, layer) for p in (g_pre, w_lat, w_kr, w_wide, g_q, g_kv)],
        out_specs=(row(Q_LORA), row(KV_LORA), row(LANES), row(D_MLA), row(D_SSM), row(D_SSM)),
        out_shape=out_shape,
        compiler_params=_params("parallel"),
        name="inproj",
    )(x, g_pre, w_lat, w_kr, w_wide, g_q, g_kv)


def _rope_kernel(pos_ref, freq_ref, cos_ref, sin_ref):
    ang = freq_ref[...] * pos_ref[...].astype(F32)
    cos = jnp.cos(ang)
    sin = jnp.sin(ang)
    cos_ref[...] = jnp.concatenate([cos, cos], axis=0)
    sin_ref[...] = jnp.concatenate([sin, sin], axis=0)


def _rope_tables(positions, tm):
    L = positions.shape[-1]
    inv_freq = ROPE_THETA ** (-jnp.arange(0, ROPE, 2, dtype=F32) / ROPE)
    col = pl.BlockSpec((ROPE, tm), lambda i: (0, i))
    return pl.pallas_call(
        _rope_kernel,
        grid=(L // tm,),
        in_specs=[pl.BlockSpec((1, tm), lambda i: (0, i)), _resident((ROPE // 2, 1))],
        out_specs=(col, col),
        out_shape=(jax.ShapeDtypeStruct((ROPE, L), F32), jax.ShapeDtypeStruct((ROPE, L), F32)),
        compiler_params=_params("parallel"),
        name="rope_tables",
    )(positions.reshape(1, L), inv_freq.reshape(ROPE // 2, 1))


def _qkv_kernel(cq_ref, ckv_ref, kr_ref, cost_ref, sint_ref,
                wq_ref, wk_ref, wv_ref, qt_ref, k_ref, vt_ref, *, q_scale):
    cq = cq_ref[...]
    ckv = ckv_ref[...]
    half = ROPE // 2
    cos_t = cost_ref[...]
    sin_t = sint_ref[...]

    kt = kr_ref[...].T[:ROPE]
    kt_rot = jnp.concatenate([-kt[half:], kt[:half]], axis=0)
    kt = kt * cos_t + kt_rot * sin_t
    k_rope = jnp.concatenate([kt, jnp.zeros_like(kt)], axis=0).T[:, :ROPE].astype(BF16)
    k_nope = jnp.dot(ckv, wk_ref[...], preferred_element_type=F32)

    pad_row = lax.broadcasted_iota(jnp.int32, (V_ROWS - DV, vt_ref.shape[-1]), 0)
    ones_rows = jnp.where(pad_row == 0, 1.0, 0.0).astype(BF16)

    tn = (((0,), (1,)), ((), ()))
    qt_all = lax.dot_general(wq_ref[...], cq, tn, preferred_element_type=F32)
    vt_all = lax.dot_general(wv_ref[...], ckv, tn, preferred_element_type=F32)
    for h in range(HEADS):
        qt = qt_all[h * QK:(h + 1) * QK]
        qr = qt[NOPE:]
        qr_rot = jnp.concatenate([-qr[half:], qr[:half]], axis=0)
        qr = qr * cos_t + qr_rot * sin_t
        qt_ref[h, :NOPE, :] = (qt[:NOPE] * q_scale).astype(BF16)
        qt_ref[h, NOPE:, :] = (qr * q_scale).astype(BF16)
        k_ref[h, :, :NOPE] = k_nope[:, h * NOPE:(h + 1) * NOPE].astype(BF16)
        k_ref[h, :, NOPE:] = k_rope
        vt_ref[h, 0, :DV, :] = vt_all[h * DV:(h + 1) * DV].astype(BF16)
        vt_ref[h, 0, DV:, :] = ones_rows


def _qkv(cq, ckv, kr, rope, layer, wq, wk, wv, tk):
    L = cq.shape[0]
    q_scale = math.log2(math.e) / math.sqrt(QK)
    row = lambda n: pl.BlockSpec((tk, n), lambda i: (i, 0))
    return pl.pallas_call(
        functools.partial(_qkv_kernel, q_scale=q_scale),
        grid=(L // tk,),
        in_specs=[row(Q_LORA), row(KV_LORA), row(LANES),
                  pl.BlockSpec((ROPE, tk), lambda i: (0, i)), pl.BlockSpec((ROPE, tk), lambda i: (0, i)),
                  _layer(wq, layer), _layer(wk, layer), _layer(wv, layer)],
        out_specs=(pl.BlockSpec((HEADS, QK, tk), lambda i: (0, 0, i)),
                   pl.BlockSpec((HEADS, tk, QK), lambda i: (0, i, 0)),
                   pl.BlockSpec((HEADS, 1, V_ROWS, tk), lambda i: (0, i, 0, 0))),
        out_shape=(jax.ShapeDtypeStruct((HEADS, QK, L), BF16),
                   jax.ShapeDtypeStruct((HEADS, L, QK), BF16),
                   jax.ShapeDtypeStruct((HEADS, L // tk, V_ROWS, tk), BF16)),
        compiler_params=_params("parallel"),
        name="qkv",
    )(cq, ckv, kr, *rope, wq, wk, wv)


def _attn_kernel(qt_ref, k_ref, vt_ref, gate_ref, o_ref, acc_ref, m_ref, alpha_ref, s_ref, p_ref, *, tq, tk):
    i = pl.program_id(1)
    m_ref[...] = jnp.full(m_ref.shape, -1e30, F32)
    acc_ref[...] = jnp.zeros(acc_ref.shape, F32)
    assert tq == 2 * tk
    n_chains = tq // MXU_COLS

    def cols(c):
        return slice(MXU_COLS * c, MXU_COLS * (c + 1))

    def n_keys(c, diag):
        return tk if diag is None else max(0, min(tk, MXU_COLS * (c + 1) - diag * tk))

    def scores(j, slot, c, diag):
        rows = n_keys(c, diag)
        if rows == 0:
            return
        kc = k_ref[0, pl.ds(pl.multiple_of(j * tk, tk), rows), :]
        s = jnp.dot(kc, qt_ref[0, :, cols(c)], preferred_element_type=F32)
        if diag is not None and diag * tk + rows - 1 > MXU_COLS * c:
            key = diag * tk + lax.broadcasted_iota(jnp.int32, s.shape, 0)
            qry = MXU_COLS * c + lax.broadcasted_iota(jnp.int32, s.shape, 1)
            s = jnp.where(key <= qry, s, -1e30)
        s_ref[slot, c, :rows, :] = s.astype(BF16)

    def softmax(slot, c, diag):
        rows = n_keys(c, diag)
        if rows == 0:
            return
        s = s_ref[slot, c, :rows, :]
        m_blk = jnp.max(s.reshape(rows // BF16_ROWS, BF16_ROWS, MXU_COLS), axis=0)
        m_blk = jnp.max(m_blk.astype(F32), axis=0, keepdims=True)
        m_prev = m_ref[:, cols(c)]
        m_new = jnp.maximum(m_prev, m_blk)
        alpha_ref[slot, :, cols(c)] = jnp.exp2(m_prev - m_new)
        p_ref[slot, c, :rows, :] = jnp.exp2(s - m_new.astype(BF16))
        m_ref[:, cols(c)] = m_new

    def pv(j, slot, c, diag):
        rows = n_keys(c, diag)
        if rows == 0:
            return
        vc = vt_ref[0, j, :, :rows]
        upd = jnp.dot(vc, p_ref[slot, c, :rows, :], preferred_element_type=F32)
        acc_ref[:, cols(c)] = alpha_ref[slot, :, cols(c)] * acc_ref[:, cols(c)] + upd

    def step(score=None, soft=None, pvs=None):
        for c in range(n_chains):
            if soft is not None:
                softmax(soft[1], c, soft[2])
            if score is not None:
                scores(score[0], score[1], c, score[2])
            if pvs is not None:
                pv(pvs[0], pvs[1], c, pvs[2])

    @pl.when(i > 0)
    def _():
        step(score=(0, 0, None))
        step(score=(1, 1, None), soft=(0, 0, None))

        def pair(t):
            j = 2 * t
            step(score=(j + 2, 0, None), soft=(j + 1, 1, None), pvs=(j, 0, None))
            step(score=(j + 3, 1, None), soft=(j + 2, 0, None), pvs=(j + 1, 1, None))

        def body(t, carry):
            pair(t)
            return carry

        lax.fori_loop(0, i - 1, body, 0)

        j = 2 * i - 2
        step(score=(j + 2, 0, 0), soft=(j + 1, 1, None), pvs=(j, 0, None))
        step(score=(j + 3, 1, 1), soft=(j + 2, 0, 0), pvs=(j + 1, 1, None))

    @pl.when(i == 0)
    def _():
        step(score=(0, 0, 0))
        step(score=(1, 1, 1), soft=(0, 0, 0))

    step(soft=(2 * i + 1, 1, 1), pvs=(2 * i, 0, 0))
    step(pvs=(2 * i + 1, 1, 1))

    out = (acc_ref[:DV, :] * (1.0 / acc_ref[DV:DV + 1, :])).T
    g = gate_ref[...]
    o_ref[...] = (out * (g * _sigmoid(g))).astype(BF16)


def _attention(qt, k, vt, gate, tq, tk):
    L = k.shape[1]
    return pl.pallas_call(
        functools.partial(_attn_kernel, tq=tq, tk=tk),
        grid=(HEADS, L // tq),
        in_specs=[pl.BlockSpec((1, QK, tq), lambda h, i: (h, 0, i)),
                  pl.BlockSpec((1, L, QK), lambda h, i: (h, 0, 0)),
                  pl.BlockSpec((1, L // tk, V_ROWS, tk), lambda h, i: (h, 0, 0, 0)),
                  pl.BlockSpec((tq, DV), lambda h, i: (i, h))],
        out_specs=pl.BlockSpec((tq, DV), lambda h, i: (i, h)),
        out_shape=jax.ShapeDtypeStruct((L, D_MLA), BF16),
        scratch_shapes=[pltpu.VMEM((V_ROWS, tq), F32), pltpu.VMEM((1, tq), F32),
                        pltpu.VMEM((2, 1, tq), F32),
                        pltpu.VMEM((2, tq // MXU_COLS, tk, MXU_COLS), BF16),
                        pltpu.VMEM((2, tq // MXU_COLS, tk, MXU_COLS), BF16)],
        compiler_params=_params("parallel", "arbitrary"),
        name="attention",
    )(qt, k, vt, gate)


def _gelu_tanh(x):
    c = math.sqrt(2.0 / math.pi)
    return 0.5 * x * (1.0 + jnp.tanh(c * (x + 0.044715 * (x * x * x))))


S5_CHUNK = 256
PITCH = 68
N_JOBS = 32
JOBS_PER_SLAB = N_JOBS // N_SLABS
TILES_PER_JOB = 2 * STATE_TILES // N_JOBS


def _job_tiles(slab, kk):
    is_im = kk // (JOBS_PER_SLAB // 2)
    q = kk % (JOBS_PER_SLAB // 2)
    tile0 = is_im * STATE_TILES + SLAB_TILES * slab + TILES_PER_JOB * q
    return 2 * slab + q // 2, tile0


def _s5_kernel(un_ref, up_ref, g_ref, wbj_ref, are_ref, aim_ref, wcj_ref, d_ref, wglu_ref,
               o_ref, hs0_ref, hs1_ref, st_ref, acc_ref, ub_ref, *, tt):
    c = pl.program_id(0)
    hs_refs = (hs0_ref, hs1_ref)

    def col_tile(tile):
        return pl.ds(tile, tt, stride=PITCH)

    def load_u_slabs(u_ref):
        u = u_ref[...]
        for s in range(D_SSM // LANES):
            ub_ref[s] = u[:, LANES * s:LANES * (s + 1)].astype(BF16)

    def b_job(slab, kk, slot):
        s, tile0 = _job_tiles(slab, kk)
        r = jnp.dot(ub_ref[s], wbj_ref[slab * JOBS_PER_SLAB + kk], preferred_element_type=F32)
        for k in range(TILES_PER_JOB):
            hs_refs[slot][col_tile(tile0 + k), :] = r[:, LANES * k:LANES * (k + 1)]

    def c_job(slab, kk, slot):
        _, tile0 = _job_tiles(slab, kk)
        lhs = jnp.concatenate([hs_refs[slot][col_tile(tile0 + k), :] for k in range(TILES_PER_JOB)],
                              axis=-1).astype(BF16)
        return jnp.dot(lhs, wcj_ref[slab * JOBS_PER_SLAB + kk], preferred_element_type=F32)

    @pl.when(c == 0)
    def _():
        st_ref[...] = jnp.zeros(st_ref.shape, F32)
        hs1_ref[...] = jnp.zeros(hs1_ref.shape, F32)
        load_u_slabs(up_ref)
        for ti in range(N_JOBS):
            b_job(ti // JOBS_PER_SLAB, ti % JOBS_PER_SLAB, 0)

    load_u_slabs(un_ref)
    a_re = [are_ref[i] for i in range(N_SLABS)]
    a_im = [aim_ref[i] for i in range(N_SLABS)]
    steps_per_job = tt // N_JOBS

    def run(slot):
        hs_ref = hs_refs[slot]

        def scan_steps(t0, st):
            st = list(st)
            for dt in range(steps_per_job):
                base = (t0 + dt) * PITCH
                for i in range(N_SLABS):
                    re_rows = pl.ds(base + SLAB_TILES * i, SLAB_TILES)
                    im_rows = pl.ds(base + STATE_TILES + SLAB_TILES * i, SLAB_TILES)
                    h_re, h_im = st[i], st[N_SLABS + i]
                    n_re = a_re[i] * h_re - a_im[i] * h_im + hs_ref[re_rows, :]
                    n_im = a_re[i] * h_im + a_im[i] * h_re + hs_ref[im_rows, :]
                    hs_ref[re_rows, :] = n_re
                    hs_ref[im_rows, :] = n_im
                    st[i], st[N_SLABS + i] = n_re, n_im
            return st

        st = [st_ref[i] for i in range(2 * N_SLABS)]
        for slab in range(N_SLABS):
            y = None
            for kk in range(JOBS_PER_SLAB):
                st = scan_steps((slab * JOBS_PER_SLAB + kk) * steps_per_job, st)
                part = c_job(slab, kk, 1 - slot)
                y = part if y is None else y + part
                b_job(slab, kk, 1 - slot)
            acc_ref[slab] = y
        for i in range(2 * N_SLABS):
            st_ref[i] = st[i]

    for slot in range(2):
        pl.when(c % 2 == slot)(functools.partial(run, slot))

    y = jnp.concatenate([acc_ref[i] for i in range(N_SLABS)], axis=-1) + d_ref[...] * up_ref[...]
    y = _gelu_tanh(y).astype(BF16)
    tile = D_SSM // N_SLABS
    for n in range(N_SLABS):
        val_cols = slice(tile * n, tile * (n + 1))
        gate_cols = slice(D_SSM + tile * n, D_SSM + tile * (n + 1))
        val = jnp.dot(y, wglu_ref[:, val_cols], preferred_element_type=F32)
        gate = jnp.dot(y, wglu_ref[:, gate_cols], preferred_element_type=F32)
        g = g_ref[:, val_cols]
        o_ref[:, val_cols] = (val * _sigmoid(gate) * (g * _sigmoid(g))).astype(BF16)


def _s5(u, gate, layer, wbj, a_re, a_im, wcj, d_skip, w_glu, tt):
    L = u.shape[0]
    n = L // tt
    nxt = pl.BlockSpec((tt, D_SSM), lambda c: (jnp.minimum(c + 1, n - 1), 0))
    prv = pl.BlockSpec((tt, D_SSM), lambda c: (jnp.maximum(c - 1, 0), 0))
    return pl.pallas_call(
        functools.partial(_s5_kernel, tt=tt),
        grid=(n + 1,),
        in_specs=[nxt, prv, prv] + [_layer(p, layer) for p in (wbj, a_re, a_im, wcj, d_skip, w_glu)],
        out_specs=prv,
        out_shape=jax.ShapeDtypeStruct((L, D_SSM), BF16),
        scratch_shapes=[pltpu.VMEM((tt * PITCH, LANES), F32), pltpu.VMEM((tt * PITCH, LANES), F32),
                        pltpu.VMEM((2 * N_SLABS, SUBLANES, LANES), F32),
                        pltpu.VMEM((N_SLABS, tt, D_SSM // N_SLABS), F32),
                        pltpu.VMEM((D_SSM // LANES, tt, LANES), BF16)],
        compiler_params=_params("arbitrary"),
        name="s5",
    )(u, u, gate, wbj, a_re, a_im, wcj, d_skip, w_glu)


def _outproj_kernel(mla_ref, ssm_ref, w_ref, gpost_ref, x_ref, o_ref):
    out = jnp.dot(mla_ref[...], w_ref[:D_MLA, :], preferred_element_type=F32)
    out += jnp.dot(ssm_ref[...], w_ref[D_MLA:, :], preferred_element_type=F32)
    inv = lax.rsqrt(jnp.mean(out * out, axis=-1, keepdims=True) + NORM_EPS)
    o_ref[...] = x_ref[...] + out * inv * gpost_ref[...]


def _outproj(mla, ssm, layer, w_out, g_post, x, tm):
    L = x.shape[0]
    row = lambda n: pl.BlockSpec((tm, n), lambda i: (i, 0))
    return pl.pallas_call(
        _outproj_kernel,
        grid=(L // tm,),
        in_specs=[row(D_MLA), row(D_SSM), _layer(w_out, layer), _layer(g_post, layer), row(D_MODEL)],
        out_specs=row(D_MODEL),
        out_shape=jax.ShapeDtypeStruct((L, D_MODEL), F32),
        compiler_params=_params("parallel"),
        name="outproj",
    )(mla, ssm, w_out, g_post, x)


def _split_in_weights(w_in):
    s2 = Q_LORA + KV_LORA
    s3 = s2 + ROPE
    w = lax.optimization_barrier(w_in.astype(BF16))
    w_kr = jnp.pad(w[..., s2:s3], ((0, 0), (0, 0), (0, LANES - ROPE)))
    return w[..., :s2], w_kr, w[..., s3:]


def _pack_b(bb_re, bb_im):
    depth = bb_re.shape[0]
    gpt = LANES // GROUP_CH
    same_group = jnp.eye(gpt, dtype=bool)[None, None, :, None, :, None]

    def pack(b):
        b = b.astype(BF16).reshape(depth, GROUPS // gpt, gpt, NSTATE, GROUP_CH).transpose(0, 1, 2, 4, 3)
        blocks = jnp.where(same_group, b[:, :, :, :, None, :], 0)
        return blocks.reshape(depth, GROUPS // gpt, LANES, gpt * NSTATE)

    return jnp.concatenate([pack(bb_re), pack(bb_im)], axis=-1)


def _pack_c(c):
    depth = c.shape[0]
    gps = SLAB_TILES * LANES // NSTATE
    same_group = jnp.eye(gps, dtype=bool)[None, None, :, None, :, None]
    c = c.astype(BF16).reshape(depth, N_SLABS, gps, GROUP_CH, NSTATE).transpose(0, 1, 2, 4, 3)
    blocks = jnp.where(same_group, c[:, :, :, :, None, :], 0)
    return blocks.reshape(depth, N_SLABS, gps * NSTATE, gps * GROUP_CH)


def kernel(x, positions, norm_pre, norm_post, w_in, q_norm, w_uq, kv_norm, w_ukv, ssm_a_re, ssm_a_im,
           ssm_b_re, ssm_b_im, ssm_c_re, ssm_c_im, ssm_d, ssm_log_step, w_glu, w_out):
    batch, L, _ = x.shape
    assert batch == 1
    depth = w_in.shape[0]
    tm = min(512, L)
    tq = min(2048, L)
    tk = tq // 2
    tt = S5_CHUNK

    rope = _rope_tables(positions, tk)

    w_lat, w_kr, w_wide = _split_in_weights(w_in)
    wq = w_uq.astype(BF16)
    w_ukv4 = w_ukv.astype(BF16).reshape(depth, KV_LORA, HEADS, NOPE + DV)
    wk = w_ukv4[..., :NOPE].reshape(depth, KV_LORA, HEADS * NOPE)
    wv = w_ukv4[..., NOPE:].reshape(depth, KV_LORA, HEADS * DV)
    w_glu_b = w_glu.astype(BF16)
    w_out_b = w_out.astype(BF16)

    ab_re, ab_im, bb_re, bb_im = _s5_prep(ssm_a_re, ssm_a_im, ssm_log_step, ssm_b_re, ssm_b_im)
    wb = _pack_b(bb_re, bb_im)
    half = wb.shape[-1] // 2
    job_cols = TILES_PER_JOB * LANES
    wbj = []
    for ti in range(N_JOBS):
        s, tile0 = _job_tiles(ti // JOBS_PER_SLAB, ti % JOBS_PER_SLAB)
        col = (tile0 // STATE_TILES) * half + (tile0 % STATE_TILES) * LANES - s * half
        wbj.append(wb[:, s, :, col:col + job_cols])
    wbj = jnp.stack(wbj, axis=1)
    slab_shape = (depth, N_SLABS, SUBLANES, LANES)
    a_re_s = ab_re.reshape(slab_shape)
    a_im_s = ab_im.reshape(slab_shape)
    wc_re = _pack_c(ssm_c_re).reshape(depth, N_SLABS, JOBS_PER_SLAB // 2, job_cols, -1)
    wc_im = _pack_c(-ssm_c_im).reshape(depth, N_SLABS, JOBS_PER_SLAB // 2, job_cols, -1)
    wcj = jnp.concatenate([wc_re, wc_im], axis=2).reshape(depth, N_JOBS, job_cols, -1)

    def vec(p):
        return p.reshape(depth, 1, -1)

    h = x[0]
    for i in range(depth):
        cq, ckv, kr, gate_mla, u, gate_ssm = _inproj(
            h, i, vec(norm_pre), w_lat, w_kr, w_wide, vec(q_norm), vec(kv_norm), tm)
        qt, k, vt = _qkv(cq, ckv, kr, rope, i, wq, wk, wv, tk)
        mla = _attention(qt, k, vt, gate_mla, tq, tk)
        ssm = _s5(u, gate_ssm, i, wbj, a_re_s, a_im_s, wcj, vec(ssm_d), w_glu_b, tt)
        h = _outproj(mla, ssm, i, w_out_b, vec(norm_post), h, tm)
    return h[None]
```

```python
import functools
import math

import jax
import jax.numpy as jnp
from jax import lax
from jax.experimental import pallas as pl
from jax.experimental.pallas import tpu as pltpu

F32 = jnp.float32
BF16 = jnp.bfloat16

D_MODEL = 2048
HEADS = 8
NOPE = 128
ROPE = 64
QK = NOPE + ROPE
DV = 128
V_ROWS = DV + 16
Q_LORA = 512
KV_LORA = 256
D_MLA = HEADS * DV
D_SSM = 1024
GROUP_CH = 16
GROUPS = D_SSM // GROUP_CH
NSTATE = 64
ROPE_THETA = 10000.0
NORM_EPS = 1e-6
LANES = 128
SUBLANES = 8
MXU_COLS = 256
BF16_ROWS = 16
STATE_TILES = GROUPS * NSTATE // LANES
SLAB_TILES = SUBLANES
N_SLABS = STATE_TILES // SLAB_TILES
VMEM_LIMIT = 56 * 1024 * 1024


def _resident(shape):
    zeros = (0,) * len(shape)
    return pl.BlockSpec(shape, lambda *_: zeros, pipeline_mode=pl.Buffered(1))


def _layer(stack, layer):
    shape = stack.shape[1:]
    index = (layer,) + (0,) * len(shape)
    return pl.BlockSpec((None,) + shape, lambda *_: index, pipeline_mode=pl.Buffered(1))


def _sigmoid(x):
    return 1.0 / (1.0 + jnp.exp(-x))


def _params(*sem):
    return pltpu.CompilerParams(dimension_semantics=sem, vmem_limit_bytes=VMEM_LIMIT)


def _s5_prep_kernel(are_ref, aim_ref, lstep_ref, bre_ref, bim_ref,
                    abre_ref, abim_ref, bbre_ref, bbim_ref):
    a_re = are_ref[...]
    a_im = aim_ref[...]
    step = jnp.exp(lstep_ref[...])
    mag = jnp.exp(step * a_re)
    ab_re = mag * jnp.cos(step * a_im)
    ab_im = mag * jnp.sin(step * a_im)
    n_re = ab_re - 1.0
    den = a_re * a_re + a_im * a_im
    z_re = (n_re * a_re + ab_im * a_im) / den
    z_im = (ab_im * a_re - n_re * a_im) / den
    b_re = bre_ref[...]
    b_im = bim_ref[...]
    abre_ref[...] = ab_re
    abim_ref[...] = ab_im
    bbre_ref[...] = z_re * b_re - z_im * b_im
    bbim_ref[...] = z_re * b_im + z_im * b_re


def _s5_prep(a_re, a_im, log_step, b_re, b_im):
    depth = a_re.shape[0]
    rows = depth * GROUPS
    cols = NSTATE * GROUP_CH

    def expand(a):
        return jnp.broadcast_to(a[..., None], (depth, GROUPS, NSTATE, GROUP_CH)).reshape(rows, cols)

    step = jnp.broadcast_to(log_step[..., None], (depth, GROUPS, cols)).reshape(rows, cols)
    sds = jax.ShapeDtypeStruct((rows, cols), F32)
    ab_re, ab_im, bb_re, bb_im = pl.pallas_call(
        _s5_prep_kernel, out_shape=(sds, sds, sds, sds), name="s5_prep",
    )(expand(a_re), expand(a_im), step, b_re.reshape(rows, cols), b_im.reshape(rows, cols))
    shape4 = (depth, GROUPS, NSTATE, GROUP_CH)
    ab_re = ab_re.reshape(shape4)[..., 0]
    ab_im = ab_im.reshape(shape4)[..., 0]
    return ab_re, ab_im, bb_re.reshape(shape4), bb_im.reshape(shape4)


def _inproj_kernel(x_ref, gpre_ref, wlat_ref, wkr_ref, wwide_ref, gq_ref, gkv_ref,
                   cq_ref, ckv_ref, kr_ref, gm_ref, u_ref, gs_ref):
    x = x_ref[...]
    inv = lax.rsqrt(jnp.mean(x * x, axis=-1, keepdims=True) + NORM_EPS)
    h = (x * inv * gpre_ref[...]).astype(BF16)

    def proj(w_ref, lo, hi):
        return jnp.dot(h, w_ref[:, lo:hi], preferred_element_type=F32)

    def latent_norm(c, g_ref):
        inv_c = lax.rsqrt(jnp.mean(c * c, axis=-1, keepdims=True) + NORM_EPS)
        return (c * inv_c * g_ref[...]).astype(BF16)

    cq_ref[...] = latent_norm(proj(wlat_ref, 0, Q_LORA), gq_ref)
    ckv_ref[...] = latent_norm(proj(wlat_ref, Q_LORA, Q_LORA + KV_LORA), gkv_ref)
    kr_ref[...] = proj(wkr_ref, 0, LANES)
    gm_ref[...] = proj(wwide_ref, 0, D_MLA)
    u_ref[...] = proj(wwide_ref, D_MLA, D_MLA + D_SSM)
    gs_ref[...] = proj(wwide_ref, D_MLA + D_SSM, D_MLA + 2 * D_SSM)


def _inproj(x, layer, g_pre, w_lat, w_kr, w_wide, g_q, g_kv, tm):
    L = x.shape[0]
    row = lambda n: pl.BlockSpec((tm, n), lambda i: (i, 0))
    out_shape = (
        jax.ShapeDtypeStruct((L, Q_LORA), BF16),
        jax.ShapeDtypeStruct((L, KV_LORA), BF16),
        jax.ShapeDtypeStruct((L, LANES), F32),
        jax.ShapeDtypeStruct((L, D_MLA), F32),
        jax.ShapeDtypeStruct((L, D_SSM), F32),
        jax.ShapeDtypeStruct((L, D_SSM), F32),
    )
    return pl.pallas_call(
        _inproj_kernel,
        grid=(L // tm,),
        in_specs=[row(D_MODEL)] + [_layer(p, layer) for p in (g_pre, w_lat, w_kr, w_wide, g_q, g_kv)],
        out_specs=(row(Q_LORA), row(KV_LORA), row(LANES), row(D_MLA), row(D_SSM), row(D_SSM)),
        out_shape=out_shape,
        compiler_params=_params("parallel"),
        name="inproj",
    )(x, g_pre, w_lat, w_kr, w_wide, g_q, g_kv)


def _rope_kernel(pos_ref, freq_ref, cos_ref, sin_ref):
    ang = freq_ref[...] * pos_ref[...].astype(F32)
    cos = jnp.cos(ang)
    sin = jnp.sin(ang)
    cos_ref[...] = jnp.concatenate([cos, cos], axis=0)
    sin_ref[...] = jnp.concatenate([sin, sin], axis=0)


def _rope_tables(positions, tm):
    L = positions.shape[-1]
    inv_freq = ROPE_THETA ** (-jnp.arange(0, ROPE, 2, dtype=F32) / ROPE)
    col = pl.BlockSpec((ROPE, tm), lambda i: (0, i))
    return pl.pallas_call(
        _rope_kernel,
        grid=(L // tm,),
        in_specs=[pl.BlockSpec((1, tm), lambda i: (0, i)), _resident((ROPE // 2, 1))],
        out_specs=(col, col),
        out_shape=(jax.ShapeDtypeStruct((ROPE, L), F32), jax.ShapeDtypeStruct((ROPE, L), F32)),
        compiler_params=_params("parallel"),
        name="rope_tables",
    )(positions.reshape(1, L), inv_freq.reshape(ROPE // 2, 1))


def _qkv_kernel(cq_ref, ckv_ref, kr_ref, cost_ref, sint_ref,
                wq_ref, wk_ref, wv_ref, qt_ref, k_ref, vt_ref, *, q_scale):
    cq = cq_ref[...]
    ckv = ckv_ref[...]
    half = ROPE // 2
    cos_t = cost_ref[...]
    sin_t = sint_ref[...]

    kt = kr_ref[...].T[:ROPE]
    kt_rot = jnp.concatenate([-kt[half:], kt[:half]], axis=0)
    kt = kt * cos_t + kt_rot * sin_t
    k_rope = jnp.concatenate([kt, jnp.zeros_like(kt)], axis=0).T[:, :ROPE].astype(BF16)
    k_nope = jnp.dot(ckv, wk_ref[...], preferred_element_type=F32)

    pad_row = lax.broadcasted_iota(jnp.int32, (V_ROWS - DV, vt_ref.shape[-1]), 0)
    ones_rows = jnp.where(pad_row == 0, 1.0, 0.0).astype(BF16)

    tn = (((0,), (1,)), ((), ()))
    qt_all = lax.dot_general(wq_ref[...], cq, tn, preferred_element_type=F32)
    vt_all = lax.dot_general(wv_ref[...], ckv, tn, preferred_element_type=F32)
    for h in range(HEADS):
        qt = qt_all[h * QK:(h + 1) * QK]
        qr = qt[NOPE:]
        qr_rot = jnp.concatenate([-qr[half:], qr[:half]], axis=0)
        qr = qr * cos_t + qr_rot * sin_t
        qt_ref[h, :NOPE, :] = (qt[:NOPE] * q_scale).astype(BF16)
        qt_ref[h, NOPE:, :] = (qr * q_scale).astype(BF16)
        k_ref[h, :, :NOPE] = k_nope[:, h * NOPE:(h + 1) * NOPE].astype(BF16)
        k_ref[h, :, NOPE:] = k_rope
        vt_ref[h, 0, :DV, :] = vt_all[h * DV:(h + 1) * DV].astype(BF16)
        vt_ref[h, 0, DV:, :] = ones_rows


def _qkv(cq, ckv, kr, rope, layer, wq, wk, wv, tk):
    L = cq.shape[0]
    q_scale = math.log2(math.e) / math.sqrt(QK)
    row = lambda n: pl.BlockSpec((tk, n), lambda i: (i, 0))
    return pl.pallas_call(
        functools.partial(_qkv_kernel, q_scale=q_scale),
        grid=(L // tk,),
        in_specs=[row(Q_LORA), row(KV_LORA), row(LANES),
                  pl.BlockSpec((ROPE, tk), lambda i: (0, i)), pl.BlockSpec((ROPE, tk), lambda i: (0, i)),
                  _layer(wq, layer), _layer(wk, layer), _layer(wv, layer)],
        out_specs=(pl.BlockSpec((HEADS, QK, tk), lambda i: (0, 0, i)),
                   pl.BlockSpec((HEADS, tk, QK), lambda i: (0, i, 0)),
                   pl.BlockSpec((HEADS, 1, V_ROWS, tk), lambda i: (0, i, 0, 0))),
        out_shape=(jax.ShapeDtypeStruct((HEADS, QK, L), BF16),
                   jax.ShapeDtypeStruct((HEADS, L, QK), BF16),
                   jax.ShapeDtypeStruct((HEADS, L // tk, V_ROWS, tk), BF16)),
        compiler_params=_params("parallel"),
        name="qkv",
    )(cq, ckv, kr, *rope, wq, wk, wv)


def _attn_kernel(qt_ref, k_ref, vt_ref, gate_ref, o_ref, acc_ref, m_ref, alpha_ref, s_ref, p_ref, *, tq, tk):
    i = pl.program_id(1)
    m_ref[...] = jnp.full(m_ref.shape, -1e30, F32)
    acc_ref[...] = jnp.zeros(acc_ref.shape, F32)
    assert tq == 2 * tk
    n_chains = tq // MXU_COLS

    def cols(c):
        return slice(MXU_COLS * c, MXU_COLS * (c + 1))

    def n_keys(c, diag):
        return tk if diag is None else max(0, min(tk, MXU_COLS * (c + 1) - diag * tk))

    def scores(j, slot, c, diag):
        rows = n_keys(c, diag)
        if rows == 0:
            return
        kc = k_ref[0, pl.ds(pl.multiple_of(j * tk, tk), rows), :]
        s = jnp.dot(kc, qt_ref[0, :, cols(c)], preferred_element_type=F32)
        if diag is not None and diag * tk + rows - 1 > MXU_COLS * c:
            key = diag * tk + lax.broadcasted_iota(jnp.int32, s.shape, 0)
            qry = MXU_COLS * c + lax.broadcasted_iota(jnp.int32, s.shape, 1)
            s = jnp.where(key <= qry, s, -1e30)
        s_ref[slot, c, :rows, :] = s.astype(BF16)

    def softmax(slot, c, diag):
        rows = n_keys(c, diag)
        if rows == 0:
            return
        s = s_ref[slot, c, :rows, :]
        m_blk = jnp.max(s.reshape(rows // BF16_ROWS, BF16_ROWS, MXU_COLS), axis=0)
        m_blk = jnp.max(m_blk.astype(F32), axis=0, keepdims=True)
        m_prev = m_ref[:, cols(c)]
        m_new = jnp.maximum(m_prev, m_blk)
        alpha_ref[slot, :, cols(c)] = jnp.exp2(m_prev - m_new)
        p_ref[slot, c, :rows, :] = jnp.exp2(s - m_new.astype(BF16))
        m_ref[:, cols(c)] = m_new

    def pv(j, slot, c, diag):
        rows = n_keys(c, diag)
        if rows == 0:
            return
        vc = vt_ref[0, j, :, :rows]
        upd = jnp.dot(vc, p_ref[slot, c, :rows, :], preferred_element_type=F32)
        acc_ref[:, cols(c)] = alpha_ref[slot, :, cols(c)] * acc_ref[:, cols(c)] + upd

    def step(score=None, soft=None, pvs=None):
        for c in range(n_chains):
            if soft is not None:
                softmax(soft[1], c, soft[2])
            if score is not None:
                scores(score[0], score[1], c, score[2])
            if pvs is not None:
                pv(pvs[0], pvs[1], c, pvs[2])

    @pl.when(i > 0)
    def _():
        step(score=(0, 0, None))
        step(score=(1, 1, None), soft=(0, 0, None))

        def pair(t):
            j = 2 * t
            step(score=(j + 2, 0, None), soft=(j + 1, 1, None), pvs=(j, 0, None))
            step(score=(j + 3, 1, None), soft=(j + 2, 0, None), pvs=(j + 1, 1, None))

        def body(t, carry):
            pair(t)
            return carry

        lax.fori_loop(0, i - 1, body, 0)

        j = 2 * i - 2
        step(score=(j + 2, 0, 0), soft=(j + 1, 1, None), pvs=(j, 0, None))
        step(score=(j + 3, 1, 1), soft=(j + 2, 0, 0), pvs=(j + 1, 1, None))

    @pl.when(i == 0)
    def _():
        step(score=(0, 0, 0))
        step(score=(1, 1, 1), soft=(0, 0, 0))

    step(soft=(2 * i + 1, 1, 1), pvs=(2 * i, 0, 0))
    step(pvs=(2 * i + 1, 1, 1))

    out = (acc_ref[:DV, :] * (1.0 / acc_ref[DV:DV + 1, :])).T
    g = gate_ref[...]
    o_ref[...] = (out * (g * _sigmoid(g))).astype(BF16)


def _attention(qt, k, vt, gate, tq, tk):
    L = k.shape[1]
    return pl.pallas_call(
        functools.partial(_attn_kernel, tq=tq, tk=tk),
        grid=(HEADS, L // tq),
        in_specs=[pl.BlockSpec((1, QK, tq), lambda h, i: (h, 0, i)),
                  pl.BlockSpec((1, L, QK), lambda h, i: (h, 0, 0)),
                  pl.BlockSpec((1, L // tk, V_ROWS, tk), lambda h, i: (h, 0, 0, 0)),
                  pl.BlockSpec((tq, DV), lambda h, i: (i, h))],
        out_specs=pl.BlockSpec((tq, DV), lambda h, i: (i, h)),
        out_shape=jax.ShapeDtypeStruct((L, D_MLA), BF16),
        scratch_shapes=[pltpu.VMEM((V_ROWS, tq), F32), pltpu.VMEM((1, tq), F32),
                        pltpu.VMEM((2, 1, tq), F32),
                        pltpu.VMEM((2, tq // MXU_COLS, tk, MXU_COLS), BF16),
                        pltpu.VMEM((2, tq // MXU_COLS, tk, MXU_COLS), BF16)],
        compiler_params=_params("parallel", "arbitrary"),
        name="attention",
    )(qt, k, vt, gate)


def _gelu_tanh(x):
    c = math.sqrt(2.0 / math.pi)
    return 0.5 * x * (1.0 + jnp.tanh(c * (x + 0.044715 * (x * x * x))))


S5_CHUNK = 256
PITCH = 68
N_JOBS = 32
JOBS_PER_SLAB = N_JOBS // N_SLABS
TILES_PER_JOB = 2 * STATE_TILES // N_JOBS


def _job_tiles(slab, kk):
    is_im = kk // (JOBS_PER_SLAB // 2)
    q = kk % (JOBS_PER_SLAB // 2)
    tile0 = is_im * STATE_TILES + SLAB_TILES * slab + TILES_PER_JOB * q
    return 2 * slab + q // 2, tile0


def _s5_kernel(un_ref, up_ref, g_ref, wbj_ref, are_ref, aim_ref, wcj_ref, d_ref, wglu_ref,
               o_ref, hs0_ref, hs1_ref, st_ref, acc_ref, ub_ref, *, tt):
    c = pl.program_id(0)
    hs_refs = (hs0_ref, hs1_ref)

    def col_tile(tile):
        return pl.ds(tile, tt, stride=PITCH)

    def load_u_slabs(u_ref):
        u = u_ref[...]
        for s in range(D_SSM // LANES):
            ub_ref[s] = u[:, LANES * s:LANES * (s + 1)].astype(BF16)

    def b_job(slab, kk, slot):
        s, tile0 = _job_tiles(slab, kk)
        r = jnp.dot(ub_ref[s], wbj_ref[slab * JOBS_PER_SLAB + kk], preferred_element_type=F32)
        for k in range(TILES_PER_JOB):
            hs_refs[slot][col_tile(tile0 + k), :] = r[:, LANES * k:LANES * (k + 1)]

    def c_job(slab, kk, slot):
        _, tile0 = _job_tiles(slab, kk)
        lhs = jnp.concatenate([hs_refs[slot][col_tile(tile0 + k), :] for k in range(TILES_PER_JOB)],
                              axis=-1).astype(BF16)
        return jnp.dot(lhs, wcj_ref[slab * JOBS_PER_SLAB + kk], preferred_element_type=F32)

    @pl.when(c == 0)
    def _():
        st_ref[...] = jnp.zeros(st_ref.shape, F32)
        hs1_ref[...] = jnp.zeros(hs1_ref.shape, F32)
        load_u_slabs(up_ref)
        for ti in range(N_JOBS):
            b_job(ti // JOBS_PER_SLAB, ti % JOBS_PER_SLAB, 0)

    load_u_slabs(un_ref)
    a_re = [are_ref[i] for i in range(N_SLABS)]
    a_im = [aim_ref[i] for i in range(N_SLABS)]
    steps_per_job = tt // N_JOBS

    def run(slot):
        hs_ref = hs_refs[slot]

        def scan_steps(t0, st):
            st = list(st)
            for dt in range(steps_per_job):
                base = (t0 + dt) * PITCH
                for i in range(N_SLABS):
                    re_rows = pl.ds(base + SLAB_TILES * i, SLAB_TILES)
                    im_rows = pl.ds(base + STATE_TILES + SLAB_TILES * i, SLAB_TILES)
                    h_re, h_im = st[i], st[N_SLABS + i]
                    n_re = a_re[i] * h_re - a_im[i] * h_im + hs_ref[re_rows, :]
                    n_im = a_re[i] * h_im + a_im[i] * h_re + hs_ref[im_rows, :]
                    hs_ref[re_rows, :] = n_re
                    hs_ref[im_rows, :] = n_im
                    st[i], st[N_SLABS + i] = n_re, n_im
            return st

        st = [st_ref[i] for i in range(2 * N_SLABS)]
        for slab in range(N_SLABS):
            y = None
            for kk in range(JOBS_PER_SLAB):
                st = scan_steps((slab * JOBS_PER_SLAB + kk) * steps_per_job, st)
                part = c_job(slab, kk, 1 - slot)
                y = part if y is None else y + part
                b_job(slab, kk, 1 - slot)
            acc_ref[slab] = y
        for i in range(2 * N_SLABS):
            st_ref[i] = st[i]

    for slot in range(2):
        pl.when(c % 2 == slot)(functools.partial(run, slot))

    y = jnp.concatenate([acc_ref[i] for i in range(N_SLABS)], axis=-1) + d_ref[...] * up_ref[...]
    y = _gelu_tanh(y).astype(BF16)
    tile = D_SSM // N_SLABS
    for n in range(N_SLABS):
        val_cols = slice(tile * n, tile * (n + 1))
        gate_cols = slice(D_SSM + tile * n, D_SSM + tile * (n + 1))
        val = jnp.dot(y, wglu_ref[:, val_cols], preferred_element_type=F32)
        gate = jnp.dot(y, wglu_ref[:, gate_cols], preferred_element_type=F32)
        g = g_ref[:, val_cols]
        o_ref[:, val_cols] = (val * _sigmoid(gate) * (g * _sigmoid(g))).astype(BF16)


def _s5(u, gate, layer, wbj, a_re, a_im, wcj, d_skip, w_glu, tt):
    L = u.shape[0]
    n = L // tt
    nxt = pl.BlockSpec((tt, D_SSM), lambda c: (jnp.minimum(c + 1, n - 1), 0))
    prv = pl.BlockSpec((tt, D_SSM), lambda c: (jnp.maximum(c - 1, 0), 0))
    return pl.pallas_call(
        functools.partial(_s5_kernel, tt=tt),
        grid=(n + 1,),
        in_specs=[nxt, prv, prv] + [_layer(p, layer) for p in (wbj, a_re, a_im, wcj, d_skip, w_glu)],
        out_specs=prv,
        out_shape=jax.ShapeDtypeStruct((L, D_SSM), BF16),
        scratch_shapes=[pltpu.VMEM((tt * PITCH, LANES), F32), pltpu.VMEM((tt * PITCH, LANES), F32),
                        pltpu.VMEM((2 * N_SLABS, SUBLANES, LANES), F32),
                        pltpu.VMEM((N_SLABS, tt, D_SSM // N_SLABS), F32),
                        pltpu.VMEM((D_SSM // LANES, tt, LANES), BF16)],
        compiler_params=_params("arbitrary"),
        name="s5",
    )(u, u, gate, wbj, a_re, a_im, wcj, d_skip, w_glu)


def _outproj_kernel(mla_ref, ssm_ref, w_ref, gpost_ref, x_ref, o_ref):
    out = jnp.dot(mla_ref[...], w_ref[:D_MLA, :], preferred_element_type=F32)
    out += jnp.dot(ssm_ref[...], w_ref[D_MLA:, :], preferred_element_type=F32)
    inv = lax.rsqrt(jnp.mean(out * out, axis=-1, keepdims=True) + NORM_EPS)
    o_ref[...] = x_ref[...] + out * inv * gpost_ref[...]


def _outproj(mla, ssm, layer, w_out, g_post, x, tm):
    L = x.shape[0]
    row = lambda n: pl.BlockSpec((tm, n), lambda i: (i, 0))
    return pl.pallas_call(
        _outproj_kernel,
        grid=(L // tm,),
        in_specs=[row(D_MLA), row(D_SSM), _layer(w_out, layer), _layer(g_post, layer), row(D_MODEL)],
        out_specs=row(D_MODEL),
        out_shape=jax.ShapeDtypeStruct((L, D_MODEL), F32),
        compiler_params=_params("parallel"),
        name="outproj",
    )(mla, ssm, w_out, g_post, x)


def _split_in_weights(w_in):
    s2 = Q_LORA + KV_LORA
    s3 = s2 + ROPE
    w = lax.optimization_barrier(w_in.astype(BF16))
    w_kr = jnp.pad(w[..., s2:s3], ((0, 0), (0, 0), (0, LANES - ROPE)))
    return w[..., :s2], w_kr, w[..., s3:]


def _pack_b(bb_re, bb_im):
    depth = bb_re.shape[0]
    gpt = LANES // GROUP_CH
    col_group = jnp.arange(gpt * NSTATE) // NSTATE
    own_cols = col_group[None, None, None, None, :] == jnp.arange(gpt)[None, None, :, None, None]

    def pack(b):
        b = b.astype(BF16).reshape(depth, GROUPS // gpt, gpt, NSTATE, GROUP_CH).transpose(0, 1, 2, 4, 3)
        tiled = jnp.concatenate([b] * gpt, axis=-1)
        return jnp.where(own_cols, tiled, 0).reshape(depth, GROUPS // gpt, LANES, gpt * NSTATE)

    return jnp.concatenate([pack(bb_re), pack(bb_im)], axis=-1)


def _pack_c(c):
    depth = c.shape[0]
    gps = SLAB_TILES * LANES // NSTATE
    col_group = jnp.arange(gps * GROUP_CH) // GROUP_CH
    own_cols = col_group[None, None, None, None, :] == jnp.arange(gps)[None, None, :, None, None]
    c = c.astype(BF16).reshape(depth, N_SLABS, gps, GROUP_CH, NSTATE).transpose(0, 1, 2, 4, 3)
    tiled = jnp.concatenate([c] * gps, axis=-1)
    return jnp.where(own_cols, tiled, 0).reshape(depth, N_SLABS, gps * NSTATE, gps * GROUP_CH)


def kernel(x, positions, norm_pre, norm_post, w_in, q_norm, w_uq, kv_norm, w_ukv, ssm_a_re, ssm_a_im,
           ssm_b_re, ssm_b_im, ssm_c_re, ssm_c_im, ssm_d, ssm_log_step, w_glu, w_out):
    batch, L, _ = x.shape
    assert batch == 1
    depth = w_in.shape[0]
    tm = min(512, L)
    tq = min(2048, L)
    tk = tq // 2
    tt = S5_CHUNK

    rope = _rope_tables(positions, tk)

    w_lat, w_kr, w_wide = _split_in_weights(w_in)
    wq = w_uq.astype(BF16)
    w_ukv4 = w_ukv.astype(BF16).reshape(depth, KV_LORA, HEADS, NOPE + DV)
    wk = w_ukv4[..., :NOPE].reshape(depth, KV_LORA, HEADS * NOPE)
    wv = w_ukv4[..., NOPE:].reshape(depth, KV_LORA, HEADS * DV)
    w_glu_b = w_glu.astype(BF16)
    w_out_b = w_out.astype(BF16)

    ab_re, ab_im, bb_re, bb_im = _s5_prep(ssm_a_re, ssm_a_im, ssm_log_step, ssm_b_re, ssm_b_im)
    wb = _pack_b(bb_re, bb_im)
    half = wb.shape[-1] // 2
    job_cols = TILES_PER_JOB * LANES
    wbj = []
    for ti in range(N_JOBS):
        s, tile0 = _job_tiles(ti // JOBS_PER_SLAB, ti % JOBS_PER_SLAB)
        col = (tile0 // STATE_TILES) * half + (tile0 % STATE_TILES) * LANES - s * half
        wbj.append(wb[:, s, :, col:col + job_cols])
    wbj = jnp.stack(wbj, axis=1)
    slab_shape = (depth, N_SLABS, SUBLANES, LANES)
    a_re_s = ab_re.reshape(slab_shape)
    a_im_s = ab_im.reshape(slab_shape)
    wc_re = _pack_c(ssm_c_re).reshape(depth, N_SLABS, JOBS_PER_SLAB // 2, job_cols, -1)
    wc_im = _pack_c(-ssm_c_im).reshape(depth, N_SLABS, JOBS_PER_SLAB // 2, job_cols, -1)
    wcj = jnp.concatenate([wc_re, wc_im], axis=2).reshape(depth, N_JOBS, job_cols, -1)

    def vec(p):
        return p.reshape(depth, 1, -1)

    h = x[0]
    for i in range(depth):
        cq, ckv, kr, gate_mla, u, gate_ssm = _inproj(
            h, i, vec(norm_pre), w_lat, w_kr, w_wide, vec(q_norm), vec(kv_norm), tm)
        qt, k, vt = _qkv(cq, ckv, kr, rope, i, wq, wk, wv, tk)
        mla = _attention(qt, k, vt, gate_mla, tq, tk)
        ssm = _s5(u, gate_ssm, i, wbj, a_re_s, a_im_s, wcj, vec(ssm_d), w_glu_b, tt)
        h = _outproj(mla, ssm, i, w_out_b, vec(norm_post), h, tm)
    return h[None]
```

```python
import functools
import math

import jax
import jax.numpy as jnp
from jax import lax
from jax.experimental import pallas as pl
from jax.experimental.pallas import tpu as pltpu

F32 = jnp.float32
BF16 = jnp.bfloat16

D_MODEL = 2048
HEADS = 8
NOPE = 128
ROPE = 64
QK = NOPE + ROPE
DV = 128
V_ROWS = DV + 16
Q_LORA = 512
KV_LORA = 256
D_MLA = HEADS * DV
D_SSM = 1024
GROUP_CH = 16
GROUPS = D_SSM // GROUP_CH
NSTATE = 64
ROPE_THETA = 10000.0
NORM_EPS = 1e-6
LANES = 128
SUBLANES = 8
MXU_COLS = 256
BF16_ROWS = 16
STATE_TILES = GROUPS * NSTATE // LANES
SLAB_TILES = SUBLANES
N_SLABS = STATE_TILES // SLAB_TILES
VMEM_LIMIT = 56 * 1024 * 1024


def _resident(shape):
    zeros = (0,) * len(shape)
    return pl.BlockSpec(shape, lambda *_: zeros, pipeline_mode=pl.Buffered(1))


def _layer(stack, layer):
    shape = stack.shape[1:]
    index = (layer,) + (0,) * len(shape)
    return pl.BlockSpec((None,) + shape, lambda *_: index, pipeline_mode=pl.Buffered(1))


def _sigmoid(x):
    return 1.0 / (1.0 + jnp.exp(-x))


def _params(*sem):
    return pltpu.CompilerParams(dimension_semantics=sem, vmem_limit_bytes=VMEM_LIMIT)


def _s5_prep_kernel(are_ref, aim_ref, lstep_ref, bre_ref, bim_ref,
                    abre_ref, abim_ref, bbre_ref, bbim_ref):
    a_re = are_ref[...]
    a_im = aim_ref[...]
    step = jnp.exp(lstep_ref[...])
    mag = jnp.exp(step * a_re)
    ab_re = mag * jnp.cos(step * a_im)
    ab_im = mag * jnp.sin(step * a_im)
    n_re = ab_re - 1.0
    den = a_re * a_re + a_im * a_im
    z_re = (n_re * a_re + ab_im * a_im) / den
    z_im = (ab_im * a_re - n_re * a_im) / den
    b_re = bre_ref[...]
    b_im = bim_ref[...]
    abre_ref[...] = ab_re
    abim_ref[...] = ab_im
    bbre_ref[...] = z_re * b_re - z_im * b_im
    bbim_ref[...] = z_re * b_im + z_im * b_re


def _s5_prep(a_re, a_im, log_step, b_re, b_im):
    depth = a_re.shape[0]
    rows = depth * GROUPS
    cols = NSTATE * GROUP_CH

    def expand(a):
        return jnp.broadcast_to(a[..., None], (depth, GROUPS, NSTATE, GROUP_CH)).reshape(rows, cols)

    step = jnp.broadcast_to(log_step[..., None], (depth, GROUPS, cols)).reshape(rows, cols)
    sds = jax.ShapeDtypeStruct((rows, cols), F32)
    ab_re, ab_im, bb_re, bb_im = pl.pallas_call(
        _s5_prep_kernel, out_shape=(sds, sds, sds, sds), name="s5_prep",
    )(expand(a_re), expand(a_im), step, b_re.reshape(rows, cols), b_im.reshape(rows, cols))
    shape4 = (depth, GROUPS, NSTATE, GROUP_CH)
    ab_re = ab_re.reshape(shape4)[..., 0]
    ab_im = ab_im.reshape(shape4)[..., 0]
    return ab_re, ab_im, bb_re.reshape(shape4), bb_im.reshape(shape4)


def _inproj_kernel(x_ref, gpre_ref, wlat_ref, wkr_ref, wwide_ref, gq_ref, gkv_ref,
                   cq_ref, ckv_ref, kr_ref, gm_ref, u_ref, gs_ref):
    x = x_ref[...]
    inv = lax.rsqrt(jnp.mean(x * x, axis=-1, keepdims=True) + NORM_EPS)
    h = (x * inv * gpre_ref[...]).astype(BF16)

    def proj(w_ref, lo, hi):
        return jnp.dot(h, w_ref[:, lo:hi], preferred_element_type=F32)

    def latent_norm(c, g_ref):
        inv_c = lax.rsqrt(jnp.mean(c * c, axis=-1, keepdims=True) + NORM_EPS)
        return (c * inv_c * g_ref[...]).astype(BF16)

    cq_ref[...] = latent_norm(proj(wlat_ref, 0, Q_LORA), gq_ref)
    ckv_ref[...] = latent_norm(proj(wlat_ref, Q_LORA, Q_LORA + KV_LORA), gkv_ref)
    kr_ref[...] = proj(wkr_ref, 0, LANES)
    gm_ref[...] = proj(wwide_ref, 0, D_MLA)
    u_ref[...] = proj(wwide_ref, D_MLA, D_MLA + D_SSM)
    gs_ref[...] = proj(wwide_ref, D_MLA + D_SSM, D_MLA + 2 * D_SSM)


def _inproj(x, layer, g_pre, w_lat, w_kr, w_wide, g_q, g_kv, tm):
    L = x.shape[0]
    row = lambda n: pl.BlockSpec((tm, n), lambda i: (i, 0))
    out_shape = (
        jax.ShapeDtypeStruct((L, Q_LORA), BF16),
        jax.ShapeDtypeStruct((L, KV_LORA), BF16),
        jax.ShapeDtypeStruct((L, LANES), F32),
        jax.ShapeDtypeStruct((L, D_MLA), F32),
        jax.ShapeDtypeStruct((L, D_SSM), F32),
        jax.ShapeDtypeStruct((L, D_SSM), F32),
    )
    return pl.pallas_call(
        _inproj_kernel,
        grid=(L // tm,),
        in_specs=[row(D_MODEL)] + [_layer(p, layer) for p in (g_pre, w_lat, w_kr, w_wide, g_q, g_kv)],
        out_specs=(row(Q_LORA), row(KV_LORA), row(LANES), row(D_MLA), row(D_SSM), row(D_SSM)),
        out_shape=out_shape,
        compiler_params=_params("parallel"),
        name="inproj",
    )(x, g_pre, w_lat, w_kr, w_wide, g_q, g_kv)


def _rope_kernel(pos_ref, freq_ref, cos_ref, sin_ref):
    ang = freq_ref[...] * pos_ref[...].astype(F32)
    cos = jnp.cos(ang)
    sin = jnp.sin(ang)
    cos_ref[...] = jnp.concatenate([cos, cos], axis=0)
    sin_ref[...] = jnp.concatenate([sin, sin], axis=0)


def _rope_tables(positions, tm):
    L = positions.shape[-1]
    inv_freq = ROPE_THETA ** (-jnp.arange(0, ROPE, 2, dtype=F32) / ROPE)
    col = pl.BlockSpec((ROPE, tm), lambda i: (0, i))
    return pl.pallas_call(
        _rope_kernel,
        grid=(L // tm,),
        in_specs=[pl.BlockSpec((1, tm), lambda i: (0, i)), _resident((ROPE // 2, 1))],
        out_specs=(col, col),
        out_shape=(jax.ShapeDtypeStruct((ROPE, L), F32), jax.ShapeDtypeStruct((ROPE, L), F32)),
        compiler_params=_params("parallel"),
        name="rope_tables",
    )(positions.reshape(1, L), inv_freq.reshape(ROPE // 2, 1))


def _qkv_kernel(cq_ref, ckv_ref, kr_ref, cost_ref, sint_ref,
                wq_ref, wk_ref, wv_ref, qt_ref, k_ref, vt_ref, *, q_scale):
    cq = cq_ref[...]
    ckv = ckv_ref[...]
    half = ROPE // 2
    cos_t = cost_ref[...]
    sin_t = sint_ref[...]

    kt = kr_ref[...].T[:ROPE]
    kt_rot = jnp.concatenate([-kt[half:], kt[:half]], axis=0)
    kt = kt * cos_t + kt_rot * sin_t
    k_rope = jnp.concatenate([kt, jnp.zeros_like(kt)], axis=0).T[:, :ROPE].astype(BF16)
    k_nope = jnp.dot(ckv, wk_ref[...], preferred_element_type=F32)

    pad_row = lax.broadcasted_iota(jnp.int32, (V_ROWS - DV, vt_ref.shape[-1]), 0)
    ones_rows = jnp.where(pad_row == 0, 1.0, 0.0).astype(BF16)

    tn = (((0,), (1,)), ((), ()))
    qt_all = lax.dot_general(wq_ref[...], cq, tn, preferred_element_type=F32)
    vt_all = lax.dot_general(wv_ref[...], ckv, tn, preferred_element_type=F32)
    for h in range(HEADS):
        qt = qt_all[h * QK:(h + 1) * QK]
        qr = qt[NOPE:]
        qr_rot = jnp.concatenate([-qr[half:], qr[:half]], axis=0)
        qr = qr * cos_t + qr_rot * sin_t
        qt_ref[h, :NOPE, :] = (qt[:NOPE] * q_scale).astype(BF16)
        qt_ref[h, NOPE:, :] = (qr * q_scale).astype(BF16)
        k_ref[h, :, :NOPE] = k_nope[:, h * NOPE:(h + 1) * NOPE].astype(BF16)
        k_ref[h, :, NOPE:] = k_rope
        vt_ref[h, 0, :DV, :] = vt_all[h * DV:(h + 1) * DV].astype(BF16)
        vt_ref[h, 0, DV:, :] = ones_rows


def _qkv(cq, ckv, kr, rope, layer, wq, wk, wv, tk):
    L = cq.shape[0]
    q_scale = math.log2(math.e) / math.sqrt(QK)
    row = lambda n: pl.BlockSpec((tk, n), lambda i: (i, 0))
    return pl.pallas_call(
        functools.partial(_qkv_kernel, q_scale=q_scale),
        grid=(L // tk,),
        in_specs=[row(Q_LORA), row(KV_LORA), row(LANES),
                  pl.BlockSpec((ROPE, tk), lambda i: (0, i)), pl.BlockSpec((ROPE, tk), lambda i: (0, i)),
                  _layer(wq, layer), _layer(wk, layer), _layer(wv, layer)],
        out_specs=(pl.BlockSpec((HEADS, QK, tk), lambda i: (0, 0, i)),
                   pl.BlockSpec((HEADS, tk, QK), lambda i: (0, i, 0)),
                   pl.BlockSpec((HEADS, 1, V_ROWS, tk), lambda i: (0, i, 0, 0))),
        out_shape=(jax.ShapeDtypeStruct((HEADS, QK, L), BF16),
                   jax.ShapeDtypeStruct((HEADS, L, QK), BF16),
                   jax.ShapeDtypeStruct((HEADS, L // tk, V_ROWS, tk), BF16)),
        compiler_params=_params("parallel"),
        name="qkv",
    )(cq, ckv, kr, *rope, wq, wk, wv)


def _attn_kernel(qt_ref, k_ref, vt_ref, gate_ref, o_ref, acc_ref, m_ref, alpha_ref, s_ref, p_ref, *, tq, tk):
    i = pl.program_id(1)
    m_ref[...] = jnp.full(m_ref.shape, -1e30, F32)
    acc_ref[...] = jnp.zeros(acc_ref.shape, F32)
    assert tq == 2 * tk
    n_chains = tq // MXU_COLS

    def cols(c):
        return slice(MXU_COLS * c, MXU_COLS * (c + 1))

    def n_keys(c, diag):
        return tk if diag is None else max(0, min(tk, MXU_COLS * (c + 1) - diag * tk))

    def scores(j, slot, c, diag):
        rows = n_keys(c, diag)
        if rows == 0:
            return
        kc = k_ref[0, pl.ds(pl.multiple_of(j * tk, tk), rows), :]
        s = jnp.dot(kc, qt_ref[0, :, cols(c)], preferred_element_type=F32)
        if diag is not None and diag * tk + rows - 1 > MXU_COLS * c:
            key = diag * tk + lax.broadcasted_iota(jnp.int32, s.shape, 0)
            qry = MXU_COLS * c + lax.broadcasted_iota(jnp.int32, s.shape, 1)
            s = jnp.where(key <= qry, s, -1e30)
        s_ref[slot, c, :rows, :] = s.astype(BF16)

    def softmax(slot, c, diag):
        rows = n_keys(c, diag)
        if rows == 0:
            return
        s = s_ref[slot, c, :rows, :]
        m_blk = jnp.max(s.reshape(rows // BF16_ROWS, BF16_ROWS, MXU_COLS), axis=0)
        m_blk = jnp.max(m_blk.astype(F32), axis=0, keepdims=True)
        m_prev = m_ref[:, cols(c)]
        m_new = jnp.maximum(m_prev, m_blk)
        alpha_ref[slot, :, cols(c)] = jnp.exp2(m_prev - m_new)
        p_ref[slot, c, :rows, :] = jnp.exp2(s - m_new.astype(BF16))
        m_ref[:, cols(c)] = m_new

    def pv(j, slot, c, diag):
        rows = n_keys(c, diag)
        if rows == 0:
            return
        vc = vt_ref[0, j, :, :rows]
        upd = jnp.dot(vc, p_ref[slot, c, :rows, :], preferred_element_type=F32)
        acc_ref[:, cols(c)] = alpha_ref[slot, :, cols(c)] * acc_ref[:, cols(c)] + upd

    def step(score=None, soft=None, pvs=None):
        for c in range(n_chains):
            if soft is not None:
                softmax(soft[1], c, soft[2])
            if score is not None:
                scores(score[0], score[1], c, score[2])
            if pvs is not None:
                pv(pvs[0], pvs[1], c, pvs[2])

    @pl.when(i > 0)
    def _():
        step(score=(0, 0, None))
        step(score=(1, 1, None), soft=(0, 0, None))

        def pair(t):
            j = 2 * t
            step(score=(j + 2, 0, None), soft=(j + 1, 1, None), pvs=(j, 0, None))
            step(score=(j + 3, 1, None), soft=(j + 2, 0, None), pvs=(j + 1, 1, None))

        def body(t, carry):
            pair(t)
            return carry

        lax.fori_loop(0, i - 1, body, 0)

        j = 2 * i - 2
        step(score=(j + 2, 0, 0), soft=(j + 1, 1, None), pvs=(j, 0, None))
        step(score=(j + 3, 1, 1), soft=(j + 2, 0, 0), pvs=(j + 1, 1, None))

    @pl.when(i == 0)
    def _():
        step(score=(0, 0, 0))
        step(score=(1, 1, 1), soft=(0, 0, 0))

    step(soft=(2 * i + 1, 1, 1), pvs=(2 * i, 0, 0))
    step(pvs=(2 * i + 1, 1, 1))

    out = (acc_ref[:DV, :] * (1.0 / acc_ref[DV:DV + 1, :])).T
    g = gate_ref[...]
    o_ref[...] = (out * (g * _sigmoid(g))).astype(BF16)


def _attention(qt, k, vt, gate, tq, tk):
    L = k.shape[1]
    return pl.pallas_call(
        functools.partial(_attn_kernel, tq=tq, tk=tk),
        grid=(HEADS, L // tq),
        in_specs=[pl.BlockSpec((1, QK, tq), lambda h, i: (h, 0, i)),
                  pl.BlockSpec((1, L, QK), lambda h, i: (h, 0, 0)),
                  pl.BlockSpec((1, L // tk, V_ROWS, tk), lambda h, i: (h, 0, 0, 0)),
                  pl.BlockSpec((tq, DV), lambda h, i: (i, h))],
        out_specs=pl.BlockSpec((tq, DV), lambda h, i: (i, h)),
        out_shape=jax.ShapeDtypeStruct((L, D_MLA), BF16),
        scratch_shapes=[pltpu.VMEM((V_ROWS, tq), F32), pltpu.VMEM((1, tq), F32),
                        pltpu.VMEM((2, 1, tq), F32),
                        pltpu.VMEM((2, tq // MXU_COLS, tk, MXU_COLS), BF16),
                        pltpu.VMEM((2, tq // MXU_COLS, tk, MXU_COLS), BF16)],
        compiler_params=_params("parallel", "arbitrary"),
        name="attention",
    )(qt, k, vt, gate)


def _gelu_tanh(x):
    c = math.sqrt(2.0 / math.pi)
    return 0.5 * x * (1.0 + jnp.tanh(c * (x + 0.044715 * (x * x * x))))


S5_CHUNK = 256
PITCH = 68
N_JOBS = 32
JOBS_PER_SLAB = N_JOBS // N_SLABS
TILES_PER_JOB = 2 * STATE_TILES // N_JOBS


def _job_tiles(slab, kk):
    is_im = kk // (JOBS_PER_SLAB // 2)
    q = kk % (JOBS_PER_SLAB // 2)
    tile0 = is_im * STATE_TILES + SLAB_TILES * slab + TILES_PER_JOB * q
    return 2 * slab + q // 2, tile0


def _s5_kernel(un_ref, up_ref, g_ref, wbj_ref, are_ref, aim_ref, wcj_ref, d_ref, wglu_ref,
               o_ref, hs0_ref, hs1_ref, st_ref, acc_ref, ub_ref, *, tt):
    c = pl.program_id(0)
    hs_refs = (hs0_ref, hs1_ref)

    def col_tile(tile):
        return pl.ds(tile, tt, stride=PITCH)

    def load_u_slabs(u_ref):
        u = u_ref[...]
        for s in range(D_SSM // LANES):
            ub_ref[s] = u[:, LANES * s:LANES * (s + 1)].astype(BF16)

    def b_job(slab, kk, slot):
        s, tile0 = _job_tiles(slab, kk)
        r = jnp.dot(ub_ref[s], wbj_ref[slab * JOBS_PER_SLAB + kk], preferred_element_type=F32)
        for k in range(TILES_PER_JOB):
            hs_refs[slot][col_tile(tile0 + k), :] = r[:, LANES * k:LANES * (k + 1)]

    def c_job(slab, kk, slot):
        _, tile0 = _job_tiles(slab, kk)
        lhs = jnp.concatenate([hs_refs[slot][col_tile(tile0 + k), :] for k in range(TILES_PER_JOB)],
                              axis=-1).astype(BF16)
        return jnp.dot(lhs, wcj_ref[slab * JOBS_PER_SLAB + kk], preferred_element_type=F32)

    @pl.when(c == 0)
    def _():
        st_ref[...] = jnp.zeros(st_ref.shape, F32)
        hs1_ref[...] = jnp.zeros(hs1_ref.shape, F32)
        load_u_slabs(up_ref)
        for ti in range(N_JOBS):
            b_job(ti // JOBS_PER_SLAB, ti % JOBS_PER_SLAB, 0)

    load_u_slabs(un_ref)
    a_re = [are_ref[i] for i in range(N_SLABS)]
    a_im = [aim_ref[i] for i in range(N_SLABS)]
    steps_per_job = tt // N_JOBS

    def run(slot):
        hs_ref = hs_refs[slot]

        def scan_steps(t0, st):
            st = list(st)
            for dt in range(steps_per_job):
                base = (t0 + dt) * PITCH
                for i in range(N_SLABS):
                    re_rows = pl.ds(base + SLAB_TILES * i, SLAB_TILES)
                    im_rows = pl.ds(base + STATE_TILES + SLAB_TILES * i, SLAB_TILES)
                    h_re, h_im = st[i], st[N_SLABS + i]
                    n_re = a_re[i] * h_re - a_im[i] * h_im + hs_ref[re_rows, :]
                    n_im = a_re[i] * h_im + a_im[i] * h_re + hs_ref[im_rows, :]
                    hs_ref[re_rows, :] = n_re
                    hs_ref[im_rows, :] = n_im
                    st[i], st[N_SLABS + i] = n_re, n_im
            return st

        st = [st_ref[i] for i in range(2 * N_SLABS)]
        for slab in range(N_SLABS):
            y = None
            for kk in range(JOBS_PER_SLAB):
                st = scan_steps((slab * JOBS_PER_SLAB + kk) * steps_per_job, st)
                part = c_job(slab, kk, 1 - slot)
                y = part if y is None else y + part
                b_job(slab, kk, 1 - slot)
            acc_ref[slab] = y
        for i in range(2 * N_SLABS):
            st_ref[i] = st[i]

    for slot in range(2):
        pl.when(c % 2 == slot)(functools.partial(run, slot))

    y = jnp.concatenate([acc_ref[i] for i in range(N_SLABS)], axis=-1) + d_ref[...] * up_ref[...]
    y = _gelu_tanh(y).astype(BF16)
    tile = D_SSM // N_SLABS
    for n in range(N_SLABS):
        val_cols = slice(tile * n, tile * (n + 1))
        gate_cols = slice(D_SSM + tile * n, D_SSM + tile * (n + 1))
        val = jnp.dot(y, wglu_ref[:, val_cols], preferred_element_type=F32)
        gate = jnp.dot(y, wglu_ref[:, gate_cols], preferred_element_type=F32)
        g = g_ref[:, val_cols]
        o_ref[:, val_cols] = (val * _sigmoid(gate) * (g * _sigmoid(g))).astype(BF16)


def _s5(u, gate, layer, wbj, a_re, a_im, wcj, d_skip, w_glu, tt):
    L = u.shape[0]
    n = L // tt
    nxt = pl.BlockSpec((tt, D_SSM), lambda c: (jnp.minimum(c + 1, n - 1), 0))
    prv = pl.BlockSpec((tt, D_SSM), lambda c: (jnp.maximum(c - 1, 0), 0))
    return pl.pallas_call(
        functools.partial(_s5_kernel, tt=tt),
        grid=(n + 1,),
        in_specs=[nxt, prv, prv] + [_layer(p, layer) for p in (wbj, a_re, a_im, wcj, d_skip, w_glu)],
        out_specs=prv,
        out_shape=jax.ShapeDtypeStruct((L, D_SSM), BF16),
        scratch_shapes=[pltpu.VMEM((tt * PITCH, LANES), F32), pltpu.VMEM((tt * PITCH, LANES), F32),
                        pltpu.VMEM((2 * N_SLABS, SUBLANES, LANES), F32),
                        pltpu.VMEM((N_SLABS, tt, D_SSM // N_SLABS), F32),
                        pltpu.VMEM((D_SSM // LANES, tt, LANES), BF16)],
        compiler_params=_params("arbitrary"),
        name="s5",
    )(u, u, gate, wbj, a_re, a_im, wcj, d_skip, w_glu)


def _outproj_kernel(mla_ref, ssm_ref, w_ref, gpost_ref, x_ref, o_ref):
    out = jnp.dot(mla_ref[...], w_ref[:D_MLA, :], preferred_element_type=F32)
    out += jnp.dot(ssm_ref[...], w_ref[D_MLA:, :], preferred_element_type=F32)
    inv = lax.rsqrt(jnp.mean(out * out, axis=-1, keepdims=True) + NORM_EPS)
    o_ref[...] = x_ref[...] + out * inv * gpost_ref[...]


def _outproj(mla, ssm, layer, w_out, g_post, x, tm):
    L = x.shape[0]
    row = lambda n: pl.BlockSpec((tm, n), lambda i: (i, 0))
    return pl.pallas_call(
        _outproj_kernel,
        grid=(L // tm,),
        in_specs=[row(D_MLA), row(D_SSM), _layer(w_out, layer), _layer(g_post, layer), row(D_MODEL)],
        out_specs=row(D_MODEL),
        out_shape=jax.ShapeDtypeStruct((L, D_MODEL), F32),
        compiler_params=_params("parallel"),
        name="outproj",
    )(mla, ssm, w_out, g_post, x)


def _split_in_kernel(w_ref, lat_ref, kr_ref, wide_ref):
    s2 = Q_LORA + KV_LORA
    s3 = s2 + ROPE
    w = w_ref[...]
    lat_ref[...] = w[:, :s2].astype(BF16)
    kr = w[:, s2:s3].astype(BF16)
    kr_ref[...] = jnp.concatenate([kr, jnp.zeros((kr.shape[0], LANES - ROPE), BF16)], axis=-1)
    wide_ref[...] = w[:, s3:].astype(BF16)


def _split_in_weights(w_in, rows=256):
    depth, d_model, d_in = w_in.shape
    n_lat = Q_LORA + KV_LORA
    n_wide = d_in - n_lat - ROPE
    blk = lambda n: pl.BlockSpec((None, rows, n), lambda d, i: (d, i, 0))
    return pl.pallas_call(
        _split_in_kernel,
        grid=(depth, d_model // rows),
        in_specs=[blk(d_in)],
        out_specs=(blk(n_lat), blk(LANES), blk(n_wide)),
        out_shape=(jax.ShapeDtypeStruct((depth, d_model, n_lat), BF16),
                   jax.ShapeDtypeStruct((depth, d_model, LANES), BF16),
                   jax.ShapeDtypeStruct((depth, d_model, n_wide), BF16)),
        compiler_params=_params("parallel", "parallel"),
        name="split_w_in",
    )(w_in)


def _pack_b(bb_re, bb_im):
    depth = bb_re.shape[0]
    gpt = LANES // GROUP_CH
    same_group = jnp.eye(gpt, dtype=bool)[None, None, :, None, :, None]

    def pack(b):
        b = b.astype(BF16).reshape(depth, GROUPS // gpt, gpt, NSTATE, GROUP_CH).transpose(0, 1, 2, 4, 3)
        blocks = jnp.where(same_group, b[:, :, :, :, None, :], 0)
        return blocks.reshape(depth, GROUPS // gpt, LANES, gpt * NSTATE)

    return jnp.concatenate([pack(bb_re), pack(bb_im)], axis=-1)


def _pack_c(c):
    depth = c.shape[0]
    gps = SLAB_TILES * LANES // NSTATE
    same_group = jnp.eye(gps, dtype=bool)[None, None, :, None, :, None]
    c = c.astype(BF16).reshape(depth, N_SLABS, gps, GROUP_CH, NSTATE).transpose(0, 1, 2, 4, 3)
    blocks = jnp.where(same_group, c[:, :, :, :, None, :], 0)
    return blocks.reshape(depth, N_SLABS, gps * NSTATE, gps * GROUP_CH)


def kernel(x, positions, norm_pre, norm_post, w_in, q_norm, w_uq, kv_norm, w_ukv, ssm_a_re, ssm_a_im,
           ssm_b_re, ssm_b_im, ssm_c_re, ssm_c_im, ssm_d, ssm_log_step, w_glu, w_out):
    batch, L, _ = x.shape
    assert batch == 1
    depth = w_in.shape[0]
    tm = min(512, L)
    tq = min(2048, L)
    tk = tq // 2
    tt = S5_CHUNK

    rope = _rope_tables(positions, tk)

    w_lat, w_kr, w_wide = _split_in_weights(w_in)
    wq = w_uq.astype(BF16)
    w_ukv4 = w_ukv.astype(BF16).reshape(depth, KV_LORA, HEADS, NOPE + DV)
    wk = w_ukv4[..., :NOPE].reshape(depth, KV_LORA, HEADS * NOPE)
    wv = w_ukv4[..., NOPE:].reshape(depth, KV_LORA, HEADS * DV)
    w_glu_b = w_glu.astype(BF16)
    w_out_b = w_out.astype(BF16)

    ab_re, ab_im, bb_re, bb_im = _s5_prep(ssm_a_re, ssm_a_im, ssm_log_step, ssm_b_re, ssm_b_im)
    wb = _pack_b(bb_re, bb_im)
    half = wb.shape[-1] // 2
    job_cols = TILES_PER_JOB * LANES
    wbj = []
    for ti in range(N_JOBS):
        s, tile0 = _job_tiles(ti // JOBS_PER_SLAB, ti % JOBS_PER_SLAB)
        col = (tile0 // STATE_TILES) * half + (tile0 % STATE_TILES) * LANES - s * half
        wbj.append(wb[:, s, :, col:col + job_cols])
    wbj = jnp.stack(wbj, axis=1)
    slab_shape = (depth, N_SLABS, SUBLANES, LANES)
    a_re_s = ab_re.reshape(slab_shape)
    a_im_s = ab_im.reshape(slab_shape)
    wc_re = _pack_c(ssm_c_re).reshape(depth, N_SLABS, JOBS_PER_SLAB // 2, job_cols, -1)
    wc_im = _pack_c(-ssm_c_im).reshape(depth, N_SLABS, JOBS_PER_SLAB // 2, job_cols, -1)
    wcj = jnp.concatenate([wc_re, wc_im], axis=2).reshape(depth, N_JOBS, job_cols, -1)

    def vec(p):
        return p.reshape(depth, 1, -1)

    h = x[0]
    for i in range(depth):
        cq, ckv, kr, gate_mla, u, gate_ssm = _inproj(
            h, i, vec(norm_pre), w_lat, w_kr, w_wide, vec(q_norm), vec(kv_norm), tm)
        qt, k, vt = _qkv(cq, ckv, kr, rope, i, wq, wk, wv, tk)
        mla = _attention(qt, k, vt, gate_mla, tq, tk)
        ssm = _s5(u, gate_ssm, i, wbj, a_re_s, a_im_s, wcj, vec(ssm_d), w_glu_b, tt)
        h = _outproj(mla, ssm, i, w_out_b, vec(norm_post), h, tm)
    return h[None]
```

```python
import functools
import math

import jax
import jax.numpy as jnp
from jax import lax
from jax.experimental import pallas as pl
from jax.experimental.pallas import tpu as pltpu

F32 = jnp.float32
BF16 = jnp.bfloat16

D_MODEL = 2048
HEADS = 8
NOPE = 128
ROPE = 64
QK = NOPE + ROPE
DV = 128
V_ROWS = DV + 16
Q_LORA = 512
KV_LORA = 256
D_MLA = HEADS * DV
D_SSM = 1024
GROUP_CH = 16
GROUPS = D_SSM // GROUP_CH
NSTATE = 64
ROPE_THETA = 10000.0
NORM_EPS = 1e-6
LANES = 128
SUBLANES = 8
MXU_COLS = 256
BF16_ROWS = 16
STATE_TILES = GROUPS * NSTATE // LANES
SLAB_TILES = SUBLANES
N_SLABS = STATE_TILES // SLAB_TILES
VMEM_LIMIT = 56 * 1024 * 1024


def _resident(shape):
    zeros = (0,) * len(shape)
    return pl.BlockSpec(shape, lambda *_: zeros, pipeline_mode=pl.Buffered(1))


def _layer(stack, layer):
    shape = stack.shape[1:]
    index = (layer,) + (0,) * len(shape)
    return pl.BlockSpec((None,) + shape, lambda *_: index, pipeline_mode=pl.Buffered(1))


def _sigmoid(x):
    return 1.0 / (1.0 + jnp.exp(-x))


def _params(*sem):
    return pltpu.CompilerParams(dimension_semantics=sem, vmem_limit_bytes=VMEM_LIMIT)


def _s5_prep_kernel(are_ref, aim_ref, lstep_ref, bre_ref, bim_ref,
                    abre_ref, abim_ref, bbre_ref, bbim_ref):
    a_re = are_ref[...]
    a_im = aim_ref[...]
    step = jnp.exp(lstep_ref[...])
    mag = jnp.exp(step * a_re)
    ab_re = mag * jnp.cos(step * a_im)
    ab_im = mag * jnp.sin(step * a_im)
    n_re = ab_re - 1.0
    den = a_re * a_re + a_im * a_im
    z_re = (n_re * a_re + ab_im * a_im) / den
    z_im = (ab_im * a_re - n_re * a_im) / den
    b_re = bre_ref[...]
    b_im = bim_ref[...]
    abre_ref[...] = ab_re
    abim_ref[...] = ab_im
    bbre_ref[...] = z_re * b_re - z_im * b_im
    bbim_ref[...] = z_re * b_im + z_im * b_re


def _s5_prep(a_re, a_im, log_step, b_re, b_im):
    depth = a_re.shape[0]
    rows = depth * GROUPS
    cols = NSTATE * GROUP_CH

    def expand(a):
        return jnp.broadcast_to(a[..., None], (depth, GROUPS, NSTATE, GROUP_CH)).reshape(rows, cols)

    step = jnp.broadcast_to(log_step[..., None], (depth, GROUPS, cols)).reshape(rows, cols)
    sds = jax.ShapeDtypeStruct((rows, cols), F32)
    ab_re, ab_im, bb_re, bb_im = pl.pallas_call(
        _s5_prep_kernel, out_shape=(sds, sds, sds, sds), name="s5_prep",
    )(expand(a_re), expand(a_im), step, b_re.reshape(rows, cols), b_im.reshape(rows, cols))
    shape4 = (depth, GROUPS, NSTATE, GROUP_CH)
    ab_re = ab_re.reshape(shape4)[..., 0]
    ab_im = ab_im.reshape(shape4)[..., 0]
    return ab_re, ab_im, bb_re.reshape(shape4), bb_im.reshape(shape4)


def _inproj_kernel(x_ref, gpre_ref, wlat_ref, wkr_ref, wwide_ref, gq_ref, gkv_ref,
                   cq_ref, ckv_ref, kr_ref, gm_ref, u_ref, gs_ref):
    x = x_ref[...]
    inv = lax.rsqrt(jnp.mean(x * x, axis=-1, keepdims=True) + NORM_EPS)
    h = (x * inv * gpre_ref[...]).astype(BF16)

    def proj(w_ref, lo, hi):
        return jnp.dot(h, w_ref[:, lo:hi], preferred_element_type=F32)

    def latent_norm(c, g_ref):
        inv_c = lax.rsqrt(jnp.mean(c * c, axis=-1, keepdims=True) + NORM_EPS)
        return (c * inv_c * g_ref[...]).astype(BF16)

    cq_ref[...] = latent_norm(proj(wlat_ref, 0, Q_LORA), gq_ref)
    ckv_ref[...] = latent_norm(proj(wlat_ref, Q_LORA, Q_LORA + KV_LORA), gkv_ref)
    kr_ref[...] = proj(wkr_ref, 0, LANES)
    gm_ref[...] = proj(wwide_ref, 0, D_MLA)
    u_ref[...] = proj(wwide_ref, D_MLA, D_MLA + D_SSM)
    gs_ref[...] = proj(wwide_ref, D_MLA + D_SSM, D_MLA + 2 * D_SSM)


def _inproj(x, layer, g_pre, w_lat, w_kr, w_wide, g_q, g_kv, tm):
    L = x.shape[0]
    row = lambda n: pl.BlockSpec((tm, n), lambda i: (i, 0))
    out_shape = (
        jax.ShapeDtypeStruct((L, Q_LORA), BF16),
        jax.ShapeDtypeStruct((L, KV_LORA), BF16),
        jax.ShapeDtypeStruct((L, LANES), F32),
        jax.ShapeDtypeStruct((L, D_MLA), F32),
        jax.ShapeDtypeStruct((L, D_SSM), F32),
        jax.ShapeDtypeStruct((L, D_SSM), F32),
    )
    return pl.pallas_call(
        _inproj_kernel,
        grid=(L // tm,),
        in_specs=[row(D_MODEL)] + [_layer(p, layer) for p in (g_pre, w_lat, w_kr, w_wide, g_q, g_kv)],
        out_specs=(row(Q_LORA), row(KV_LORA), row(LANES), row(D_MLA), row(D_SSM), row(D_SSM)),
        out_shape=out_shape,
        compiler_params=_params("parallel"),
        name="inproj",
    )(x, g_pre, w_lat, w_kr, w_wide, g_q, g_kv)


def _rope_kernel(pos_ref, freq_ref, cos_ref, sin_ref):
    ang = freq_ref[...] * pos_ref[...].astype(F32)
    cos = jnp.cos(ang)
    sin = jnp.sin(ang)
    cos_ref[...] = jnp.concatenate([cos, cos], axis=0)
    sin_ref[...] = jnp.concatenate([sin, sin], axis=0)


def _rope_tables(positions, tm):
    L = positions.shape[-1]
    inv_freq = ROPE_THETA ** (-jnp.arange(0, ROPE, 2, dtype=F32) / ROPE)
    col = pl.BlockSpec((ROPE, tm), lambda i: (0, i))
    return pl.pallas_call(
        _rope_kernel,
        grid=(L // tm,),
        in_specs=[pl.BlockSpec((1, tm), lambda i: (0, i)), _resident((ROPE // 2, 1))],
        out_specs=(col, col),
        out_shape=(jax.ShapeDtypeStruct((ROPE, L), F32), jax.ShapeDtypeStruct((ROPE, L), F32)),
        compiler_params=_params("parallel"),
        name="rope_tables",
    )(positions.reshape(1, L), inv_freq.reshape(ROPE // 2, 1))


def _qkv_kernel(cq_ref, ckv_ref, kr_ref, cost_ref, sint_ref,
                wq_ref, wk_ref, wv_ref, qt_ref, k_ref, vt_ref, *, q_scale):
    cq = cq_ref[...]
    ckv = ckv_ref[...]
    half = ROPE // 2
    cos_t = cost_ref[...]
    sin_t = sint_ref[...]

    kt = kr_ref[...].T[:ROPE]
    kt_rot = jnp.concatenate([-kt[half:], kt[:half]], axis=0)
    kt = kt * cos_t + kt_rot * sin_t
    k_rope = jnp.concatenate([kt, jnp.zeros_like(kt)], axis=0).T[:, :ROPE].astype(BF16)
    k_nope = jnp.dot(ckv, wk_ref[...], preferred_element_type=F32)

    pad_row = lax.broadcasted_iota(jnp.int32, (V_ROWS - DV, vt_ref.shape[-1]), 0)
    ones_rows = jnp.where(pad_row == 0, 1.0, 0.0).astype(BF16)

    tn = (((0,), (1,)), ((), ()))
    qt_all = lax.dot_general(wq_ref[...], cq, tn, preferred_element_type=F32)
    vt_all = lax.dot_general(wv_ref[...], ckv, tn, preferred_element_type=F32)
    for h in range(HEADS):
        qt = qt_all[h * QK:(h + 1) * QK]
        qr = qt[NOPE:]
        qr_rot = jnp.concatenate([-qr[half:], qr[:half]], axis=0)
        qr = qr * cos_t + qr_rot * sin_t
        qt_ref[h, :NOPE, :] = (qt[:NOPE] * q_scale).astype(BF16)
        qt_ref[h, NOPE:, :] = (qr * q_scale).astype(BF16)
        k_ref[h, :, :NOPE] = k_nope[:, h * NOPE:(h + 1) * NOPE].astype(BF16)
        k_ref[h, :, NOPE:] = k_rope
        vt_ref[h, 0, :DV, :] = vt_all[h * DV:(h + 1) * DV].astype(BF16)
        vt_ref[h, 0, DV:, :] = ones_rows


def _qkv(cq, ckv, kr, rope, layer, wq, wk, wv, tk):
    L = cq.shape[0]
    q_scale = math.log2(math.e) / math.sqrt(QK)
    row = lambda n: pl.BlockSpec((tk, n), lambda i: (i, 0))
    return pl.pallas_call(
        functools.partial(_qkv_kernel, q_scale=q_scale),
        grid=(L // tk,),
        in_specs=[row(Q_LORA), row(KV_LORA), row(LANES),
                  pl.BlockSpec((ROPE, tk), lambda i: (0, i)), pl.BlockSpec((ROPE, tk), lambda i: (0, i)),
                  _layer(wq, layer), _layer(wk, layer), _layer(wv, layer)],
        out_specs=(pl.BlockSpec((HEADS, QK, tk), lambda i: (0, 0, i)),
                   pl.BlockSpec((HEADS, tk, QK), lambda i: (0, i, 0)),
                   pl.BlockSpec((HEADS, 1, V_ROWS, tk), lambda i: (0, i, 0, 0))),
        out_shape=(jax.ShapeDtypeStruct((HEADS, QK, L), BF16),
                   jax.ShapeDtypeStruct((HEADS, L, QK), BF16),
                   jax.ShapeDtypeStruct((HEADS, L // tk, V_ROWS, tk), BF16)),
        compiler_params=_params("parallel"),
        name="qkv",
    )(cq, ckv, kr, *rope, wq, wk, wv)


def _attn_kernel(qt_ref, k_ref, vt_ref, gate_ref, o_ref, acc_ref, m_ref, alpha_ref, s_ref, p_ref, *, tq, tk):
    i = pl.program_id(1)
    m_ref[...] = jnp.full(m_ref.shape, -1e30, F32)
    acc_ref[...] = jnp.zeros(acc_ref.shape, F32)
    assert tq == 2 * tk
    n_chains = tq // MXU_COLS

    def cols(c):
        return slice(MXU_COLS * c, MXU_COLS * (c + 1))

    def n_keys(c, diag):
        return tk if diag is None else max(0, min(tk, MXU_COLS * (c + 1) - diag * tk))

    def scores(j, slot, c, diag):
        rows = n_keys(c, diag)
        if rows == 0:
            return
        kc = k_ref[0, pl.ds(pl.multiple_of(j * tk, tk), rows), :]
        s = jnp.dot(kc, qt_ref[0, :, cols(c)], preferred_element_type=F32)
        if diag is not None and diag * tk + rows - 1 > MXU_COLS * c:
            key = diag * tk + lax.broadcasted_iota(jnp.int32, s.shape, 0)
            qry = MXU_COLS * c + lax.broadcasted_iota(jnp.int32, s.shape, 1)
            s = jnp.where(key <= qry, s, -1e30)
        s_ref[slot, c, :rows, :] = s.astype(BF16)

    def softmax(slot, c, diag):
        rows = n_keys(c, diag)
        if rows == 0:
            return
        s = s_ref[slot, c, :rows, :]
        m_blk = jnp.max(s.reshape(rows // BF16_ROWS, BF16_ROWS, MXU_COLS), axis=0)
        m_blk = jnp.max(m_blk.astype(F32), axis=0, keepdims=True)
        m_prev = m_ref[:, cols(c)]
        m_new = jnp.maximum(m_prev, m_blk)
        alpha_ref[slot, :, cols(c)] = jnp.exp2(m_prev - m_new)
        p_ref[slot, c, :rows, :] = jnp.exp2(s - m_new.astype(BF16))
        m_ref[:, cols(c)] = m_new

    def pv(j, slot, c, diag):
        rows = n_keys(c, diag)
        if rows == 0:
            return
        vc = vt_ref[0, j, :, :rows]
        upd = jnp.dot(vc, p_ref[slot, c, :rows, :], preferred_element_type=F32)
        acc_ref[:, cols(c)] = alpha_ref[slot, :, cols(c)] * acc_ref[:, cols(c)] + upd

    def step(score=None, soft=None, pvs=None):
        for c in range(n_chains):
            if soft is not None:
                softmax(soft[1], c, soft[2])
            if score is not None:
                scores(score[0], score[1], c, score[2])
            if pvs is not None:
                pv(pvs[0], pvs[1], c, pvs[2])

    @pl.when(i > 0)
    def _():
        step(score=(0, 0, None))
        step(score=(1, 1, None), soft=(0, 0, None))

        def pair(t):
            j = 2 * t
            step(score=(j + 2, 0, None), soft=(j + 1, 1, None), pvs=(j, 0, None))
            step(score=(j + 3, 1, None), soft=(j + 2, 0, None), pvs=(j + 1, 1, None))

        def body(t, carry):
            pair(t)
            return carry

        lax.fori_loop(0, i - 1, body, 0)

        j = 2 * i - 2
        step(score=(j + 2, 0, 0), soft=(j + 1, 1, None), pvs=(j, 0, None))
        step(score=(j + 3, 1, 1), soft=(j + 2, 0, 0), pvs=(j + 1, 1, None))

    @pl.when(i == 0)
    def _():
        step(score=(0, 0, 0))
        step(score=(1, 1, 1), soft=(0, 0, 0))

    step(soft=(2 * i + 1, 1, 1), pvs=(2 * i, 0, 0))
    step(pvs=(2 * i + 1, 1, 1))

    out = (acc_ref[:DV, :] * (1.0 / acc_ref[DV:DV + 1, :])).T
    g = gate_ref[...]
    o_ref[...] = (out * (g * _sigmoid(g))).astype(BF16)


def _attention(qt, k, vt, gate, tq, tk):
    L = k.shape[1]
    return pl.pallas_call(
        functools.partial(_attn_kernel, tq=tq, tk=tk),
        grid=(HEADS, L // tq),
        in_specs=[pl.BlockSpec((1, QK, tq), lambda h, i: (h, 0, i)),
                  pl.BlockSpec((1, L, QK), lambda h, i: (h, 0, 0)),
                  pl.BlockSpec((1, L // tk, V_ROWS, tk), lambda h, i: (h, 0, 0, 0)),
                  pl.BlockSpec((tq, DV), lambda h, i: (i, h))],
        out_specs=pl.BlockSpec((tq, DV), lambda h, i: (i, h)),
        out_shape=jax.ShapeDtypeStruct((L, D_MLA), BF16),
        scratch_shapes=[pltpu.VMEM((V_ROWS, tq), F32), pltpu.VMEM((1, tq), F32),
                        pltpu.VMEM((2, 1, tq), F32),
                        pltpu.VMEM((2, tq // MXU_COLS, tk, MXU_COLS), BF16),
                        pltpu.VMEM((2, tq // MXU_COLS, tk, MXU_COLS), BF16)],
        compiler_params=_params("parallel", "arbitrary"),
        name="attention",
    )(qt, k, vt, gate)


def _gelu_tanh(x):
    c = math.sqrt(2.0 / math.pi)
    return 0.5 * x * (1.0 + jnp.tanh(c * (x + 0.044715 * (x * x * x))))


S5_CHUNK = 256
PITCH = 68
N_JOBS = 32
JOBS_PER_SLAB = N_JOBS // N_SLABS
TILES_PER_JOB = 2 * STATE_TILES // N_JOBS


def _job_tiles(slab, kk):
    is_im = kk // (JOBS_PER_SLAB // 2)
    q = kk % (JOBS_PER_SLAB // 2)
    tile0 = is_im * STATE_TILES + SLAB_TILES * slab + TILES_PER_JOB * q
    return 2 * slab + q // 2, tile0


def _s5_kernel(un_ref, up_ref, g_ref, bre_ref, bim_ref, are_ref, aim_ref, cre_ref, cim_ref, d_ref, wglu_ref,
               o_ref, hs0_ref, hs1_ref, st_ref, acc_ref, ub_ref, wbj_ref, wcj_ref, *, tt):
    c = pl.program_id(0)
    hs_refs = (hs0_ref, hs1_ref)

    def col_tile(tile):
        return pl.ds(tile, tt, stride=PITCH)

    def load_u_slabs(u_ref):
        u = u_ref[...]
        for s in range(D_SSM // LANES):
            ub_ref[s] = u[:, LANES * s:LANES * (s + 1)].astype(BF16)

    def b_job(slab, kk, slot):
        s, tile0 = _job_tiles(slab, kk)
        r = jnp.dot(ub_ref[s], wbj_ref[slab * JOBS_PER_SLAB + kk], preferred_element_type=F32)
        for k in range(TILES_PER_JOB):
            hs_refs[slot][col_tile(tile0 + k), :] = r[:, LANES * k:LANES * (k + 1)]

    def c_job(slab, kk, slot):
        _, tile0 = _job_tiles(slab, kk)
        lhs = jnp.concatenate([hs_refs[slot][col_tile(tile0 + k), :] for k in range(TILES_PER_JOB)],
                              axis=-1).astype(BF16)
        return jnp.dot(lhs, wcj_ref[slab * JOBS_PER_SLAB + kk], preferred_element_type=F32)

    def build_job_weights():
        wbj_ref[...] = jnp.zeros(wbj_ref.shape, BF16)
        wcj_ref[...] = jnp.zeros(wcj_ref.shape, BF16)
        groups_per_job = TILES_PER_JOB * LANES // NSTATE
        groups_per_slab = JOBS_PER_SLAB // 2 * groups_per_job
        for g in range(GROUPS):
            job_re = (g // groups_per_slab) * JOBS_PER_SLAB + (g % groups_per_slab) // groups_per_job
            state = slice((g % groups_per_job) * NSTATE, (g % groups_per_job + 1) * NSTATE)
            u_ch = slice((g % (LANES // GROUP_CH)) * GROUP_CH, (g % (LANES // GROUP_CH) + 1) * GROUP_CH)
            y_ch = slice((g % groups_per_slab) * GROUP_CH, (g % groups_per_slab + 1) * GROUP_CH)
            for part, (b_ref, c_ref) in enumerate(((bre_ref, cre_ref), (bim_ref, cim_ref))):
                job = job_re + part * (JOBS_PER_SLAB // 2)
                wbj_ref[job, u_ch, state] = b_ref[g].astype(BF16)
                wcj_ref[job, state, y_ch] = c_ref[g].astype(BF16)

    @pl.when(c == 0)
    def _():
        build_job_weights()
        st_ref[...] = jnp.zeros(st_ref.shape, F32)
        hs1_ref[...] = jnp.zeros(hs1_ref.shape, F32)
        load_u_slabs(up_ref)
        for ti in range(N_JOBS):
            b_job(ti // JOBS_PER_SLAB, ti % JOBS_PER_SLAB, 0)

    load_u_slabs(un_ref)
    a_re = [are_ref[i] for i in range(N_SLABS)]
    a_im = [aim_ref[i] for i in range(N_SLABS)]
    steps_per_job = tt // N_JOBS

    def run(slot):
        hs_ref = hs_refs[slot]

        def scan_steps(t0, st):
            st = list(st)
            for dt in range(steps_per_job):
                base = (t0 + dt) * PITCH
                for i in range(N_SLABS):
                    re_rows = pl.ds(base + SLAB_TILES * i, SLAB_TILES)
                    im_rows = pl.ds(base + STATE_TILES + SLAB_TILES * i, SLAB_TILES)
                    h_re, h_im = st[i], st[N_SLABS + i]
                    n_re = a_re[i] * h_re - a_im[i] * h_im + hs_ref[re_rows, :]
                    n_im = a_re[i] * h_im + a_im[i] * h_re + hs_ref[im_rows, :]
                    hs_ref[re_rows, :] = n_re
                    hs_ref[im_rows, :] = n_im
                    st[i], st[N_SLABS + i] = n_re, n_im
            return st

        st = [st_ref[i] for i in range(2 * N_SLABS)]
        for slab in range(N_SLABS):
            y = None
            for kk in range(JOBS_PER_SLAB):
                st = scan_steps((slab * JOBS_PER_SLAB + kk) * steps_per_job, st)
                part = c_job(slab, kk, 1 - slot)
                y = part if y is None else y + part
                b_job(slab, kk, 1 - slot)
            acc_ref[slab] = y
        for i in range(2 * N_SLABS):
            st_ref[i] = st[i]

    for slot in range(2):
        pl.when(c % 2 == slot)(functools.partial(run, slot))

    y = jnp.concatenate([acc_ref[i] for i in range(N_SLABS)], axis=-1) + d_ref[...] * up_ref[...]
    y = _gelu_tanh(y).astype(BF16)
    tile = D_SSM // N_SLABS
    for n in range(N_SLABS):
        val_cols = slice(tile * n, tile * (n + 1))
        gate_cols = slice(D_SSM + tile * n, D_SSM + tile * (n + 1))
        val = jnp.dot(y, wglu_ref[:, val_cols], preferred_element_type=F32)
        gate = jnp.dot(y, wglu_ref[:, gate_cols], preferred_element_type=F32)
        g = g_ref[:, val_cols]
        o_ref[:, val_cols] = (val * _sigmoid(gate) * (g * _sigmoid(g))).astype(BF16)


def _s5(u, gate, layer, b_re, b_im, a_re, a_im, c_re, c_im, d_skip, w_glu, tt):
    L = u.shape[0]
    n = L // tt
    job_cols = TILES_PER_JOB * LANES
    nxt = pl.BlockSpec((tt, D_SSM), lambda c: (jnp.minimum(c + 1, n - 1), 0))
    prv = pl.BlockSpec((tt, D_SSM), lambda c: (jnp.maximum(c - 1, 0), 0))
    return pl.pallas_call(
        functools.partial(_s5_kernel, tt=tt),
        grid=(n + 1,),
        in_specs=[nxt, prv, prv] + [_layer(p, layer)
                                    for p in (b_re, b_im, a_re, a_im, c_re, c_im, d_skip, w_glu)],
        out_specs=prv,
        out_shape=jax.ShapeDtypeStruct((L, D_SSM), BF16),
        scratch_shapes=[pltpu.VMEM((tt * PITCH, LANES), F32), pltpu.VMEM((tt * PITCH, LANES), F32),
                        pltpu.VMEM((2 * N_SLABS, SUBLANES, LANES), F32),
                        pltpu.VMEM((N_SLABS, tt, D_SSM // N_SLABS), F32),
                        pltpu.VMEM((D_SSM // LANES, tt, LANES), BF16),
                        pltpu.VMEM((N_JOBS, LANES, job_cols), BF16),
                        pltpu.VMEM((N_JOBS, job_cols, D_SSM // N_SLABS), BF16)],
        compiler_params=_params("arbitrary"),
        name="s5",
    )(u, u, gate, b_re, b_im, a_re, a_im, c_re, c_im, d_skip, w_glu)


def _outproj_kernel(mla_ref, ssm_ref, w_ref, gpost_ref, x_ref, o_ref):
    out = jnp.dot(mla_ref[...], w_ref[:D_MLA, :], preferred_element_type=F32)
    out += jnp.dot(ssm_ref[...], w_ref[D_MLA:, :], preferred_element_type=F32)
    inv = lax.rsqrt(jnp.mean(out * out, axis=-1, keepdims=True) + NORM_EPS)
    o_ref[...] = x_ref[...] + out * inv * gpost_ref[...]


def _outproj(mla, ssm, layer, w_out, g_post, x, tm):
    L = x.shape[0]
    row = lambda n: pl.BlockSpec((tm, n), lambda i: (i, 0))
    return pl.pallas_call(
        _outproj_kernel,
        grid=(L // tm,),
        in_specs=[row(D_MLA), row(D_SSM), _layer(w_out, layer), _layer(g_post, layer), row(D_MODEL)],
        out_specs=row(D_MODEL),
        out_shape=jax.ShapeDtypeStruct((L, D_MODEL), F32),
        compiler_params=_params("parallel"),
        name="outproj",
    )(mla, ssm, w_out, g_post, x)


def _split_in_weights(w_in):
    s2 = Q_LORA + KV_LORA
    s3 = s2 + ROPE
    w = lax.optimization_barrier(w_in.astype(BF16))
    w_kr = jnp.pad(w[..., s2:s3], ((0, 0), (0, 0), (0, LANES - ROPE)))
    return w[..., :s2], w_kr, w[..., s3:]


def kernel(x, positions, norm_pre, norm_post, w_in, q_norm, w_uq, kv_norm, w_ukv, ssm_a_re, ssm_a_im,
           ssm_b_re, ssm_b_im, ssm_c_re, ssm_c_im, ssm_d, ssm_log_step, w_glu, w_out):
    batch, L, _ = x.shape
    assert batch == 1
    depth = w_in.shape[0]
    tm = min(512, L)
    tq = min(2048, L)
    tk = tq // 2
    tt = S5_CHUNK

    rope = _rope_tables(positions, tk)

    w_lat, w_kr, w_wide = _split_in_weights(w_in)
    wq = w_uq.astype(BF16)
    w_ukv4 = w_ukv.astype(BF16).reshape(depth, KV_LORA, HEADS, NOPE + DV)
    wk = w_ukv4[..., :NOPE].reshape(depth, KV_LORA, HEADS * NOPE)
    wv = w_ukv4[..., NOPE:].reshape(depth, KV_LORA, HEADS * DV)
    w_glu_b = w_glu.astype(BF16)
    w_out_b = w_out.astype(BF16)

    ab_re, ab_im, bb_re, bb_im = _s5_prep(ssm_a_re, ssm_a_im, ssm_log_step, ssm_b_re, ssm_b_im)
    b_t = [b.transpose(0, 1, 3, 2) for b in (bb_re, bb_im)]
    slab_shape = (depth, N_SLABS, SUBLANES, LANES)
    a_re_s = ab_re.reshape(slab_shape)
    a_im_s = ab_im.reshape(slab_shape)
    c_t = [c.transpose(0, 1, 3, 2) for c in (ssm_c_re, -ssm_c_im)]

    def vec(p):
        return p.reshape(depth, 1, -1)

    h = x[0]
    for i in range(depth):
        cq, ckv, kr, gate_mla, u, gate_ssm = _inproj(
            h, i, vec(norm_pre), w_lat, w_kr, w_wide, vec(q_norm), vec(kv_norm), tm)
        qt, k, vt = _qkv(cq, ckv, kr, rope, i, wq, wk, wv, tk)
        mla = _attention(qt, k, vt, gate_mla, tq, tk)
        ssm = _s5(u, gate_ssm, i, *b_t, a_re_s, a_im_s, *c_t, vec(ssm_d), w_glu_b, tt)
        h = _outproj(mla, ssm, i, w_out_b, vec(norm_post), h, tm)
    return h[None]
```

```python
import functools
import math

import jax
import jax.numpy as jnp
from jax import lax
from jax.experimental import pallas as pl
from jax.experimental.pallas import tpu as pltpu

F32 = jnp.float32
BF16 = jnp.bfloat16

D_MODEL = 2048
HEADS = 8
NOPE = 128
ROPE = 64
QK = NOPE + ROPE
DV = 128
V_ROWS = DV + 16
Q_LORA = 512
KV_LORA = 256
D_MLA = HEADS * DV
D_SSM = 1024
GROUP_CH = 16
GROUPS = D_SSM // GROUP_CH
NSTATE = 64
ROPE_THETA = 10000.0
NORM_EPS = 1e-6
LANES = 128
SUBLANES = 8
MXU_COLS = 256
BF16_ROWS = 16
STATE_TILES = GROUPS * NSTATE // LANES
SLAB_TILES = SUBLANES
N_SLABS = STATE_TILES // SLAB_TILES
VMEM_LIMIT = 56 * 1024 * 1024


def _resident(shape):
    zeros = (0,) * len(shape)
    return pl.BlockSpec(shape, lambda *_: zeros, pipeline_mode=pl.Buffered(1))


def _layer(stack, layer):
    shape = stack.shape[1:]
    index = (layer,) + (0,) * len(shape)
    return pl.BlockSpec((None,) + shape, lambda *_: index, pipeline_mode=pl.Buffered(1))


def _sigmoid(x):
    return 1.0 / (1.0 + jnp.exp(-x))


def _params(*sem):
    return pltpu.CompilerParams(dimension_semantics=sem, vmem_limit_bytes=VMEM_LIMIT)


def _s5_prep_kernel(are_ref, aim_ref, lstep_ref, bre_ref, bim_ref,
                    abre_ref, abim_ref, bbre_ref, bbim_ref):
    a_re = are_ref[...]
    a_im = aim_ref[...]
    step = jnp.exp(lstep_ref[...])
    mag = jnp.exp(step * a_re)
    ab_re = mag * jnp.cos(step * a_im)
    ab_im = mag * jnp.sin(step * a_im)
    n_re = ab_re - 1.0
    den = a_re * a_re + a_im * a_im
    z_re = (n_re * a_re + ab_im * a_im) / den
    z_im = (ab_im * a_re - n_re * a_im) / den
    b_re = bre_ref[...]
    b_im = bim_ref[...]
    abre_ref[...] = ab_re
    abim_ref[...] = ab_im
    bbre_ref[...] = z_re * b_re - z_im * b_im
    bbim_ref[...] = z_re * b_im + z_im * b_re


def _s5_prep(a_re, a_im, log_step, b_re, b_im):
    depth = a_re.shape[0]
    rows = depth * GROUPS
    cols = NSTATE * GROUP_CH

    def expand(a):
        return jnp.broadcast_to(a[..., None], (depth, GROUPS, NSTATE, GROUP_CH)).reshape(rows, cols)

    step = jnp.broadcast_to(log_step[..., None], (depth, GROUPS, cols)).reshape(rows, cols)
    sds = jax.ShapeDtypeStruct((rows, cols), F32)
    ab_re, ab_im, bb_re, bb_im = pl.pallas_call(
        _s5_prep_kernel, out_shape=(sds, sds, sds, sds), name="s5_prep",
    )(expand(a_re), expand(a_im), step, b_re.reshape(rows, cols), b_im.reshape(rows, cols))
    shape4 = (depth, GROUPS, NSTATE, GROUP_CH)
    ab_re = ab_re.reshape(shape4)[..., 0]
    ab_im = ab_im.reshape(shape4)[..., 0]
    return ab_re, ab_im, bb_re.reshape(shape4), bb_im.reshape(shape4)


def _inproj_kernel(x_ref, gpre_ref, w_ref, gq_ref, gkv_ref,
                   cq_ref, ckv_ref, kr_ref, gm_ref, u_ref, gs_ref, wkr_ref, wwide_ref):
    n_lat = Q_LORA + KV_LORA

    @pl.when(pl.program_id(0) == 0)
    def _():
        rows = 256
        for r in range(0, D_MODEL, rows):
            blk = w_ref[r:r + rows, :]
            kr = blk[:, n_lat:n_lat + ROPE]
            wkr_ref[r:r + rows, :] = jnp.concatenate([kr, jnp.zeros((rows, LANES - ROPE), BF16)], axis=-1)
            wwide_ref[r:r + rows, :] = blk[:, n_lat + ROPE:]

    wlat_ref = w_ref
    x = x_ref[...]
    inv = lax.rsqrt(jnp.mean(x * x, axis=-1, keepdims=True) + NORM_EPS)
    h = (x * inv * gpre_ref[...]).astype(BF16)

    def proj(w_ref, lo, hi):
        return jnp.dot(h, w_ref[:, lo:hi], preferred_element_type=F32)

    def latent_norm(c, g_ref):
        inv_c = lax.rsqrt(jnp.mean(c * c, axis=-1, keepdims=True) + NORM_EPS)
        return (c * inv_c * g_ref[...]).astype(BF16)

    cq_ref[...] = latent_norm(proj(wlat_ref, 0, Q_LORA), gq_ref)
    ckv_ref[...] = latent_norm(proj(wlat_ref, Q_LORA, Q_LORA + KV_LORA), gkv_ref)
    kr_ref[...] = proj(wkr_ref, 0, LANES)
    gm_ref[...] = proj(wwide_ref, 0, D_MLA)
    u_ref[...] = proj(wwide_ref, D_MLA, D_MLA + D_SSM)
    gs_ref[...] = proj(wwide_ref, D_MLA + D_SSM, D_MLA + 2 * D_SSM)


def _inproj(x, layer, g_pre, w_in, g_q, g_kv, tm):
    L = x.shape[0]
    row = lambda n: pl.BlockSpec((tm, n), lambda i: (i, 0))
    out_shape = (
        jax.ShapeDtypeStruct((L, Q_LORA), BF16),
        jax.ShapeDtypeStruct((L, KV_LORA), BF16),
        jax.ShapeDtypeStruct((L, LANES), F32),
        jax.ShapeDtypeStruct((L, D_MLA), F32),
        jax.ShapeDtypeStruct((L, D_SSM), F32),
        jax.ShapeDtypeStruct((L, D_SSM), F32),
    )
    return pl.pallas_call(
        _inproj_kernel,
        grid=(L // tm,),
        in_specs=[row(D_MODEL)] + [_layer(p, layer) for p in (g_pre, w_in, g_q, g_kv)],
        out_specs=(row(Q_LORA), row(KV_LORA), row(LANES), row(D_MLA), row(D_SSM), row(D_SSM)),
        out_shape=out_shape,
        scratch_shapes=[pltpu.VMEM((D_MODEL, LANES), BF16), pltpu.VMEM((D_MODEL, D_MLA + 2 * D_SSM), BF16)],
        compiler_params=_params("arbitrary"),
        name="inproj",
    )(x, g_pre, w_in, g_q, g_kv)


def _rope_kernel(pos_ref, freq_ref, cos_ref, sin_ref):
    ang = freq_ref[...] * pos_ref[...].astype(F32)
    cos = jnp.cos(ang)
    sin = jnp.sin(ang)
    cos_ref[...] = jnp.concatenate([cos, cos], axis=0)
    sin_ref[...] = jnp.concatenate([sin, sin], axis=0)


def _rope_tables(positions, tm):
    L = positions.shape[-1]
    inv_freq = ROPE_THETA ** (-jnp.arange(0, ROPE, 2, dtype=F32) / ROPE)
    col = pl.BlockSpec((ROPE, tm), lambda i: (0, i))
    return pl.pallas_call(
        _rope_kernel,
        grid=(L // tm,),
        in_specs=[pl.BlockSpec((1, tm), lambda i: (0, i)), _resident((ROPE // 2, 1))],
        out_specs=(col, col),
        out_shape=(jax.ShapeDtypeStruct((ROPE, L), F32), jax.ShapeDtypeStruct((ROPE, L), F32)),
        compiler_params=_params("parallel"),
        name="rope_tables",
    )(positions.reshape(1, L), inv_freq.reshape(ROPE // 2, 1))


def _qkv_kernel(cq_ref, ckv_ref, kr_ref, cost_ref, sint_ref,
                wq_ref, wk_ref, wv_ref, qt_ref, k_ref, vt_ref, *, q_scale):
    cq = cq_ref[...]
    ckv = ckv_ref[...]
    half = ROPE // 2
    cos_t = cost_ref[...]
    sin_t = sint_ref[...]

    kt = kr_ref[...].T[:ROPE]
    kt_rot = jnp.concatenate([-kt[half:], kt[:half]], axis=0)
    kt = kt * cos_t + kt_rot * sin_t
    k_rope = jnp.concatenate([kt, jnp.zeros_like(kt)], axis=0).T[:, :ROPE].astype(BF16)
    k_nope = jnp.dot(ckv, wk_ref[...], preferred_element_type=F32)

    pad_row = lax.broadcasted_iota(jnp.int32, (V_ROWS - DV, vt_ref.shape[-1]), 0)
    ones_rows = jnp.where(pad_row == 0, 1.0, 0.0).astype(BF16)

    tn = (((0,), (1,)), ((), ()))
    qt_all = lax.dot_general(wq_ref[...], cq, tn, preferred_element_type=F32)
    vt_all = lax.dot_general(wv_ref[...], ckv, tn, preferred_element_type=F32)
    for h in range(HEADS):
        qt = qt_all[h * QK:(h + 1) * QK]
        qr = qt[NOPE:]
        qr_rot = jnp.concatenate([-qr[half:], qr[:half]], axis=0)
        qr = qr * cos_t + qr_rot * sin_t
        qt_ref[h, :NOPE, :] = (qt[:NOPE] * q_scale).astype(BF16)
        qt_ref[h, NOPE:, :] = (qr * q_scale).astype(BF16)
        k_ref[h, :, :NOPE] = k_nope[:, h * NOPE:(h + 1) * NOPE].astype(BF16)
        k_ref[h, :, NOPE:] = k_rope
        vt_ref[h, 0, :DV, :] = vt_all[h * DV:(h + 1) * DV].astype(BF16)
        vt_ref[h, 0, DV:, :] = ones_rows


def _qkv(cq, ckv, kr, rope, layer, wq, wk, wv, tk):
    L = cq.shape[0]
    q_scale = math.log2(math.e) / math.sqrt(QK)
    row = lambda n: pl.BlockSpec((tk, n), lambda i: (i, 0))
    return pl.pallas_call(
        functools.partial(_qkv_kernel, q_scale=q_scale),
        grid=(L // tk,),
        in_specs=[row(Q_LORA), row(KV_LORA), row(LANES),
                  pl.BlockSpec((ROPE, tk), lambda i: (0, i)), pl.BlockSpec((ROPE, tk), lambda i: (0, i)),
                  _layer(wq, layer), _layer(wk, layer), _layer(wv, layer)],
        out_specs=(pl.BlockSpec((HEADS, QK, tk), lambda i: (0, 0, i)),
                   pl.BlockSpec((HEADS, tk, QK), lambda i: (0, i, 0)),
                   pl.BlockSpec((HEADS, 1, V_ROWS, tk), lambda i: (0, i, 0, 0))),
        out_shape=(jax.ShapeDtypeStruct((HEADS, QK, L), BF16),
                   jax.ShapeDtypeStruct((HEADS, L, QK), BF16),
                   jax.ShapeDtypeStruct((HEADS, L // tk, V_ROWS, tk), BF16)),
        compiler_params=_params("parallel"),
        name="qkv",
    )(cq, ckv, kr, *rope, wq, wk, wv)


def _attn_kernel(qt_ref, k_ref, vt_ref, gate_ref, o_ref, acc_ref, m_ref, alpha_ref, s_ref, p_ref, *, tq, tk):
    i = pl.program_id(1)
    m_ref[...] = jnp.full(m_ref.shape, -1e30, F32)
    acc_ref[...] = jnp.zeros(acc_ref.shape, F32)
    assert tq == 2 * tk
    n_chains = tq // MXU_COLS

    def cols(c):
        return slice(MXU_COLS * c, MXU_COLS * (c + 1))

    def n_keys(c, diag):
        return tk if diag is None else max(0, min(tk, MXU_COLS * (c + 1) - diag * tk))

    def scores(j, slot, c, diag):
        rows = n_keys(c, diag)
        if rows == 0:
            return
        kc = k_ref[0, pl.ds(pl.multiple_of(j * tk, tk), rows), :]
        s = jnp.dot(kc, qt_ref[0, :, cols(c)], preferred_element_type=F32)
        if diag is not None and diag * tk + rows - 1 > MXU_COLS * c:
            key = diag * tk + lax.broadcasted_iota(jnp.int32, s.shape, 0)
            qry = MXU_COLS * c + lax.broadcasted_iota(jnp.int32, s.shape, 1)
            s = jnp.where(key <= qry, s, -1e30)
        s_ref[slot, c, :rows, :] = s.astype(BF16)

    def softmax(slot, c, diag):
        rows = n_keys(c, diag)
        if rows == 0:
            return
        s = s_ref[slot, c, :rows, :]
        m_blk = jnp.max(s.reshape(rows // BF16_ROWS, BF16_ROWS, MXU_COLS), axis=0)
        m_blk = jnp.max(m_blk.astype(F32), axis=0, keepdims=True)
        m_prev = m_ref[:, cols(c)]
        m_new = jnp.maximum(m_prev, m_blk)
        alpha_ref[slot, :, cols(c)] = jnp.exp2(m_prev - m_new)
        p_ref[slot, c, :rows, :] = jnp.exp2(s - m_new.astype(BF16))
        m_ref[:, cols(c)] = m_new

    def pv(j, slot, c, diag):
        rows = n_keys(c, diag)
        if rows == 0:
            return
        vc = vt_ref[0, j, :, :rows]
        upd = jnp.dot(vc, p_ref[slot, c, :rows, :], preferred_element_type=F32)
        acc_ref[:, cols(c)] = alpha_ref[slot, :, cols(c)] * acc_ref[:, cols(c)] + upd

    def step(score=None, soft=None, pvs=None):
        for c in range(n_chains):
            if soft is not None:
                softmax(soft[1], c, soft[2])
            if score is not None:
                scores(score[0], score[1], c, score[2])
            if pvs is not None:
                pv(pvs[0], pvs[1], c, pvs[2])

    @pl.when(i > 0)
    def _():
        step(score=(0, 0, None))
        step(score=(1, 1, None), soft=(0, 0, None))

        def pair(t):
            j = 2 * t
            step(score=(j + 2, 0, None), soft=(j + 1, 1, None), pvs=(j, 0, None))
            step(score=(j + 3, 1, None), soft=(j + 2, 0, None), pvs=(j + 1, 1, None))

        def body(t, carry):
            pair(t)
            return carry

        lax.fori_loop(0, i - 1, body, 0)

        j = 2 * i - 2
        step(score=(j + 2, 0, 0), soft=(j + 1, 1, None), pvs=(j, 0, None))
        step(score=(j + 3, 1, 1), soft=(j + 2, 0, 0), pvs=(j + 1, 1, None))

    @pl.when(i == 0)
    def _():
        step(score=(0, 0, 0))
        step(score=(1, 1, 1), soft=(0, 0, 0))

    step(soft=(2 * i + 1, 1, 1), pvs=(2 * i, 0, 0))
    step(pvs=(2 * i + 1, 1, 1))

    out = (acc_ref[:DV, :] * (1.0 / acc_ref[DV:DV + 1, :])).T
    g = gate_ref[...]
    o_ref[...] = (out * (g * _sigmoid(g))).astype(BF16)


def _attention(qt, k, vt, gate, tq, tk):
    L = k.shape[1]
    return pl.pallas_call(
        functools.partial(_attn_kernel, tq=tq, tk=tk),
        grid=(HEADS, L // tq),
        in_specs=[pl.BlockSpec((1, QK, tq), lambda h, i: (h, 0, i)),
                  pl.BlockSpec((1, L, QK), lambda h, i: (h, 0, 0)),
                  pl.BlockSpec((1, L // tk, V_ROWS, tk), lambda h, i: (h, 0, 0, 0)),
                  pl.BlockSpec((tq, DV), lambda h, i: (i, h))],
        out_specs=pl.BlockSpec((tq, DV), lambda h, i: (i, h)),
        out_shape=jax.ShapeDtypeStruct((L, D_MLA), BF16),
        scratch_shapes=[pltpu.VMEM((V_ROWS, tq), F32), pltpu.VMEM((1, tq), F32),
                        pltpu.VMEM((2, 1, tq), F32),
                        pltpu.VMEM((2, tq // MXU_COLS, tk, MXU_COLS), BF16),
                        pltpu.VMEM((2, tq // MXU_COLS, tk, MXU_COLS), BF16)],
        compiler_params=_params("parallel", "arbitrary"),
        name="attention",
    )(qt, k, vt, gate)


def _gelu_tanh(x):
    c = math.sqrt(2.0 / math.pi)
    return 0.5 * x * (1.0 + jnp.tanh(c * (x + 0.044715 * (x * x * x))))


S5_CHUNK = 256
PITCH = 68
N_JOBS = 32
JOBS_PER_SLAB = N_JOBS // N_SLABS
TILES_PER_JOB = 2 * STATE_TILES // N_JOBS


def _job_tiles(slab, kk):
    is_im = kk // (JOBS_PER_SLAB // 2)
    q = kk % (JOBS_PER_SLAB // 2)
    tile0 = is_im * STATE_TILES + SLAB_TILES * slab + TILES_PER_JOB * q
    return 2 * slab + q // 2, tile0


def _s5_kernel(un_ref, up_ref, g_ref, bre_ref, bim_ref, are_ref, aim_ref, cre_ref, cim_ref, d_ref, wglu_ref,
               o_ref, hs0_ref, hs1_ref, st_ref, acc_ref, ub_ref, wbj_ref, wcj_ref, *, tt):
    c = pl.program_id(0)
    hs_refs = (hs0_ref, hs1_ref)

    def col_tile(tile):
        return pl.ds(tile, tt, stride=PITCH)

    def load_u_slabs(u_ref):
        u = u_ref[...]
        for s in range(D_SSM // LANES):
            ub_ref[s] = u[:, LANES * s:LANES * (s + 1)].astype(BF16)

    def b_job(slab, kk, slot):
        s, tile0 = _job_tiles(slab, kk)
        r = jnp.dot(ub_ref[s], wbj_ref[slab * JOBS_PER_SLAB + kk], preferred_element_type=F32)
        for k in range(TILES_PER_JOB):
            hs_refs[slot][col_tile(tile0 + k), :] = r[:, LANES * k:LANES * (k + 1)]

    def c_job(slab, kk, slot):
        _, tile0 = _job_tiles(slab, kk)
        lhs = jnp.concatenate([hs_refs[slot][col_tile(tile0 + k), :] for k in range(TILES_PER_JOB)],
                              axis=-1).astype(BF16)
        return jnp.dot(lhs, wcj_ref[slab * JOBS_PER_SLAB + kk], preferred_element_type=F32)

    def build_job_weights():
        wbj_ref[...] = jnp.zeros(wbj_ref.shape, BF16)
        wcj_ref[...] = jnp.zeros(wcj_ref.shape, BF16)
        groups_per_job = TILES_PER_JOB * LANES // NSTATE
        groups_per_slab = JOBS_PER_SLAB // 2 * groups_per_job
        for g in range(GROUPS):
            job_re = (g // groups_per_slab) * JOBS_PER_SLAB + (g % groups_per_slab) // groups_per_job
            state = slice((g % groups_per_job) * NSTATE, (g % groups_per_job + 1) * NSTATE)
            u_ch = slice((g % (LANES // GROUP_CH)) * GROUP_CH, (g % (LANES // GROUP_CH) + 1) * GROUP_CH)
            y_ch = slice((g % groups_per_slab) * GROUP_CH, (g % groups_per_slab + 1) * GROUP_CH)
            for part, (b_ref, c_ref) in enumerate(((bre_ref, cre_ref), (bim_ref, cim_ref))):
                job = job_re + part * (JOBS_PER_SLAB // 2)
                wbj_ref[job, u_ch, state] = b_ref[g].astype(BF16)
                wcj_ref[job, state, y_ch] = c_ref[g].astype(BF16)

    @pl.when(c == 0)
    def _():
        build_job_weights()
        st_ref[...] = jnp.zeros(st_ref.shape, F32)
        hs1_ref[...] = jnp.zeros(hs1_ref.shape, F32)
        load_u_slabs(up_ref)
        for ti in range(N_JOBS):
            b_job(ti // JOBS_PER_SLAB, ti % JOBS_PER_SLAB, 0)

    load_u_slabs(un_ref)
    a_re = [are_ref[i] for i in range(N_SLABS)]
    a_im = [aim_ref[i] for i in range(N_SLABS)]
    steps_per_job = tt // N_JOBS

    def run(slot):
        hs_ref = hs_refs[slot]

        def scan_steps(t0, st):
            st = list(st)
            for dt in range(steps_per_job):
                base = (t0 + dt) * PITCH
                for i in range(N_SLABS):
                    re_rows = pl.ds(base + SLAB_TILES * i, SLAB_TILES)
                    im_rows = pl.ds(base + STATE_TILES + SLAB_TILES * i, SLAB_TILES)
                    h_re, h_im = st[i], st[N_SLABS + i]
                    n_re = a_re[i] * h_re - a_im[i] * h_im + hs_ref[re_rows, :]
                    n_im = a_re[i] * h_im + a_im[i] * h_re + hs_ref[im_rows, :]
                    hs_ref[re_rows, :] = n_re
                    hs_ref[im_rows, :] = n_im
                    st[i], st[N_SLABS + i] = n_re, n_im
            return st

        st = [st_ref[i] for i in range(2 * N_SLABS)]
        for slab in range(N_SLABS):
            y = None
            for kk in range(JOBS_PER_SLAB):
                st = scan_steps((slab * JOBS_PER_SLAB + kk) * steps_per_job, st)
                part = c_job(slab, kk, 1 - slot)
                y = part if y is None else y + part
                b_job(slab, kk, 1 - slot)
            acc_ref[slab] = y
        for i in range(2 * N_SLABS):
            st_ref[i] = st[i]

    for slot in range(2):
        pl.when(c % 2 == slot)(functools.partial(run, slot))

    y = jnp.concatenate([acc_ref[i] for i in range(N_SLABS)], axis=-1) + d_ref[...] * up_ref[...]
    y = _gelu_tanh(y).astype(BF16)
    tile = D_SSM // N_SLABS
    for n in range(N_SLABS):
        val_cols = slice(tile * n, tile * (n + 1))
        gate_cols = slice(D_SSM + tile * n, D_SSM + tile * (n + 1))
        val = jnp.dot(y, wglu_ref[:, val_cols], preferred_element_type=F32)
        gate = jnp.dot(y, wglu_ref[:, gate_cols], preferred_element_type=F32)
        g = g_ref[:, val_cols]
        o_ref[:, val_cols] = (val * _sigmoid(gate) * (g * _sigmoid(g))).astype(BF16)


def _s5(u, gate, layer, b_re, b_im, a_re, a_im, c_re, c_im, d_skip, w_glu, tt):
    L = u.shape[0]
    n = L // tt
    job_cols = TILES_PER_JOB * LANES
    nxt = pl.BlockSpec((tt, D_SSM), lambda c: (jnp.minimum(c + 1, n - 1), 0))
    prv = pl.BlockSpec((tt, D_SSM), lambda c: (jnp.maximum(c - 1, 0), 0))
    return pl.pallas_call(
        functools.partial(_s5_kernel, tt=tt),
        grid=(n + 1,),
        in_specs=[nxt, prv, prv] + [_layer(p, layer)
                                    for p in (b_re, b_im, a_re, a_im, c_re, c_im, d_skip, w_glu)],
        out_specs=prv,
        out_shape=jax.ShapeDtypeStruct((L, D_SSM), BF16),
        scratch_shapes=[pltpu.VMEM((tt * PITCH, LANES), F32), pltpu.VMEM((tt * PITCH, LANES), F32),
                        pltpu.VMEM((2 * N_SLABS, SUBLANES, LANES), F32),
                        pltpu.VMEM((N_SLABS, tt, D_SSM // N_SLABS), F32),
                        pltpu.VMEM((D_SSM // LANES, tt, LANES), BF16),
                        pltpu.VMEM((N_JOBS, LANES, job_cols), BF16),
                        pltpu.VMEM((N_JOBS, job_cols, D_SSM // N_SLABS), BF16)],
        compiler_params=_params("arbitrary"),
        name="s5",
    )(u, u, gate, b_re, b_im, a_re, a_im, c_re, c_im, d_skip, w_glu)


def _outproj_kernel(mla_ref, ssm_ref, w_ref, gpost_ref, x_ref, o_ref):
    out = jnp.dot(mla_ref[...], w_ref[:D_MLA, :], preferred_element_type=F32)
    out += jnp.dot(ssm_ref[...], w_ref[D_MLA:, :], preferred_element_type=F32)
    inv = lax.rsqrt(jnp.mean(out * out, axis=-1, keepdims=True) + NORM_EPS)
    o_ref[...] = x_ref[...] + out * inv * gpost_ref[...]


def _outproj(mla, ssm, layer, w_out, g_post, x, tm):
    L = x.shape[0]
    row = lambda n: pl.BlockSpec((tm, n), lambda i: (i, 0))
    return pl.pallas_call(
        _outproj_kernel,
        grid=(L // tm,),
        in_specs=[row(D_MLA), row(D_SSM), _layer(w_out, layer), _layer(g_post, layer), row(D_MODEL)],
        out_specs=row(D_MODEL),
        out_shape=jax.ShapeDtypeStruct((L, D_MODEL), F32),
        compiler_params=_params("parallel"),
        name="outproj",
    )(mla, ssm, w_out, g_post, x)


def kernel(x, positions, norm_pre, norm_post, w_in, q_norm, w_uq, kv_norm, w_ukv, ssm_a_re, ssm_a_im,
           ssm_b_re, ssm_b_im, ssm_c_re, ssm_c_im, ssm_d, ssm_log_step, w_glu, w_out):
    batch, L, _ = x.shape
    assert batch == 1
    depth = w_in.shape[0]
    tm = min(512, L)
    tq = min(2048, L)
    tk = tq // 2
    tt = S5_CHUNK

    rope = _rope_tables(positions, tk)

    w_in_b = w_in.astype(BF16)
    wq = w_uq.astype(BF16)
    w_ukv4 = w_ukv.astype(BF16).reshape(depth, KV_LORA, HEADS, NOPE + DV)
    wk = w_ukv4[..., :NOPE].reshape(depth, KV_LORA, HEADS * NOPE)
    wv = w_ukv4[..., NOPE:].reshape(depth, KV_LORA, HEADS * DV)
    w_glu_b = w_glu.astype(BF16)
    w_out_b = w_out.astype(BF16)

    ab_re, ab_im, bb_re, bb_im = _s5_prep(ssm_a_re, ssm_a_im, ssm_log_step, ssm_b_re, ssm_b_im)
    b_t = [b.transpose(0, 1, 3, 2) for b in (bb_re, bb_im)]
    slab_shape = (depth, N_SLABS, SUBLANES, LANES)
    a_re_s = ab_re.reshape(slab_shape)
    a_im_s = ab_im.reshape(slab_shape)
    c_t = [c.transpose(0, 1, 3, 2) for c in (ssm_c_re, -ssm_c_im)]

    def vec(p):
        return p.reshape(depth, 1, -1)

    h = x[0]
    for i in range(depth):
        cq, ckv, kr, gate_mla, u, gate_ssm = _inproj(
            h, i, vec(norm_pre), w_in_b, vec(q_norm), vec(kv_norm), tm)
        qt, k, vt = _qkv(cq, ckv, kr, rope, i, wq, wk, wv, tk)
        mla = _attention(qt, k, vt, gate_mla, tq, tk)
        ssm = _s5(u, gate_ssm, i, *b_t, a_re_s, a_im_s, *c_t, vec(ssm_d), w_glu_b, tt)
        h = _outproj(mla, ssm, i, w_out_b, vec(norm_post), h, tm)
    return h[None]
```
